```python
import math
import jax, jax.numpy as jnp
from jax import lax
import numpy as np

D_MODEL = 1024
BATCH = 8
SEQ = 4096
DEPTH = 2
DEC_BATCH = 32
DEC_SEQ = 8
PAST_LEN = 16384
PAGE_SIZE = 128

N_A_LAYERS = DEPTH // 2
N_B_LAYERS = DEPTH - N_A_LAYERS
MLSTM_HEADS = 4
MLSTM_INNER = 2 * D_MODEL
MLSTM_HEAD_DIM = MLSTM_INNER // MLSTM_HEADS
MLSTM_CHUNK = 64
GROUPS = ((128, 1), (512, 4), (2048, 16))
N_GROUPS = len(GROUPS)
GROUP_HEADS = 8
ATTN_HEAD_DIM = 64
N_Q_HEADS = N_GROUPS * GROUP_HEADS
ATTN_OUT = GROUP_HEADS * ATTN_HEAD_DIM
ATTN_BLOCK = 128
N_BUCKETS = 32
MAX_DISTANCE = 2048
EPS = 1e-6

kernel_name = 'yoco_mlstm_dilated_swa_step'


def _rmsnorm(x, g):
    xf = x.astype(jnp.float32)
    y = xf * lax.rsqrt(jnp.mean(xf * xf, axis=-1, keepdims=True) + EPS)
    return (y * g.astype(jnp.float32)).astype(x.dtype)


def _t5_bucket(dist):
    exact = N_BUCKETS // 2
    d = jnp.maximum(dist, 1).astype(jnp.float32)
    large = exact + (jnp.log(d / exact) / math.log(MAX_DISTANCE / exact) * (N_BUCKETS - exact)).astype(jnp.int32)
    return jnp.where(dist < exact, dist, jnp.minimum(large, N_BUCKETS - 1))


def _mlstm_scan(q, k, v, ig, lf, C0, n0, m0):
    B, T, H, Dh = q.shape
    L = min(MLSTM_CHUNK, T)
    pad = (-T) % L
    if pad:
        q = jnp.pad(q, ((0, 0), (0, pad), (0, 0), (0, 0)))
        k = jnp.pad(k, ((0, 0), (0, pad), (0, 0), (0, 0)))
        v = jnp.pad(v, ((0, 0), (0, pad), (0, 0), (0, 0)))
        ig = jnp.pad(ig, ((0, 0), (0, pad), (0, 0)), constant_values=-jnp.inf)
        lf = jnp.pad(lf, ((0, 0), (0, pad), (0, 0)))
    nc = (T + pad) // L

    def chunks(a):
        a = a.reshape((B, nc, L) + a.shape[2:])
        return jnp.moveaxis(a, (1, 2), (0, 3))

    causal = jnp.tril(jnp.ones((L, L), dtype=bool))

    def step(carry, xs):
        C, n, m = carry
        qc, kc, vc, igc, lfc = xs
        b = jnp.cumsum(lfc, axis=-1)
        log_inter = b + m[..., None]
        log_d = jnp.where(causal, b[..., :, None] - b[..., None, :] + igc[..., None, :], -jnp.inf)
        m_t = jnp.maximum(log_inter, jnp.max(log_d, axis=-1))
        dmat = jnp.exp(log_d - m_t[..., None])
        inter = jnp.exp(log_inter - m_t)
        s = jnp.einsum('bhtd,bhsd->bhts', qc, kc) * dmat
        num = jnp.einsum('bhts,bhse->bhte', s, vc) + inter[..., None] * jnp.einsum('bhtd,bhde->bhte', qc, C)
        den = jnp.sum(s, axis=-1) + inter * jnp.einsum('bhtd,bhd->bht', qc, n)
        h = num / jnp.maximum(jnp.abs(den), jnp.exp(-m_t))[..., None]
        m_new = m_t[..., -1]
        w_in = jnp.exp(b[..., -1:] - b + igc - m_new[..., None])
        decay = jnp.exp(b[..., -1] + m - m_new)
        C_new = decay[..., None, None] * C + jnp.einsum('bhsd,bhse->bhde', kc * w_in[..., None], vc)
        n_new = decay[..., None] * n + jnp.einsum('bhsd,bhs->bhd', kc, w_in)
        return (C_new, n_new, m_new), h

    (C, n, m), hs = lax.scan(step, (C0, n0, m0), (chunks(q), chunks(k), chunks(v), chunks(ig), chunks(lf)))
    hs = jnp.moveaxis(hs, (0, 3), (1, 2)).reshape(B, nc * L, H, Dh)[:, :T]
    return hs, (C, n, m)


def _mlstm_layer(x, C0, n0, m0, norm_g, w_in, b_gates, h_gain, w_out):
    f32 = jnp.float32
    B, T, _ = x.shape
    H, Dh, DI = MLSTM_HEADS, MLSTM_HEAD_DIM, MLSTM_INNER
    p = _rmsnorm(x, norm_g) @ w_in
    q, k, v, o, z = (p[..., i * DI:(i + 1) * DI] for i in range(5))
    gates = p[..., 5 * DI:].astype(f32) + b_gates.astype(f32)
    ig = gates[..., :H]
    lf = jax.nn.log_sigmoid(gates[..., H:])
    heads = lambda a: a.astype(f32).reshape(B, T, H, Dh)
    h, (C, n, m) = _mlstm_scan(heads(q), heads(k) * (Dh ** -0.5), heads(v), ig, lf,
                               C0.astype(f32), n0.astype(f32), m0.astype(f32))
    h = h * lax.rsqrt(jnp.mean(h * h, axis=-1, keepdims=True) + EPS)
    h = h.reshape(B, T, DI) * h_gain.astype(f32)
    y = (h * jax.nn.sigmoid(o.astype(f32)) * jax.nn.silu(z.astype(f32))).astype(x.dtype) @ w_out
    return x + y, C, n, m


def _shared_kv(x, norm_g, w_kv, k_gain):
    B, T, _ = x.shape
    p = _rmsnorm(x, norm_g) @ w_kv
    k = _rmsnorm(p[..., :N_Q_HEADS * ATTN_HEAD_DIM].reshape(B, T, N_Q_HEADS, ATTN_HEAD_DIM), k_gain)
    v = p[..., N_Q_HEADS * ATTN_HEAD_DIM:].reshape(B, T, N_Q_HEADS, ATTN_HEAD_DIM)
    return k, v


def _b_query(x, norm_g, w_in, q_gain):
    B, T, _ = x.shape
    p = _rmsnorm(x, norm_g) @ w_in
    q = _rmsnorm(p[..., :N_Q_HEADS * ATTN_HEAD_DIM].reshape(B, T, N_Q_HEADS, ATTN_HEAD_DIM), q_gain)
    z = p[..., N_Q_HEADS * ATTN_HEAD_DIM:]
    return q, z


def _group_prompt(q, k, v, bias_tab, win, dil):
    B, T, GH, HD = q.shape
    J = win // dil
    span = dil * ATTN_BLOCK
    Tp = -(-T // span) * span
    pad = Tp - T
    S = Tp // dil
    nb = S // ATTN_BLOCK
    padt = lambda a: jnp.pad(a, ((0, 0), (0, pad), (0, 0), (0, 0)))

    def blocks(a):
        a = a.reshape(B, S, dil, GH, HD).transpose(0, 2, 1, 3, 4)
        return a.reshape(B * dil, nb, ATTN_BLOCK, GH, HD)

    def with_prev(a):
        prev = jnp.concatenate([jnp.zeros_like(a[:, :1]), a[:, :-1]], axis=1)
        return jnp.concatenate([prev, a], axis=2)

    qb = blocks(padt(q))
    kc = with_prev(blocks(padt(k)))
    vc = with_prev(blocks(padt(v)))
    qi = jnp.arange(ATTN_BLOCK)[:, None]
    kj = jnp.arange(2 * ATTN_BLOCK)[None, :]
    rel = qi + ATTN_BLOCK - kj
    band = (rel >= 0) & (rel <= J)
    valid = band[None] & ((jnp.arange(nb) > 0)[:, None, None] | (kj >= ATTN_BLOCK)[None])
    bias = bias_tab.astype(jnp.float32)[_t5_bucket(jnp.clip(rel, 0, J) * dil)]
    s = jnp.einsum('bnqhd,bnkhd->bnhqk', qb, kc).astype(jnp.float32) * (HD ** -0.5)
    s = s + jnp.moveaxis(bias, -1, 0)[None, None]
    s = jnp.where(valid[None, :, None], s, -jnp.inf)
    lse = jax.nn.logsumexp(s, axis=-1)
    p = jnp.exp(s - lse[..., None])
    o = jnp.einsum('bnhqk,bnkhd->bnqhd', p.astype(vc.dtype), vc)
    o = o.reshape(B, dil, S, GH, HD).transpose(0, 2, 1, 3, 4).reshape(B, Tp, GH, HD)[:, :T]
    lse = jnp.moveaxis(lse, 2, 3).reshape(B, dil, S, GH).transpose(0, 2, 1, 3).reshape(B, Tp, GH)[:, :T]
    return o, lse


def _group_sample(q, k_new, v_new, buf, bias_tab, win, dil):
    B, S, GH, HD = q.shape
    L = buf.shape[1]
    J = win // dil
    k_all = jnp.concatenate([buf[:, :, 0].astype(k_new.dtype), k_new], axis=1)
    v_all = jnp.concatenate([buf[:, :, 1].astype(v_new.dtype), v_new], axis=1)
    j = jnp.arange(J + 1)
    idx = L + jnp.arange(S)[:, None] - dil * j[None, :]
    valid = idx >= 0
    idx = jnp.maximum(idx, 0)
    kg = k_all[:, idx]
    vg = v_all[:, idx]
    bias = bias_tab.astype(jnp.float32)[_t5_bucket(dil * j)]
    s = jnp.einsum('bshd,bsjhd->bshj', q, kg).astype(jnp.float32) * (HD ** -0.5) + bias.T[None, None]
    s = jnp.where(valid[None, :, None, :], s, -jnp.inf)
    lse = jax.nn.logsumexp(s, axis=-1)
    p = jnp.exp(s - lse[..., None])
    o = jnp.einsum('bshj,bsjhd->bshd', p.astype(vg.dtype), vg)
    return o, lse


def _merge(x, outs, lses, z, w_out):
    B, T, _ = x.shape
    w = jax.nn.softmax(jnp.stack(lses, axis=0), axis=0)
    o = jnp.sum(w[..., None] * jnp.stack(outs, axis=0).astype(jnp.float32), axis=0)
    y = (o.reshape(B, T, ATTN_OUT) * jax.nn.silu(z.astype(jnp.float32))).astype(x.dtype) @ w_out
    return x + y


def _dilated_layer_prompt(x, k, v, norm_g, w_in, q_gain, rel_bias, w_out):
    q, z = _b_query(x, norm_g, w_in, q_gain)
    outs, lses = [], []
    for g, (win, dil) in enumerate(GROUPS):
        sl = slice(g * GROUP_HEADS, (g + 1) * GROUP_HEADS)
        o, lse = _group_prompt(q[:, :, sl], k[:, :, sl], v[:, :, sl], rel_bias[:, sl], win, dil)
        outs.append(o)
        lses.append(lse)
    return _merge(x, outs, lses, z, w_out)


def _dilated_layer_sample(x, k, v, bufs, norm_g, w_in, q_gain, rel_bias, w_out):
    q, z = _b_query(x, norm_g, w_in, q_gain)
    outs, lses = [], []
    for g, (win, dil) in enumerate(GROUPS):
        sl = slice(g * GROUP_HEADS, (g + 1) * GROUP_HEADS)
        o, lse = _group_sample(q[:, :, sl], k[:, :, sl], v[:, :, sl], bufs[g], rel_bias[:, sl], win, dil)
        outs.append(o)
        lses.append(lse)
    return _merge(x, outs, lses, z, w_out)


def _window_rows(k, v, g):
    win = GROUPS[g][0]
    sl = slice(g * GROUP_HEADS, (g + 1) * GROUP_HEADS)
    rows = min(win, k.shape[1])
    return jnp.stack([k[:, -rows:, sl], v[:, -rows:, sl]], axis=2)


def _new_rows(k, v, g):
    sl = slice(g * GROUP_HEADS, (g + 1) * GROUP_HEADS)
    return jnp.stack([k[:, :, sl], v[:, :, sl]], axis=2)


def setup_inputs(seed: int = 0) -> dict:
    key = jax.random.key(seed)
    ks = jax.random.split(key, 24)
    f32 = jnp.float32
    nrm = lambda kk, shape, scale=1.0: scale * jax.random.normal(kk, shape, f32)
    H, Dh, DI = MLSTM_HEADS, MLSTM_HEAD_DIM, MLSTM_INNER
    QW = N_Q_HEADS * ATTN_HEAD_DIM
    wl = [min(w, PAST_LEN) for w, _ in GROUPS]
    b_gates = jnp.concatenate([nrm(ks[11], (N_A_LAYERS, H), 0.1),
                               jnp.linspace(3.0, 6.0, H)[None, :] + nrm(ks[12], (N_A_LAYERS, H), 0.1)], axis=-1)
    return {
        'x_prompt': nrm(ks[0], (BATCH, SEQ, D_MODEL)),
        'x_sample': nrm(ks[1], (DEC_BATCH, DEC_SEQ, D_MODEL)),
        'state_mlstm_C': nrm(ks[2], (N_A_LAYERS, DEC_BATCH, H, Dh, Dh), Dh ** -0.5),
        'state_mlstm_n': nrm(ks[3], (N_A_LAYERS, DEC_BATCH, H, Dh), Dh ** -0.5),
        'state_mlstm_m': nrm(ks[4], (N_A_LAYERS, DEC_BATCH, H)),
        'cache_kv_w128': nrm(ks[5], (DEC_BATCH, wl[0], 2, GROUP_HEADS, ATTN_HEAD_DIM)),
        'cache_kv_w512': nrm(ks[6], (DEC_BATCH, wl[1], 2, GROUP_HEADS, ATTN_HEAD_DIM)),
        'cache_kv_w2048': nrm(ks[7], (DEC_BATCH, wl[2], 2, GROUP_HEADS, ATTN_HEAD_DIM)),
        'norm_a': 1.0 + nrm(ks[8], (N_A_LAYERS, D_MODEL), 0.02),
        'w_in_a': nrm(ks[9], (N_A_LAYERS, D_MODEL, 5 * DI + 2 * H), D_MODEL ** -0.5),
        'b_gates_a': b_gates,
        'hnorm_a': 1.0 + nrm(ks[10], (N_A_LAYERS, DI), 0.02),
        'w_out_a': nrm(ks[13], (N_A_LAYERS, DI, D_MODEL), DI ** -0.5),
        'norm_kv': 1.0 + nrm(ks[14], (D_MODEL,), 0.02),
        'w_kv': nrm(ks[15], (D_MODEL, 2 * QW), D_MODEL ** -0.5),
        'k_norm': 1.0 + nrm(ks[16], (ATTN_HEAD_DIM,), 0.02),
        'norm_b': 1.0 + nrm(ks[17], (N_B_LAYERS, D_MODEL), 0.02),
        'w_in_b': nrm(ks[18], (N_B_LAYERS, D_MODEL, QW + ATTN_OUT), D_MODEL ** -0.5),
        'q_norm': 1.0 + nrm(ks[19], (N_B_LAYERS, ATTN_HEAD_DIM), 0.02),
        'rel_bias': nrm(ks[20], (N_BUCKETS, N_Q_HEADS), 0.5),
        'w_out_b': nrm(ks[21], (N_B_LAYERS, ATTN_OUT, D_MODEL), ATTN_OUT ** -0.5),
    }


def reference(x_prompt, x_sample, state_mlstm_C, state_mlstm_n, state_mlstm_m,
              cache_kv_w128, cache_kv_w512, cache_kv_w2048,
              norm_a, w_in_a, b_gates_a, hnorm_a, w_out_a,
              norm_kv, w_kv, k_norm,
              norm_b, w_in_b, q_norm, rel_bias, w_out_b):
    f32 = jnp.float32
    bufs = (cache_kv_w128, cache_kv_w512, cache_kv_w2048)
    n_p = x_prompt.shape[0]
    xp, xs = x_prompt, x_sample
    Cp, Np, Mp, Cs, Ns, Ms = [], [], [], [], [], []
    kp = vp = ks = vs = None
    for layer in range(DEPTH):
        if layer < N_A_LAYERS:
            a = (norm_a[layer], w_in_a[layer], b_gates_a[layer], hnorm_a[layer], w_out_a[layer])
            zero_C = jnp.zeros((n_p, MLSTM_HEADS, MLSTM_HEAD_DIM, MLSTM_HEAD_DIM), f32)
            zero_n = jnp.zeros((n_p, MLSTM_HEADS, MLSTM_HEAD_DIM), f32)
            zero_m = jnp.zeros((n_p, MLSTM_HEADS), f32)
            xp, c, n, m = _mlstm_layer(xp, zero_C, zero_n, zero_m, *a)
            Cp.append(c)
            Np.append(n)
            Mp.append(m)
            xs, c, n, m = _mlstm_layer(xs, state_mlstm_C[layer], state_mlstm_n[layer], state_mlstm_m[layer], *a)
            Cs.append(c)
            Ns.append(n)
            Ms.append(m)
            if layer == N_A_LAYERS - 1:
                kp, vp = _shared_kv(xp, norm_kv, w_kv, k_norm)
                ks, vs = _shared_kv(xs, norm_kv, w_kv, k_norm)
        else:
            i = layer - N_A_LAYERS
            b = (norm_b[i], w_in_b[i], q_norm[i], rel_bias, w_out_b[i])
            xp = _dilated_layer_prompt(xp, kp, vp, *b)
            xs = _dilated_layer_sample(xs, ks, vs, bufs, *b)
    kv128_p, kv512_p, kv2048_p = [_window_rows(kp, vp, g) for g in range(N_GROUPS)]
    kv128_s, kv512_s, kv2048_s = [_new_rows(ks, vs, g) for g in range(N_GROUPS)]
    return (xp, xs, jnp.stack(Cp), jnp.stack(Np), jnp.stack(Mp), jnp.stack(Cs), jnp.stack(Ns), jnp.stack(Ms),
            kv128_p, kv512_p, kv2048_p, kv128_s, kv512_s, kv2048_s)
```

```python
import functools
import math

import numpy as np
import jax
import jax.numpy as jnp
from jax import lax
from jax.experimental import pallas as pl
from jax.experimental.pallas import tpu as pltpu

F32 = jnp.float32
BF16 = jnp.bfloat16
HIGHEST = lax.Precision.HIGHEST

EPS = 1e-6
MLSTM_HEADS = 4
GROUPS = ((128, 1), (512, 4), (2048, 16))
GROUP_HEADS = 8
ATTN_HEAD_DIM = 64
ATTN_BLOCK = 128
N_BUCKETS = 32
MAX_DISTANCE = 2048

LANES = 128
MXU_DIM = 256
VMEM_LIMIT_BYTES = 56 * 1024 * 1024

NT_DIMS = (((1,), (1,)), ((), ()))


def _params(sem):
    return pltpu.CompilerParams(dimension_semantics=sem, vmem_limit_bytes=VMEM_LIMIT_BYTES)


def _rms_scale(xf):
    return lax.rsqrt(jnp.mean(xf * xf, axis=-1, keepdims=True) + EPS)


def _sigmoid(x):
    return 1.0 / (1.0 + jnp.exp(-x))


def _split_bf16(a):
    hi = a.astype(BF16)
    lo = (a - hi.astype(F32)).astype(BF16)
    return hi, lo


def _inproj_kernel(x_ref, g_ref, w_ref, cs_ref, wg_ref, bg_ref,
                   p_ref, gc_ref, gr_ref, xn_ref, *, n_heads):
    j = pl.program_id(1)

    @pl.when(j == 0)
    def _():
        xf = x_ref[...]
        xn = xf * _rms_scale(xf) * g_ref[...]
        xn_ref[...] = xn.astype(BF16)
        gates = jnp.dot(xn, wg_ref[...], precision=HIGHEST,
                        preferred_element_type=F32) + bg_ref[...]
        lane = lax.broadcasted_iota(jnp.int32, gates.shape, 1)
        logsig = jnp.minimum(gates, 0.0) - jnp.log(1.0 + jnp.exp(-jnp.abs(gates)))
        gcol = jnp.where(lane < n_heads, gates, jnp.where(lane < 2 * n_heads, logsig, 0.0))
        gc_ref[...] = gcol
        gr_ref[...] = gcol.T[:8, :]

    acc = jnp.dot(xn_ref[...], w_ref[...], preferred_element_type=F32)
    p_ref[...] = (acc * cs_ref[...]).astype(BF16)


def _inproj(x2, g, w, colscale, wg, bg, *, tm, tn, n_heads):
    m, d = x2.shape
    n = w.shape[1]
    assert m % tm == 0 and n % tn == 0 and 2 * n_heads <= 8
    return pl.pallas_call(
        functools.partial(_inproj_kernel, n_heads=n_heads),
        grid=(m // tm, n // tn),
        in_specs=[
            pl.BlockSpec((tm, d), lambda i, j: (i, 0)),
            pl.BlockSpec((1, d), lambda i, j: (0, 0)),
            pl.BlockSpec((d, tn), lambda i, j: (0, j)),
            pl.BlockSpec((1, tn), lambda i, j: (0, j)),
            pl.BlockSpec((d, LANES), lambda i, j: (0, 0)),
            pl.BlockSpec((1, LANES), lambda i, j: (0, 0)),
        ],
        out_specs=[
            pl.BlockSpec((tm, tn), lambda i, j: (i, j)),
            pl.BlockSpec((tm, LANES), lambda i, j: (i, 0)),
            pl.BlockSpec((8, tm), lambda i, j: (0, i)),
        ],
        out_shape=[
            jax.ShapeDtypeStruct((m, n), BF16),
            jax.ShapeDtypeStruct((m, LANES), F32),
            jax.ShapeDtypeStruct((8, m), F32),
        ],
        scratch_shapes=[pltpu.VMEM((tm, d), BF16)],
        compiler_params=_params(("parallel", "arbitrary")),
        name="inproj",
    )(x2, g, w, colscale, wg, bg)


def _kproj_t_kernel(x_ref, g_ref, wt_ref, o_ref, xn_ref, *, scale):
    @pl.when(pl.program_id(2) == 0)
    def _():
        xf = x_ref[...]
        xn_ref[...] = (xf * _rms_scale(xf) * g_ref[...]).astype(BF16)

    acc = lax.dot_general(wt_ref[...], xn_ref[...], NT_DIMS, preferred_element_type=F32)
    o_ref[...] = (acc * scale).astype(BF16)


def _kproj_t(x3, g, wt, *, tm, tn, scale):
    b, t, d = x3.shape
    n = wt.shape[0]
    assert t % tm == 0 and n % tn == 0
    return pl.pallas_call(
        functools.partial(_kproj_t_kernel, scale=scale),
        grid=(b, t // tm, n // tn),
        in_specs=[
            pl.BlockSpec((None, tm, d), lambda bi, i, j: (bi, i, 0)),
            pl.BlockSpec((1, d), lambda bi, i, j: (0, 0)),
            pl.BlockSpec((tn, d), lambda bi, i, j: (j, 0)),
        ],
        out_specs=pl.BlockSpec((None, tn, tm), lambda bi, i, j: (bi, j, i)),
        out_shape=jax.ShapeDtypeStruct((b, n, t), BF16),
        scratch_shapes=[pltpu.VMEM((tm, d), BF16)],
        compiler_params=_params(("parallel", "parallel", "arbitrary")),
        name="kproj_t",
    )(x3, g, wt)


def _scan_kernel(*refs, chunk, n_heads, dh, n_chunks, has_state):
    L, H = chunk, n_heads
    dext = dh + LANES
    (q_ref, kt_ref, v_ref, o_ref, z_ref, gc_ref, gr_ref, x_ref, wout_ref, hg_ref) = refs[:10]
    pos = 10
    if has_state:
        c0_ref, n0_ref, m0_ref = refs[pos:pos + 3]
        pos += 3
    xo_ref, cout_ref, nout_ref, mout_ref = refs[pos:pos + 4]
    cext_ref, cb_ref, m_ref = refs[pos + 4:pos + 7]
    c = pl.program_id(1)

    @pl.when(c == 0)
    def _():
        if has_state:
            lane0 = lax.broadcasted_iota(jnp.int32, (dh, LANES), 1) == 0
            for h in range(H):
                ncol = jnp.broadcast_to(n0_ref[h:h + 1, :], (LANES, dh)).T
                cext_ref[h, :, :dh] = c0_ref[h]
                cext_ref[h, :, dh:] = jnp.where(lane0, ncol, 0.0)
            m_ref[...] = m0_ref[...]
        else:
            cext_ref[...] = jnp.zeros(cext_ref.shape, F32)
            m_ref[...] = jnp.zeros(m_ref.shape, F32)
        cb_ref[...] = cext_ref[...].astype(BF16)

    gc = gc_ref[...]
    gr = gr_ref[...]
    row = lax.broadcasted_iota(jnp.int32, (L, L), 0)
    col = lax.broadcasted_iota(jnp.int32, (L, L), 1)
    causal = row >= col
    bc_all = jnp.dot(causal.astype(F32), gc, precision=HIGHEST, preferred_element_type=F32)
    br_all = jnp.dot(gr, (row <= col).astype(F32), precision=HIGHEST, preferred_element_type=F32)
    ones_col = (lax.broadcasted_iota(jnp.int32, (L, LANES), 1) == 0).astype(BF16)

    y = jnp.zeros((L, wout_ref.shape[1]), F32)
    for h in range(H):
        sl = slice(h * dh, (h + 1) * dh)
        q = q_ref[:, sl]
        kt = kt_ref[sl, :]
        vext = jnp.concatenate([v_ref[:, sl], ones_col], axis=1)
        ig_r = gr[h:h + 1, :]
        b_c = bc_all[:, H + h:H + h + 1]
        b_r = br_all[H + h:H + h + 1, :]
        m_prev = m_ref[h:h + 1, 0:1]

        log_d = jnp.where(causal, b_c - b_r + ig_r, -jnp.inf)
        log_inter = b_c + m_prev
        m_t = jnp.maximum(log_inter, jnp.max(log_d, axis=1, keepdims=True))
        dmat = jnp.exp(log_d - m_t)
        inter = jnp.exp(log_inter - m_t)
        s = jnp.dot(q, kt, preferred_element_type=F32) * dmat
        numden = (jnp.dot(s.astype(BF16), vext, preferred_element_type=F32)
                  + inter * jnp.dot(q, cb_ref[h], preferred_element_type=F32))
        num = numden[:, :dh]
        den = numden[:, dh:dh + 1]
        hh = num / jnp.maximum(jnp.abs(den), jnp.exp(-m_t))

        b_last = b_r[:, L - 1:L]
        a_r = b_last - b_r + ig_r
        m_new = jnp.maximum(b_last + m_prev, jnp.max(a_r, axis=1, keepdims=True))
        w_r = jnp.exp(a_r - m_new)
        decay = jnp.exp(b_last + m_prev - m_new)
        ktw = (kt.astype(F32) * w_r).astype(BF16)
        c_new = decay * cext_ref[h] + jnp.dot(ktw, vext, preferred_element_type=F32)
        cext_ref[h] = c_new
        cb_ref[h] = c_new.astype(BF16)
        m_ref[h:h + 1, :] = jnp.broadcast_to(m_new, (1, LANES))

        hn = hh * lax.rsqrt(jnp.mean(hh * hh, axis=1, keepdims=True) + EPS)
        zf = z_ref[:, sl].astype(F32)
        gate = _sigmoid(o_ref[:, sl].astype(F32)) * (zf * _sigmoid(zf))
        hg = (hn * hg_ref[:, sl] * gate).astype(BF16)
        y = y + jnp.dot(hg, wout_ref[sl, :], preferred_element_type=F32)

    xo_ref[...] = x_ref[...] + y

    @pl.when(c == n_chunks - 1)
    def _():
        for h in range(H):
            cout_ref[h] = cext_ref[h, :, :dh]
            nout_ref[h:h + 1, :] = cext_ref[h, :, dh:].T[0:1, :]
        mout_ref[...] = m_ref[...]


def _scan(p3, col_idx, kt3, gc, gr, x3, wout, hgain, state, *, chunk, n_heads):
    b, t, _ = p3.shape
    dh = kt3.shape[1] // n_heads
    di = n_heads * dh
    d = x3.shape[2]
    nc = t // chunk
    assert t % chunk == 0
    has_state = state is not None
    qi, vi, oi, zi = col_idx

    def pspec(ci):
        return pl.BlockSpec((None, chunk, di), lambda bi, c, ci=ci: (bi, c, ci))

    in_specs = [
        pspec(qi),
        pl.BlockSpec((None, di, chunk), lambda bi, c: (bi, 0, c)),
        pspec(vi), pspec(oi), pspec(zi),
        pl.BlockSpec((chunk, LANES), lambda bi, c: (bi * nc + c, 0)),
        pl.BlockSpec((8, chunk), lambda bi, c: (0, bi * nc + c)),
        pl.BlockSpec((None, chunk, d), lambda bi, c: (bi, c, 0)),
        pl.BlockSpec((di, d), lambda bi, c: (0, 0)),
        pl.BlockSpec((1, di), lambda bi, c: (0, 0)),
    ]
    args = [p3, kt3, p3, p3, p3, gc, gr, x3, wout, hgain]
    if has_state:
        c0, n0, m0 = state
        in_specs += [
            pl.BlockSpec((None, n_heads, dh, dh), lambda bi, c: (bi, 0, 0, 0)),
            pl.BlockSpec((None, n_heads, dh), lambda bi, c: (bi, 0, 0)),
            pl.BlockSpec((None, 8, LANES), lambda bi, c: (bi, 0, 0)),
        ]
        args += [c0, n0, m0]
    return pl.pallas_call(
        functools.partial(_scan_kernel, chunk=chunk, n_heads=n_heads, dh=dh,
                          n_chunks=nc, has_state=has_state),
        grid=(b, nc),
        in_specs=in_specs,
        out_specs=[
            pl.BlockSpec((None, chunk, d), lambda bi, c: (bi, c, 0)),
            pl.BlockSpec((None, n_heads, dh, dh), lambda bi, c: (bi, 0, 0, 0)),
            pl.BlockSpec((None, n_heads, dh), lambda bi, c: (bi, 0, 0)),
            pl.BlockSpec((None, 8, LANES), lambda bi, c: (bi, 0, 0)),
        ],
        out_shape=[
            jax.ShapeDtypeStruct((b, t, d), F32),
            jax.ShapeDtypeStruct((b, n_heads, dh, dh), F32),
            jax.ShapeDtypeStruct((b, n_heads, dh), F32),
            jax.ShapeDtypeStruct((b, 8, LANES), F32),
        ],
        scratch_shapes=[
            pltpu.VMEM((n_heads, dh, dh + LANES), F32),
            pltpu.VMEM((n_heads, dh, dh + LANES), BF16),
            pltpu.VMEM((8, LANES), F32),
        ],
        compiler_params=_params(("parallel", "arbitrary")),
        name="mlstm_scan",
    )(*args)


def _headnorm(a):
    n = a.shape[1]
    r = lax.broadcasted_iota(jnp.int32, (MXU_DIM, MXU_DIM), 0)
    c = lax.broadcasted_iota(jnp.int32, (MXU_DIM, MXU_DIM), 1)
    same_head = (lax.shift_right_logical(r, 6) == lax.shift_right_logical(c, 6)).astype(BF16)
    parts = []
    for c0 in range(0, n, MXU_DIM):
        blk = a[:, c0:c0 + MXU_DIM]
        hi, lo = _split_bf16(blk * blk)
        ss = (jnp.dot(hi, same_head, preferred_element_type=F32)
              + jnp.dot(lo, same_head, preferred_element_type=F32))
        parts.append(blk * lax.rsqrt(ss * (1.0 / ATTN_HEAD_DIM) + EPS))
    return jnp.concatenate(parts, axis=1)


def _proj_headnorm_kernel(*refs, n_norm, scale, tails, tm, n_tiles):
    x_ref, g_ref, w_ref, hg_ref = refs[:4]
    a_ref, r_ref = refs[4:6]
    tail_refs = refs[6:]
    xf = x_ref[...]
    xn = (xf * _rms_scale(xf) * g_ref[...]).astype(BF16)
    p = jnp.dot(xn, w_ref[...], preferred_element_type=F32)
    a = _headnorm(p[:, :n_norm]) * hg_ref[...]
    if scale != 1.0:
        a = a * scale
    r = p[:, n_norm:]
    a_ref[...] = a.astype(BF16)
    r_ref[...] = r.astype(BF16)
    i = pl.program_id(1)
    gw = GROUP_HEADS * ATTN_HEAD_DIM
    for g, (rows, first_tile) in enumerate(tails):
        t_ref = tail_refs[g]
        cs = slice(g * gw, (g + 1) * gw)

        @pl.when(i >= first_tile)
        def _(t_ref=t_ref, cs=cs, rows=rows):
            if rows >= tm:
                t_ref[:, :gw] = a[:, cs]
                t_ref[:, gw:] = r[:, cs]
            else:
                t_ref[:, :gw] = a[tm - rows:, cs]
                t_ref[:, gw:] = r[tm - rows:, cs]


def _proj_headnorm(x3, g, w, hgain, *, n_norm, scale, tm, tail_rows=None):
    b, t, d = x3.shape
    n = w.shape[1]
    assert t % tm == 0 and n_norm % MXU_DIM == 0
    n_tiles = t // tm
    gw = GROUP_HEADS * ATTN_HEAD_DIM
    tails = []
    out_specs = [
        pl.BlockSpec((None, tm, n_norm), lambda bi, i: (bi, i, 0)),
        pl.BlockSpec((None, tm, n - n_norm), lambda bi, i: (bi, i, 0)),
    ]
    out_shape = [
        jax.ShapeDtypeStruct((b, t, n_norm), BF16),
        jax.ShapeDtypeStruct((b, t, n - n_norm), BF16),
    ]
    for rows in (tail_rows or ()):
        if rows >= tm:
            assert rows % tm == 0
            first = n_tiles - rows // tm
            blk = tm
        else:
            first = n_tiles - 1
            blk = rows
        tails.append((rows, first))
        out_specs.append(pl.BlockSpec(
            (None, blk, 2 * gw), lambda bi, i, first=first: (bi, jnp.maximum(i - first, 0), 0)))
        out_shape.append(jax.ShapeDtypeStruct((b, rows, 2 * gw), F32))
    return pl.pallas_call(
        functools.partial(_proj_headnorm_kernel, n_norm=n_norm, scale=scale,
                          tails=tuple(tails), tm=tm, n_tiles=n_tiles),
        grid=(b, n_tiles),
        in_specs=[
            pl.BlockSpec((None, tm, d), lambda bi, i: (bi, i, 0)),
            pl.BlockSpec((1, d), lambda bi, i: (0, 0)),
            pl.BlockSpec((d, n), lambda bi, i: (0, 0)),
            pl.BlockSpec((1, n_norm), lambda bi, i: (0, 0)),
        ],
        out_specs=out_specs,
        out_shape=out_shape,
        compiler_params=_params(("parallel", "arbitrary")),
        name="proj_headnorm",
    )(x3, g, w, hgain)


def _attn_prompt_kernel(q_ref, kp_ref, kc_ref, vp_ref, vc_ref, bias_ref, o_ref, lse_ref):
    hd = ATTN_HEAD_DIM
    q = q_ref[...]
    kcat = jnp.concatenate([kp_ref[...], kc_ref[...]], axis=0)
    vcat = jnp.concatenate([vp_ref[...], vc_ref[...]], axis=0)
    lane = lax.broadcasted_iota(jnp.int32, (ATTN_BLOCK, LANES), 1)
    lse_all = jnp.zeros((ATTN_BLOCK, LANES), F32)
    outs = []
    for h in range(GROUP_HEADS):
        hs = slice(h * hd, (h + 1) * hd)
        s = lax.dot_general(q[:, hs], kcat[:, hs], NT_DIMS, preferred_element_type=F32)
        s = s + bias_ref[h]
        m = jnp.max(s, axis=1, keepdims=True)
        p = jnp.exp(s - m)
        l = jnp.sum(p, axis=1, keepdims=True)
        outs.append(jnp.dot(p.astype(BF16), vcat[:, hs], preferred_element_type=F32) / l)
        lse_all = jnp.where(lane == h, m + jnp.log(l), lse_all)
    o_ref[...] = jnp.concatenate(outs, axis=1)
    lse_ref[...] = lse_all


def _attn_prompt(q, k, v, bias, g, dil):
    b, t, qw = q.shape
    gw = GROUP_HEADS * ATTN_HEAD_DIM
    ng = qw // gw
    s = t // dil
    nb = s // ATTN_BLOCK
    assert t % (dil * ATTN_BLOCK) == 0
    q2 = q.reshape(b, s, dil * qw)
    k2 = k.reshape(b, s, dil * qw)
    v2 = v.reshape(b, s, dil * qw)
    cur = pl.BlockSpec((None, ATTN_BLOCK, gw), lambda bi, r, j: (bi, j, r * ng + g))
    prev = pl.BlockSpec((None, ATTN_BLOCK, gw),
                        lambda bi, r, j: (bi, jnp.maximum(j - 1, 0), r * ng + g))
    o, lse = pl.pallas_call(
        _attn_prompt_kernel,
        grid=(b, dil, nb),
        in_specs=[
            cur, prev, cur, prev, cur,
            pl.BlockSpec((None, GROUP_HEADS, ATTN_BLOCK, 2 * ATTN_BLOCK),
                         lambda bi, r, j: (jnp.minimum(j, 1), 0, 0, 0)),
        ],
        out_specs=[
            pl.BlockSpec((None, ATTN_BLOCK, gw), lambda bi, r, j: (bi, j, r)),
            pl.BlockSpec((None, ATTN_BLOCK, LANES), lambda bi, r, j: (bi, j, r)),
        ],
        out_shape=[
            jax.ShapeDtypeStruct((b, s, dil * gw), F32),
            jax.ShapeDtypeStruct((b, s, dil * LANES), F32),
        ],
        compiler_params=_params(("parallel", "parallel", "arbitrary")),
        name="attn_prompt_g%d" % g,
    )(q2, k2, k2, v2, v2, bias)
    return o.reshape(b, t, gw), lse.reshape(b, t, LANES)


def _head_expand_matrix():
    r = lax.broadcasted_iota(jnp.int32, (LANES, GROUP_HEADS * ATTN_HEAD_DIM), 0)
    c = lax.broadcasted_iota(jnp.int32, (LANES, GROUP_HEADS * ATTN_HEAD_DIM), 1)
    return (r == lax.shift_right_logical(c, 6)).astype(BF16)


def _merge_out_kernel(o0_ref, o1_ref, o2_ref, l0_ref, l1_ref, l2_ref, z_ref, x_ref, w_ref, y_ref):
    ls = [l0_ref[...], l1_ref[...], l2_ref[...]]
    os_ = [o0_ref[...], o1_ref[...], o2_ref[...]]
    lmax = jnp.maximum(jnp.maximum(ls[0], ls[1]), ls[2])
    es = [jnp.exp(l - lmax) for l in ls]
    tot = es[0] + es[1] + es[2]
    expand = _head_expand_matrix()
    o = jnp.zeros(os_[0].shape, F32)
    for e, og in zip(es, os_):
        hi, lo = _split_bf16(e / tot)
        wexp = (jnp.dot(hi, expand, preferred_element_type=F32)
                + jnp.dot(lo, expand, preferred_element_type=F32))
        o = o + wexp * og
    zf = z_ref[...].astype(F32)
    a = (o * (zf * _sigmoid(zf))).astype(BF16)
    y_ref[...] = x_ref[...] + jnp.dot(a, w_ref[...], preferred_element_type=F32)


def _merge_out(outs, lses, z2, x2, w, *, tm):
    m, d = x2.shape
    gw = w.shape[0]
    assert m % tm == 0
    row = lambda width: pl.BlockSpec((tm, width), lambda i: (i, 0))
    return pl.pallas_call(
        _merge_out_kernel,
        grid=(m // tm,),
        in_specs=[row(gw)] * 3 + [row(LANES)] * 3 + [row(gw), row(d),
                                                     pl.BlockSpec((gw, d), lambda i: (0, 0))],
        out_specs=row(d),
        out_shape=jax.ShapeDtypeStruct((m, d), F32),
        compiler_params=_params(("parallel",)),
        name="merge_out",
    )(*outs, *lses, z2, x2, w)


def _attn_sample_kernel(q_ref, kn_ref, vn_ref, z_ref, c0_ref, c1_ref, c2_ref,
                        b0_ref, b1_ref, b2_ref, a_ref, ks0, vs0, ks1, vs1, ks2, vs2, *, s_new):
    gw = GROUP_HEADS * ATTN_HEAD_DIM
    rows = GROUP_HEADS * s_new
    r = lax.broadcasted_iota(jnp.int32, (rows, gw), 0)
    c = lax.broadcasted_iota(jnp.int32, (rows, gw), 1)
    head_mask = (lax.shift_right_logical(r, int(math.log2(s_new)))
                 == lax.shift_right_logical(c, int(math.log2(ATTN_HEAD_DIM))))
    caches = ((c0_ref, b0_ref, ks0, vs0), (c1_ref, b1_ref, ks1, vs1), (c2_ref, b2_ref, ks2, vs2))
    pad_rows = jnp.zeros((LANES - s_new, gw), F32)
    outs, lses = [], []
    for g, (c_ref, b_ref, ks, vs) in enumerate(caches):
        buf_len = c_ref.shape[0]
        cs = slice(g * gw, (g + 1) * gw)
        ks[:buf_len, :] = c_ref[:, :gw].astype(BF16)
        vs[:buf_len, :] = c_ref[:, gw:].astype(BF16)
        ks[buf_len:, :] = jnp.concatenate([kn_ref[:, cs].astype(F32), pad_rows], axis=0).astype(BF16)
        vs[buf_len:, :] = jnp.concatenate([vn_ref[:, cs].astype(F32), pad_rows], axis=0).astype(BF16)
        qg = q_ref[:, cs].astype(F32)
        qbd = jnp.where(head_mask, jnp.concatenate([qg] * GROUP_HEADS, axis=0), 0.0).astype(BF16)
        s = lax.dot_general(qbd, ks[...], NT_DIMS, preferred_element_type=F32) + b_ref[...]
        m = jnp.max(s, axis=1, keepdims=True)
        p = jnp.exp(s - m)
        l = jnp.sum(p, axis=1, keepdims=True)
        outs.append(jnp.dot(p.astype(BF16), vs[...], preferred_element_type=F32) / l)
        lses.append(m + jnp.log(l))
    lmax = jnp.maximum(jnp.maximum(lses[0], lses[1]), lses[2])
    es = [jnp.exp(l - lmax) for l in lses]
    tot = es[0] + es[1] + es[2]
    o = jnp.zeros((rows, gw), F32)
    for e, og in zip(es, outs):
        o = o + (e / tot) * og
    o = jnp.where(head_mask, o, 0.0)
    folded = o[0:s_new, :]
    for h in range(1, GROUP_HEADS):
        folded = folded + o[h * s_new:(h + 1) * s_new, :]
    zf = z_ref[...].astype(F32)
    a_ref[...] = (folded * (zf * _sigmoid(zf))).astype(BF16)


def _attn_sample(q, kn, vn, z, caches, biases):
    b, s_new, qw = q.shape
    gw = GROUP_HEADS * ATTN_HEAD_DIM
    assert s_new % 8 == 0
    cache2 = [cb.reshape(b, cb.shape[1], 2 * gw) for cb in caches]
    in_specs = [
        pl.BlockSpec((None, s_new, qw), lambda bi: (bi, 0, 0)),
        pl.BlockSpec((None, s_new, qw), lambda bi: (bi, 0, 0)),
        pl.BlockSpec((None, s_new, qw), lambda bi: (bi, 0, 0)),
        pl.BlockSpec((None, s_new, gw), lambda bi: (bi, 0, 0)),
    ]
    for cb in cache2:
        in_specs.append(pl.BlockSpec((None, cb.shape[1], 2 * gw), lambda bi: (bi, 0, 0)))
    for bt in biases:
        in_specs.append(pl.BlockSpec(bt.shape, lambda bi: (0, 0)))
    scratch = []
    for cb in cache2:
        scratch += [pltpu.VMEM((cb.shape[1] + LANES, gw), BF16)] * 2
    return pl.pallas_call(
        functools.partial(_attn_sample_kernel, s_new=s_new),
        grid=(b,),
        in_specs=in_specs,
        out_specs=pl.BlockSpec((None, s_new, gw), lambda bi: (bi, 0, 0)),
        out_shape=jax.ShapeDtypeStruct((b, s_new, gw), BF16),
        scratch_shapes=scratch,
        compiler_params=_params(("arbitrary",)),
        name="attn_sample",
    )(q, kn, vn, z, *cache2, *biases)


def _matmul_residual_kernel(a_ref, x_ref, w_ref, y_ref):
    y_ref[...] = x_ref[...] + jnp.dot(a_ref[...], w_ref[...], preferred_element_type=F32)


def _matmul_residual(a2, x2, w, *, tm):
    m, d = x2.shape
    kdim = a2.shape[1]
    assert m % tm == 0
    return pl.pallas_call(
        _matmul_residual_kernel,
        grid=(m // tm,),
        in_specs=[
            pl.BlockSpec((tm, kdim), lambda i: (i, 0)),
            pl.BlockSpec((tm, d), lambda i: (i, 0)),
            pl.BlockSpec((kdim, d), lambda i: (0, 0)),
        ],
        out_specs=pl.BlockSpec((tm, d), lambda i: (i, 0)),
        out_shape=jax.ShapeDtypeStruct((m, d), F32),
        compiler_params=_params(("parallel",)),
        name="matmul_residual",
    )(a2, x2, w)


def _t5_bucket_np(dist):
    exact = N_BUCKETS // 2
    d = np.maximum(dist, 1).astype(np.float32)
    large = exact + (np.log(d / np.float32(exact)) / np.float32(math.log(MAX_DISTANCE / exact))
                     * np.float32(N_BUCKETS - exact)).astype(np.int32)
    return np.where(dist < exact, dist, np.minimum(large, N_BUCKETS - 1)).astype(np.int32)


def _prompt_bias(rel_bias_g, win, dil):
    jmax = win // dil
    qi = np.arange(ATTN_BLOCK)[:, None]
    kj = np.arange(2 * ATTN_BLOCK)[None, :]
    rel = qi + ATTN_BLOCK - kj
    band = (rel >= 0) & (rel <= jmax)
    bucket = _t5_bucket_np(np.clip(rel, 0, jmax) * dil)
    bias = jnp.moveaxis(rel_bias_g.astype(F32)[bucket], -1, 0)
    rest = jnp.where(band[None], bias, -jnp.inf)
    first = jnp.where((band & (kj >= ATTN_BLOCK))[None], bias, -jnp.inf)
    return jnp.stack([first, rest], axis=0)


def _sample_bias(rel_bias_g, win, dil, buf_len, s_new):
    jmax = win // dil
    qpos = buf_len + np.arange(s_new)[:, None]
    kpos = np.arange(buf_len + LANES)[None, :]
    dist = qpos - kpos
    ok = (dist >= 0) & (dist % dil == 0) & (dist // dil <= jmax) & (kpos < buf_len + s_new)
    bucket = _t5_bucket_np(np.clip(dist, 0, jmax * dil))
    bias = jnp.moveaxis(rel_bias_g.astype(F32)[bucket], -1, 0)
    table = jnp.where(ok[None], bias, -jnp.inf)
    return table.reshape(GROUP_HEADS * s_new, buf_len + LANES)


def _layer_a(x3, state, weights, *, chunk, pad_to):
    norm_a, w_cols, colscale, wkt, wg, bg, hgain, wout, col_idx, k_idx = weights
    b, t, d = x3.shape
    H = MLSTM_HEADS
    m = b * t
    di = wout.shape[0]
    dh = di // H
    tm = min(1024, m)
    p, gc, gr = _inproj(x3.reshape(m, d), norm_a, w_cols, colscale, wg, bg,
                        tm=tm, tn=1024, n_heads=H)
    if pad_to == t:
        kt3 = _kproj_t(x3, norm_a, wkt, tm=min(1024, t), tn=1024, scale=dh ** -0.5)
        p3 = p.reshape(b, t, -1)
        xin = x3
    else:
        extra = pad_to - t
        p3 = p.reshape(b, t, -1)
        k3 = p3[:, :, k_idx * di:(k_idx + 1) * di]
        kt3 = jnp.pad(jnp.swapaxes(k3, 1, 2), ((0, 0), (0, 0), (0, extra)))
        p3 = jnp.pad(p3, ((0, 0), (0, extra), (0, 0)))
        xin = jnp.pad(x3, ((0, 0), (0, extra), (0, 0)))
        lane = np.arange(LANES)
        pad_col = np.where(lane < H, -np.inf, 0.0).astype(np.float32)
        gc = jnp.concatenate([gc.reshape(b, t, LANES),
                              jnp.broadcast_to(pad_col, (b, extra, LANES))], axis=1).reshape(-1, LANES)
        pad_row = np.where(np.arange(8) < H, -np.inf, 0.0).astype(np.float32)[:, None, None]
        gr = jnp.concatenate([gr.reshape(8, b, t),
                              jnp.broadcast_to(pad_row, (8, b, extra))], axis=2).reshape(8, -1)
    xo, c_out, n_out, m_out = _scan(p3, col_idx, kt3, gc, gr, xin, wout, hgain, state,
                                    chunk=chunk, n_heads=H)
    return xo[:, :t], c_out, n_out, m_out[:, :H, 0]


def kernel(x_prompt, x_sample, state_mlstm_C, state_mlstm_n, state_mlstm_m, cache_kv_w128, cache_kv_w512, cache_kv_w2048, norm_a, w_in_a, b_gates_a, hnorm_a, w_out_a, norm_kv, w_kv, k_norm, norm_b, w_in_b, q_norm, rel_bias, w_out_b):
    H = MLSTM_HEADS
    bp, tp, d = x_prompt.shape
    bs, ts, _ = x_sample.shape
    di = w_out_a.shape[1]
    dh = di // H
    gw = GROUP_HEADS * ATTN_HEAD_DIM
    qw = len(GROUPS) * gw
    caches = (cache_kv_w128, cache_kv_w512, cache_kv_w2048)
    assert norm_a.shape[0] == 1 and norm_b.shape[0] == 1, "one mLSTM layer, one attention layer"
    for cb, (win, _) in zip(caches, GROUPS):
        assert cb.shape[1] == win, "window buffers must hold a full window"

    w_a = w_in_a[0]
    wq, wk, wv, wo, wz = (w_a[:, i * di:(i + 1) * di] for i in range(5))
    wg = jnp.pad(w_a[:, 5 * di:], ((0, 0), (0, LANES - 2 * H)))
    bg = jnp.pad(b_gates_a[0].astype(F32), (0, LANES - 2 * H))[None, :]
    na = norm_a[0].astype(F32)[None, :]
    hgain = hnorm_a[0].astype(F32)[None, :]
    wout_a = w_out_a[0].astype(BF16)
    w_prompt = jnp.concatenate([wq, wv, wo, wz], axis=1).astype(BF16)
    w_sample = w_a[:, :5 * di].astype(BF16)
    wkt = wk.T.astype(BF16)
    ones_p = jnp.ones((1, 4 * di), F32)
    scale_s = jnp.concatenate([jnp.ones((di,), F32), jnp.full((di,), dh ** -0.5, F32),
                               jnp.ones((3 * di,), F32)])[None, :]
    weights_p = (na, w_prompt, ones_p, wkt, wg, bg, hgain, wout_a, (0, 1, 2, 3), None)
    weights_s = (na, w_sample, scale_s, None, wg, bg, hgain, wout_a, (0, 2, 3, 4), 1)

    xp1, c_p, n_p, m_p = _layer_a(x_prompt, None, weights_p, chunk=256, pad_to=tp)
    m0 = jnp.pad(jnp.broadcast_to(state_mlstm_m[0].astype(F32)[:, :, None], (bs, H, LANES)),
                 ((0, 0), (0, 8 - H), (0, 0)))
    state_s = (state_mlstm_C[0].astype(F32), state_mlstm_n[0].astype(F32), m0)
    xs1, c_s, n_s, m_s = _layer_a(x_sample, state_s, weights_s, chunk=LANES, pad_to=LANES)

    nkv = norm_kv.astype(F32)[None, :]
    wkv = w_kv.astype(BF16)
    kgain = jnp.tile(k_norm.astype(F32), qw // ATTN_HEAD_DIM)[None, :]
    rows_p = [min(win, tp) for win, _ in GROUPS]
    kp, vp, kv128_p, kv512_p, kv2048_p = _proj_headnorm(
        xp1, nkv, wkv, kgain, n_norm=qw, scale=1.0, tm=512, tail_rows=rows_p)
    ks, vs, kv128_s, kv512_s, kv2048_s = _proj_headnorm(
        xs1, nkv, wkv, kgain, n_norm=qw, scale=1.0, tm=ts, tail_rows=[ts] * len(GROUPS))

    nb_ = norm_b[0].astype(F32)[None, :]
    wb = w_in_b[0].astype(BF16)
    qgain = jnp.tile(q_norm[0].astype(F32), qw // ATTN_HEAD_DIM)[None, :]
    wout_b = w_out_b[0].astype(BF16)
    qscale = ATTN_HEAD_DIM ** -0.5
    qp, zp = _proj_headnorm(xp1, nb_, wb, qgain, n_norm=qw, scale=qscale, tm=512)
    qs, zs = _proj_headnorm(xs1, nb_, wb, qgain, n_norm=qw, scale=qscale, tm=ts)

    outs, lses = [], []
    for g, (win, dil) in enumerate(GROUPS):
        bias = _prompt_bias(rel_bias[:, g * GROUP_HEADS:(g + 1) * GROUP_HEADS], win, dil)
        o, lse = _attn_prompt(qp, kp, vp, bias, g, dil)
        outs.append(o.reshape(bp * tp, gw))
        lses.append(lse.reshape(bp * tp, LANES))
    y_p = _merge_out(outs, lses, zp.reshape(bp * tp, gw), xp1.reshape(bp * tp, d), wout_b,
                     tm=512).reshape(bp, tp, d)

    sbias = [_sample_bias(rel_bias[:, g * GROUP_HEADS:(g + 1) * GROUP_HEADS], win, dil,
                          caches[g].shape[1], ts) for g, (win, dil) in enumerate(GROUPS)]
    a_s = _attn_sample(qs, ks, vs, zs, caches, sbias)
    y_s = _matmul_residual(a_s.reshape(bs * ts, gw), xs1.reshape(bs * ts, d), wout_b,
                           tm=bs * ts).reshape(bs, ts, d)

    kv5 = lambda a: a.reshape(a.shape[0], a.shape[1], 2, GROUP_HEADS, ATTN_HEAD_DIM)
    return (y_p, y_s, c_p[None], n_p[None], m_p[None], c_s[None], n_s[None], m_s[None],
            kv5(kv128_p), kv5(kv512_p), kv5(kv2048_p), kv5(kv128_s), kv5(kv512_s), kv5(kv2048_s))
```

```python
import functools
import math

import numpy as np
import jax
import jax.numpy as jnp
from jax import lax
from jax.experimental import pallas as pl
from jax.experimental.pallas import tpu as pltpu

F32 = jnp.float32
BF16 = jnp.bfloat16
HIGHEST = lax.Precision.HIGHEST

EPS = 1e-6
MLSTM_HEADS = 4
GROUPS = ((128, 1), (512, 4), (2048, 16))
GROUP_HEADS = 8
ATTN_HEAD_DIM = 64
ATTN_BLOCK = 128
N_BUCKETS = 32
MAX_DISTANCE = 2048

LANES = 128
MXU_DIM = 256
VMEM_LIMIT_BYTES = 56 * 1024 * 1024

NT_DIMS = (((1,), (1,)), ((), ()))


def _params(sem):
    return pltpu.CompilerParams(dimension_semantics=sem, vmem_limit_bytes=VMEM_LIMIT_BYTES)


def _rms_scale(xf):
    return lax.rsqrt(jnp.mean(xf * xf, axis=-1, keepdims=True) + EPS)


def _sigmoid(x):
    return 1.0 / (1.0 + jnp.exp(-x))


def _split_bf16(a):
    hi = a.astype(BF16)
    lo = (a - hi.astype(F32)).astype(BF16)
    return hi, lo


def _inproj_kernel(x_ref, g_ref, w_ref, cs_ref, wg_ref, bg_ref,
                   p_ref, gc_ref, gr_ref, xn_ref, *, n_heads):
    j = pl.program_id(1)

    @pl.when(j == 0)
    def _():
        xf = x_ref[...]
        xn = xf * _rms_scale(xf) * g_ref[...]
        xn_ref[...] = xn.astype(BF16)
        gates = jnp.dot(xn, wg_ref[...], precision=HIGHEST,
                        preferred_element_type=F32) + bg_ref[...]
        lane = lax.broadcasted_iota(jnp.int32, gates.shape, 1)
        logsig = jnp.minimum(gates, 0.0) - jnp.log(1.0 + jnp.exp(-jnp.abs(gates)))
        gcol = jnp.where(lane < n_heads, gates, jnp.where(lane < 2 * n_heads, logsig, 0.0))
        gc_ref[...] = gcol
        gr_ref[...] = gcol.T[:8, :]

    acc = jnp.dot(xn_ref[...], w_ref[...], preferred_element_type=F32)
    p_ref[...] = (acc * cs_ref[...]).astype(BF16)


def _inproj(x2, g, w, colscale, wg, bg, *, tm, tn, n_heads):
    m, d = x2.shape
    n = w.shape[1]
    assert m % tm == 0 and n % tn == 0 and 2 * n_heads <= 8
    return pl.pallas_call(
        functools.partial(_inproj_kernel, n_heads=n_heads),
        grid=(m // tm, n // tn),
        in_specs=[
            pl.BlockSpec((tm, d), lambda i, j: (i, 0)),
            pl.BlockSpec((1, d), lambda i, j: (0, 0)),
            pl.BlockSpec((d, tn), lambda i, j: (0, j)),
            pl.BlockSpec((1, tn), lambda i, j: (0, j)),
            pl.BlockSpec((d, LANES), lambda i, j: (0, 0)),
            pl.BlockSpec((1, LANES), lambda i, j: (0, 0)),
        ],
        out_specs=[
            pl.BlockSpec((tm, tn), lambda i, j: (i, j)),
            pl.BlockSpec((tm, LANES), lambda i, j: (i, 0)),
            pl.BlockSpec((8, tm), lambda i, j: (0, i)),
        ],
        out_shape=[
            jax.ShapeDtypeStruct((m, n), BF16),
            jax.ShapeDtypeStruct((m, LANES), F32),
            jax.ShapeDtypeStruct((8, m), F32),
        ],
        scratch_shapes=[pltpu.VMEM((tm, d), BF16)],
        compiler_params=_params(("parallel", "arbitrary")),
        name="inproj",
    )(x2, g, w, colscale, wg, bg)


def _kproj_t_kernel(x_ref, g_ref, wt_ref, o_ref, xn_ref, *, scale):
    @pl.when(pl.program_id(2) == 0)
    def _():
        xf = x_ref[...]
        xn_ref[...] = (xf * _rms_scale(xf) * g_ref[...]).astype(BF16)

    acc = lax.dot_general(wt_ref[...], xn_ref[...], NT_DIMS, preferred_element_type=F32)
    o_ref[...] = (acc * scale).astype(BF16)


def _kproj_t(x3, g, wt, *, tm, tn, scale):
    b, t, d = x3.shape
    n = wt.shape[0]
    assert t % tm == 0 and n % tn == 0
    return pl.pallas_call(
        functools.partial(_kproj_t_kernel, scale=scale),
        grid=(b, t // tm, n // tn),
        in_specs=[
            pl.BlockSpec((None, tm, d), lambda bi, i, j: (bi, i, 0)),
            pl.BlockSpec((1, d), lambda bi, i, j: (0, 0)),
            pl.BlockSpec((tn, d), lambda bi, i, j: (j, 0)),
        ],
        out_specs=pl.BlockSpec((None, tn, tm), lambda bi, i, j: (bi, j, i)),
        out_shape=jax.ShapeDtypeStruct((b, n, t), BF16),
        scratch_shapes=[pltpu.VMEM((tm, d), BF16)],
        compiler_params=_params(("parallel", "parallel", "arbitrary")),
        name="kproj_t",
    )(x3, g, wt)


def _scan_kernel(*refs, chunk, n_heads, dh, n_chunks, has_state):
    L, H = chunk, n_heads
    dext = dh + LANES
    (q_ref, kt_ref, v_ref, o_ref, z_ref, gc_ref, gr_ref, x_ref, wout_ref, hg_ref) = refs[:10]
    pos = 10
    if has_state:
        c0_ref, n0_ref, m0_ref = refs[pos:pos + 3]
        pos += 3
    xo_ref, cout_ref, nout_ref, mout_ref = refs[pos:pos + 4]
    cext_ref, cb_ref, m_ref = refs[pos + 4:pos + 7]
    c = pl.program_id(1)

    @pl.when(c == 0)
    def _():
        if has_state:
            lane0 = lax.broadcasted_iota(jnp.int32, (dh, LANES), 1) == 0
            for h in range(H):
                ncol = jnp.broadcast_to(n0_ref[h:h + 1, :], (LANES, dh)).T
                cext_ref[h, :, :dh] = c0_ref[h]
                cext_ref[h, :, dh:] = jnp.where(lane0, ncol, 0.0)
            m_ref[...] = m0_ref[...]
        else:
            cext_ref[...] = jnp.zeros(cext_ref.shape, F32)
            m_ref[...] = jnp.zeros(m_ref.shape, F32)
        cb_ref[...] = cext_ref[...].astype(BF16)

    gc = gc_ref[...]
    gr = gr_ref[...]
    row = lax.broadcasted_iota(jnp.int32, (L, L), 0)
    col = lax.broadcasted_iota(jnp.int32, (L, L), 1)
    causal = row >= col
    bc_all = jnp.dot(causal.astype(F32), gc, precision=HIGHEST, preferred_element_type=F32)
    br_all = jnp.dot(gr, (row <= col).astype(F32), precision=HIGHEST, preferred_element_type=F32)
    ones_col = (lax.broadcasted_iota(jnp.int32, (L, LANES), 1) == 0).astype(BF16)

    y = jnp.zeros((L, wout_ref.shape[1]), F32)
    for h in range(H):
        sl = slice(h * dh, (h + 1) * dh)
        q = q_ref[:, sl]
        kt = kt_ref[sl, :]
        vext = jnp.concatenate([v_ref[:, sl], ones_col], axis=1)
        ig_r = gr[h:h + 1, :]
        b_c = bc_all[:, H + h:H + h + 1]
        b_r = br_all[H + h:H + h + 1, :]
        m_prev = m_ref[h:h + 1, 0:1]

        log_d = jnp.where(causal, b_c - b_r + ig_r, -jnp.inf)
        log_inter = b_c + m_prev
        m_t = jnp.maximum(log_inter, jnp.max(log_d, axis=1, keepdims=True))
        dmat = jnp.exp(log_d - m_t)
        inter = jnp.exp(log_inter - m_t)
        s = jnp.dot(q, kt, preferred_element_type=F32) * dmat
        numden = (jnp.dot(s.astype(BF16), vext, preferred_element_type=F32)
                  + inter * jnp.dot(q, cb_ref[h], preferred_element_type=F32))
        num = numden[:, :dh]
        den = numden[:, dh:dh + 1]
        hh = num / jnp.maximum(jnp.abs(den), jnp.exp(-m_t))

        b_last = b_r[:, L - 1:L]
        a_r = b_last - b_r + ig_r
        m_new = jnp.maximum(b_last + m_prev, jnp.max(a_r, axis=1, keepdims=True))
        w_r = jnp.exp(a_r - m_new)
        decay = jnp.exp(b_last + m_prev - m_new)
        ktw = (kt.astype(F32) * w_r).astype(BF16)
        c_new = decay * cext_ref[h] + jnp.dot(ktw, vext, preferred_element_type=F32)
        cext_ref[h] = c_new
        cb_ref[h] = c_new.astype(BF16)
        m_ref[h:h + 1, :] = jnp.broadcast_to(m_new, (1, LANES))

        hn = hh * lax.rsqrt(jnp.mean(hh * hh, axis=1, keepdims=True) + EPS)
        zf = z_ref[:, sl].astype(F32)
        gate = _sigmoid(o_ref[:, sl].astype(F32)) * (zf * _sigmoid(zf))
        hg = (hn * hg_ref[:, sl] * gate).astype(BF16)
        y = y + jnp.dot(hg, wout_ref[sl, :], preferred_element_type=F32)

    xo_ref[...] = x_ref[...] + y

    @pl.when(c == n_chunks - 1)
    def _():
        for h in range(H):
            cout_ref[h] = cext_ref[h, :, :dh]
            nout_ref[h:h + 1, :] = cext_ref[h, :, dh:].T[0:1, :]
        mout_ref[...] = m_ref[...]


def _scan(p3, col_idx, kt3, gc, gr, x3, wout, hgain, state, *, chunk, n_heads):
    b, t, _ = p3.shape
    dh = kt3.shape[1] // n_heads
    di = n_heads * dh
    d = x3.shape[2]
    nc = t // chunk
    assert t % chunk == 0
    has_state = state is not None
    qi, vi, oi, zi = col_idx

    def pspec(ci):
        return pl.BlockSpec((None, chunk, di), lambda bi, c, ci=ci: (bi, c, ci))

    in_specs = [
        pspec(qi),
        pl.BlockSpec((None, di, chunk), lambda bi, c: (bi, 0, c)),
        pspec(vi), pspec(oi), pspec(zi),
        pl.BlockSpec((chunk, LANES), lambda bi, c: (bi * nc + c, 0)),
        pl.BlockSpec((8, chunk), lambda bi, c: (0, bi * nc + c)),
        pl.BlockSpec((None, chunk, d), lambda bi, c: (bi, c, 0)),
        pl.BlockSpec((di, d), lambda bi, c: (0, 0)),
        pl.BlockSpec((1, di), lambda bi, c: (0, 0)),
    ]
    args = [p3, kt3, p3, p3, p3, gc, gr, x3, wout, hgain]
    if has_state:
        c0, n0, m0 = state
        in_specs += [
            pl.BlockSpec((None, n_heads, dh, dh), lambda bi, c: (bi, 0, 0, 0)),
            pl.BlockSpec((None, n_heads, dh), lambda bi, c: (bi, 0, 0)),
            pl.BlockSpec((None, 8, LANES), lambda bi, c: (bi, 0, 0)),
        ]
        args += [c0, n0, m0]
    return pl.pallas_call(
        functools.partial(_scan_kernel, chunk=chunk, n_heads=n_heads, dh=dh,
                          n_chunks=nc, has_state=has_state),
        grid=(b, nc),
        in_specs=in_specs,
        out_specs=[
            pl.BlockSpec((None, chunk, d), lambda bi, c: (bi, c, 0)),
            pl.BlockSpec((None, n_heads, dh, dh), lambda bi, c: (bi, 0, 0, 0)),
            pl.BlockSpec((None, n_heads, dh), lambda bi, c: (bi, 0, 0)),
            pl.BlockSpec((None, 8, LANES), lambda bi, c: (bi, 0, 0)),
        ],
        out_shape=[
            jax.ShapeDtypeStruct((b, t, d), F32),
            jax.ShapeDtypeStruct((b, n_heads, dh, dh), F32),
            jax.ShapeDtypeStruct((b, n_heads, dh), F32),
            jax.ShapeDtypeStruct((b, 8, LANES), F32),
        ],
        scratch_shapes=[
            pltpu.VMEM((n_heads, dh, dh + LANES), F32),
            pltpu.VMEM((n_heads, dh, dh + LANES), BF16),
            pltpu.VMEM((8, LANES), F32),
        ],
        compiler_params=_params(("parallel", "arbitrary")),
        name="mlstm_scan",
    )(*args)


def _headnorm(a):
    n = a.shape[1]
    r = lax.broadcasted_iota(jnp.int32, (MXU_DIM, MXU_DIM), 0)
    c = lax.broadcasted_iota(jnp.int32, (MXU_DIM, MXU_DIM), 1)
    same_head = (lax.shift_right_logical(r, 6) == lax.shift_right_logical(c, 6)).astype(BF16)
    parts = []
    for c0 in range(0, n, MXU_DIM):
        blk = a[:, c0:c0 + MXU_DIM]
        hi, lo = _split_bf16(blk * blk)
        ss = (jnp.dot(hi, same_head, preferred_element_type=F32)
              + jnp.dot(lo, same_head, preferred_element_type=F32))
        parts.append(blk * lax.rsqrt(ss * (1.0 / ATTN_HEAD_DIM) + EPS))
    return jnp.concatenate(parts, axis=1)


def _store_by_residue(ref, val, scr, dil):
    rows, width = val.shape
    if dil == 1:
        ref[0] = val.astype(BF16)
        return
    for c in range(width // LANES):
        scr[c] = val[:, c * LANES:(c + 1) * LANES]
    for r in range(dil):
        parts = [scr[c, pl.ds(r, rows // dil, stride=dil), :] for c in range(width // LANES)]
        ref[r] = jnp.concatenate(parts, axis=1).astype(BF16)


def _proj_headnorm_kernel(*refs, n_norm, scale, tails, tm, dils, rest_by_group):
    x_ref, g_ref, w_ref, hg_ref = refs[:4]
    gw = GROUP_HEADS * ATTN_HEAD_DIM
    n_groups = n_norm // gw
    pos = 4
    if dils is None:
        a_refs, r_refs = refs[pos:pos + 1], refs[pos + 1:pos + 2]
        pos += 2
    else:
        a_refs = refs[pos:pos + n_groups]
        pos += n_groups
        n_rest = n_groups if rest_by_group else 1
        r_refs = refs[pos:pos + n_rest]
        pos += n_rest
    tail_refs = refs[pos:pos + len(tails)]
    scr = refs[pos + len(tails)] if dils is not None else None

    xf = x_ref[...]
    xn = (xf * _rms_scale(xf) * g_ref[...]).astype(BF16)
    p = jnp.dot(xn, w_ref[...], preferred_element_type=F32)
    a = _headnorm(p[:, :n_norm]) * hg_ref[...]
    if scale != 1.0:
        a = a * scale
    r = p[:, n_norm:]
    if dils is None:
        a_refs[0][...] = a.astype(BF16)
        r_refs[0][...] = r.astype(BF16)
    else:
        for g, dil in enumerate(dils):
            cs = slice(g * gw, (g + 1) * gw)
            _store_by_residue(a_refs[g], a[:, cs], scr, dil)
            if rest_by_group:
                _store_by_residue(r_refs[g], r[:, cs], scr, dil)
        if not rest_by_group:
            r_refs[0][...] = r.astype(BF16)
    i = pl.program_id(1)
    for g, (rows, first_tile) in enumerate(tails):
        t_ref = tail_refs[g]
        cs = slice(g * gw, (g + 1) * gw)

        @pl.when(i >= first_tile)
        def _(t_ref=t_ref, cs=cs, rows=rows):
            if rows >= tm:
                t_ref[:, :gw] = a[:, cs]
                t_ref[:, gw:] = r[:, cs]
            else:
                t_ref[:, :gw] = a[tm - rows:, cs]
                t_ref[:, gw:] = r[tm - rows:, cs]


def _proj_headnorm(x3, g, w, hgain, *, n_norm, scale, tm, tail_rows=None, dils=None,
                   rest_by_group=False):
    b, t, d = x3.shape
    n = w.shape[1]
    assert t % tm == 0 and n_norm % MXU_DIM == 0
    n_tiles = t // tm
    gw = GROUP_HEADS * ATTN_HEAD_DIM
    out_specs, out_shape, scratch = [], [], []

    def natural(width):
        out_specs.append(pl.BlockSpec((None, tm, width), lambda bi, i: (bi, i, 0)))
        out_shape.append(jax.ShapeDtypeStruct((b, t, width), BF16))

    def by_residue(dil):
        assert tm % (dil * 16) == 0
        out_specs.append(pl.BlockSpec((None, dil, tm // dil, gw), lambda bi, i: (bi, 0, i, 0)))
        out_shape.append(jax.ShapeDtypeStruct((b, dil, t // dil, gw), BF16))

    if dils is None:
        natural(n_norm)
        natural(n - n_norm)
    else:
        assert len(dils) * gw == n_norm
        for dil in dils:
            by_residue(dil)
        if rest_by_group:
            assert n - n_norm == n_norm
            for dil in dils:
                by_residue(dil)
        else:
            natural(n - n_norm)
        scratch.append(pltpu.VMEM((gw // LANES, tm, LANES), F32))
    tails = []
    for rows in (tail_rows or ()):
        if rows >= tm:
            assert rows % tm == 0
            first = n_tiles - rows // tm
            blk = tm
        else:
            first = n_tiles - 1
            blk = rows
        tails.append((rows, first))
        out_specs.append(pl.BlockSpec(
            (None, blk, 2 * gw), lambda bi, i, first=first: (bi, jnp.maximum(i - first, 0), 0)))
        out_shape.append(jax.ShapeDtypeStruct((b, rows, 2 * gw), F32))
    return pl.pallas_call(
        functools.partial(_proj_headnorm_kernel, n_norm=n_norm, scale=scale, tails=tuple(tails),
                          tm=tm, dils=dils, rest_by_group=rest_by_group),
        grid=(b, n_tiles),
        in_specs=[
            pl.BlockSpec((None, tm, d), lambda bi, i: (bi, i, 0)),
            pl.BlockSpec((1, d), lambda bi, i: (0, 0)),
            pl.BlockSpec((d, n), lambda bi, i: (0, 0)),
            pl.BlockSpec((1, n_norm), lambda bi, i: (0, 0)),
        ],
        out_specs=out_specs,
        out_shape=out_shape,
        scratch_shapes=scratch,
        compiler_params=_params(("parallel", "arbitrary")),
        name="proj_headnorm",
    )(x3, g, w, hgain)


def _attn_prompt_kernel(q_ref, kp_ref, kc_ref, vp_ref, vc_ref, bias_ref, o_ref, lse_ref, *, nq):
    hd = ATTN_HEAD_DIM
    blk = ATTN_BLOCK
    first = jnp.where(pl.program_id(2) == 0, 0, 1)
    lane = lax.broadcasted_iota(jnp.int32, (blk, LANES), 1)
    for jb in range(nq):
        q = q_ref[jb * blk:(jb + 1) * blk, :]
        if jb == 0:
            kcat = jnp.concatenate([kp_ref[...], kc_ref[0:blk, :]], axis=0)
            vcat = jnp.concatenate([vp_ref[...], vc_ref[0:blk, :]], axis=0)
        else:
            kcat = kc_ref[(jb - 1) * blk:(jb + 1) * blk, :]
            vcat = vc_ref[(jb - 1) * blk:(jb + 1) * blk, :]
        lse_all = jnp.zeros((blk, LANES), F32)
        outs = []
        for h in range(GROUP_HEADS):
            hs = slice(h * hd, (h + 1) * hd)
            bias = bias_ref[first, h] if jb == 0 else bias_ref[1, h]
            s = lax.dot_general(q[:, hs], kcat[:, hs], NT_DIMS, preferred_element_type=F32) + bias
            m = jnp.max(s, axis=1, keepdims=True)
            p = jnp.exp(s - m)
            l = jnp.sum(p, axis=1, keepdims=True)
            outs.append(jnp.dot(p.astype(BF16), vcat[:, hs], preferred_element_type=F32) / l)
            lse_all = jnp.where(lane == h, m + jnp.log(l), lse_all)
        o_ref[jb * blk:(jb + 1) * blk, :] = jnp.concatenate(outs, axis=1)
        lse_ref[jb * blk:(jb + 1) * blk, :] = lse_all


def _attn_prompt(q, k, v, bias, g):
    b, dil, s, gw = q.shape
    nb = s // ATTN_BLOCK
    assert s % ATTN_BLOCK == 0
    nq = min(4, nb)
    assert nb % nq == 0
    rows = nq * ATTN_BLOCK
    cur = pl.BlockSpec((None, None, rows, gw), lambda bi, r, j: (bi, r, j, 0))
    prev = pl.BlockSpec((None, None, ATTN_BLOCK, gw),
                        lambda bi, r, j: (bi, r, jnp.maximum(j * nq - 1, 0), 0))
    return pl.pallas_call(
        functools.partial(_attn_prompt_kernel, nq=nq),
        grid=(b, dil, nb // nq),
        in_specs=[
            cur, prev, cur, prev, cur,
            pl.BlockSpec(bias.shape, lambda bi, r, j: (0, 0, 0, 0)),
        ],
        out_specs=[
            pl.BlockSpec((None, None, rows, gw), lambda bi, r, j: (bi, r, j, 0)),
            pl.BlockSpec((None, None, rows, LANES), lambda bi, r, j: (bi, r, j, 0)),
        ],
        out_shape=[
            jax.ShapeDtypeStruct((b, dil, s, gw), F32),
            jax.ShapeDtypeStruct((b, dil, s, LANES), F32),
        ],
        compiler_params=_params(("parallel", "parallel", "arbitrary")),
        name="attn_prompt_g%d" % g,
    )(q, k, k, v, v, bias)


def _head_expand_matrix():
    r = lax.broadcasted_iota(jnp.int32, (LANES, GROUP_HEADS * ATTN_HEAD_DIM), 0)
    c = lax.broadcasted_iota(jnp.int32, (LANES, GROUP_HEADS * ATTN_HEAD_DIM), 1)
    return (r == lax.shift_right_logical(c, 6)).astype(BF16)


def _load_token_order(ref, scr):
    dil, per, width = ref.shape
    if dil == 1:
        return ref[0]
    n_tiles = width // LANES
    for r in range(dil):
        val = ref[r]
        for c in range(n_tiles):
            scr[c, pl.ds(r, per, stride=dil), :] = val[:, c * LANES:(c + 1) * LANES]
    return jnp.concatenate([scr[c] for c in range(n_tiles)], axis=1)


def _merge_out_kernel(o0_ref, o1_ref, o2_ref, l0_ref, l1_ref, l2_ref, z_ref, x_ref, w_ref, y_ref,
                      *scratch):
    o_refs, l_refs = (o0_ref, o1_ref, o2_ref), (l0_ref, l1_ref, l2_ref)
    os_, ls, k = [], [], 0
    for o_ref, l_ref in zip(o_refs, l_refs):
        if o_ref.shape[0] == 1:
            os_.append(o_ref[0])
            ls.append(l_ref[0])
        else:
            os_.append(_load_token_order(o_ref, scratch[k]))
            ls.append(_load_token_order(l_ref, scratch[k + 1]))
            k += 2
    lmax = jnp.maximum(jnp.maximum(ls[0], ls[1]), ls[2])
    es = [jnp.exp(l - lmax) for l in ls]
    tot = es[0] + es[1] + es[2]
    expand = _head_expand_matrix()
    o = jnp.zeros(os_[0].shape, F32)
    for e, og in zip(es, os_):
        hi, lo = _split_bf16(e / tot)
        wexp = (jnp.dot(hi, expand, preferred_element_type=F32)
                + jnp.dot(lo, expand, preferred_element_type=F32))
        o = o + wexp * og
    zf = z_ref[...].astype(F32)
    a = (o * (zf * _sigmoid(zf))).astype(BF16)
    y_ref[...] = x_ref[...] + jnp.dot(a, w_ref[...], preferred_element_type=F32)


def _merge_out(outs, lses, z3, x3, w, *, tm):
    b, t, d = x3.shape
    gw = w.shape[0]
    assert t % tm == 0
    row = lambda width: pl.BlockSpec((None, tm, width), lambda bi, i: (bi, i, 0))
    by_residue = lambda a: pl.BlockSpec((None, a.shape[1], tm // a.shape[1], a.shape[3]),
                                        lambda bi, i: (bi, 0, i, 0))
    scratch = []
    for o in outs:
        if o.shape[1] > 1:
            scratch += [pltpu.VMEM((gw // LANES, tm, LANES), F32), pltpu.VMEM((1, tm, LANES), F32)]
    return pl.pallas_call(
        _merge_out_kernel,
        grid=(b, t // tm),
        in_specs=[by_residue(o) for o in outs] + [by_residue(l) for l in lses]
        + [row(gw), row(d), pl.BlockSpec((gw, d), lambda bi, i: (0, 0))],
        out_specs=row(d),
        out_shape=jax.ShapeDtypeStruct((b, t, d), F32),
        scratch_shapes=scratch,
        compiler_params=_params(("parallel", "parallel")),
        name="merge_out",
    )(*outs, *lses, z3, x3, w)


def _attn_sample_kernel(q_ref, kn_ref, vn_ref, z_ref, c0_ref, c1_ref, c2_ref,
                        b0_ref, b1_ref, b2_ref, a_ref, ks0, vs0, ks1, vs1, ks2, vs2, *, s_new):
    gw = GROUP_HEADS * ATTN_HEAD_DIM
    rows = GROUP_HEADS * s_new
    r = lax.broadcasted_iota(jnp.int32, (rows, gw), 0)
    c = lax.broadcasted_iota(jnp.int32, (rows, gw), 1)
    head_mask = (lax.shift_right_logical(r, int(math.log2(s_new)))
                 == lax.shift_right_logical(c, int(math.log2(ATTN_HEAD_DIM))))
    caches = ((c0_ref, b0_ref, ks0, vs0), (c1_ref, b1_ref, ks1, vs1), (c2_ref, b2_ref, ks2, vs2))
    pad_rows = jnp.zeros((LANES - s_new, gw), F32)
    outs, lses = [], []
    for g, (c_ref, b_ref, ks, vs) in enumerate(caches):
        buf_len = c_ref.shape[2]
        cs = slice(g * gw, (g + 1) * gw)
        ks[:, :buf_len] = c_ref[0].astype(BF16)
        vs[:, :buf_len] = c_ref[1].astype(BF16)
        ks[:, buf_len:] = jnp.concatenate([kn_ref[:, cs].astype(F32), pad_rows], axis=0).T.astype(BF16)
        vs[:, buf_len:] = jnp.concatenate([vn_ref[:, cs].astype(F32), pad_rows], axis=0).T.astype(BF16)
        qg = q_ref[:, cs].astype(F32)
        qbd = jnp.where(head_mask, jnp.concatenate([qg] * GROUP_HEADS, axis=0), 0.0).astype(BF16)
        s = jnp.dot(qbd, ks[...], preferred_element_type=F32) + b_ref[...]
        m = jnp.max(s, axis=1, keepdims=True)
        p = jnp.exp(s - m)
        l = jnp.sum(p, axis=1, keepdims=True)
        outs.append(lax.dot_general(p.astype(BF16), vs[...], NT_DIMS, preferred_element_type=F32) / l)
        lses.append(m + jnp.log(l))
    lmax = jnp.maximum(jnp.maximum(lses[0], lses[1]), lses[2])
    es = [jnp.exp(l - lmax) for l in lses]
    tot = es[0] + es[1] + es[2]
    o = jnp.zeros((rows, gw), F32)
    for e, og in zip(es, outs):
        o = o + (e / tot) * og
    o = jnp.where(head_mask, o, 0.0)
    folded = o[0:s_new, :]
    for h in range(1, GROUP_HEADS):
        folded = folded + o[h * s_new:(h + 1) * s_new, :]
    zf = z_ref[...].astype(F32)
    a_ref[...] = (folded * (zf * _sigmoid(zf))).astype(BF16)


def _attn_sample(q, kn, vn, z, caches, biases):
    b, s_new, qw = q.shape
    gw = GROUP_HEADS * ATTN_HEAD_DIM
    assert s_new % 8 == 0
    cache2 = [jnp.transpose(cb, (0, 2, 3, 4, 1)).reshape(b, 2, gw, cb.shape[1]) for cb in caches]
    in_specs = [
        pl.BlockSpec((None, s_new, qw), lambda bi: (bi, 0, 0)),
        pl.BlockSpec((None, s_new, qw), lambda bi: (bi, 0, 0)),
        pl.BlockSpec((None, s_new, qw), lambda bi: (bi, 0, 0)),
        pl.BlockSpec((None, s_new, gw), lambda bi: (bi, 0, 0)),
    ]
    for cb in cache2:
        in_specs.append(pl.BlockSpec((None, 2, gw, cb.shape[3]), lambda bi: (bi, 0, 0, 0)))
    for bt in biases:
        in_specs.append(pl.BlockSpec(bt.shape, lambda bi: (0, 0)))
    scratch = []
    for cb in cache2:
        scratch += [pltpu.VMEM((gw, cb.shape[3] + LANES), BF16)] * 2
    return pl.pallas_call(
        functools.partial(_attn_sample_kernel, s_new=s_new),
        grid=(b,),
        in_specs=in_specs,
        out_specs=pl.BlockSpec((None, s_new, gw), lambda bi: (bi, 0, 0)),
        out_shape=jax.ShapeDtypeStruct((b, s_new, gw), BF16),
        scratch_shapes=scratch,
        compiler_params=_params(("arbitrary",)),
        name="attn_sample",
    )(q, kn, vn, z, *cache2, *biases)


def _matmul_residual_kernel(a_ref, x_ref, w_ref, y_ref):
    y_ref[...] = x_ref[...] + jnp.dot(a_ref[...], w_ref[...], preferred_element_type=F32)


def _matmul_residual(a2, x2, w, *, tm):
    m, d = x2.shape
    kdim = a2.shape[1]
    assert m % tm == 0
    return pl.pallas_call(
        _matmul_residual_kernel,
        grid=(m // tm,),
        in_specs=[
            pl.BlockSpec((tm, kdim), lambda i: (i, 0)),
            pl.BlockSpec((tm, d), lambda i: (i, 0)),
            pl.BlockSpec((kdim, d), lambda i: (0, 0)),
        ],
        out_specs=pl.BlockSpec((tm, d), lambda i: (i, 0)),
        out_shape=jax.ShapeDtypeStruct((m, d), F32),
        compiler_params=_params(("parallel",)),
        name="matmul_residual",
    )(a2, x2, w)


def _t5_bucket_np(dist):
    exact = N_BUCKETS // 2
    d = np.maximum(dist, 1).astype(np.float32)
    large = exact + (np.log(d / np.float32(exact)) / np.float32(math.log(MAX_DISTANCE / exact))
                     * np.float32(N_BUCKETS - exact)).astype(np.int32)
    return np.where(dist < exact, dist, np.minimum(large, N_BUCKETS - 1)).astype(np.int32)


def _bias_by_step(rel_bias_g, jmax, dil):
    return rel_bias_g.astype(F32)[_t5_bucket_np(np.arange(jmax + 1) * dil)]


def _prompt_bias(rel_bias_g, win, dil):
    jmax = win // dil
    assert jmax == ATTN_BLOCK
    qi = np.arange(ATTN_BLOCK)[:, None]
    kj = np.arange(2 * ATTN_BLOCK)[None, :]
    rel = qi + ATTN_BLOCK - kj
    band = (rel >= 0) & (rel <= jmax)
    bvec = _bias_by_step(rel_bias_g, jmax, dil)
    period = 2 * ATTN_BLOCK + 1
    base = jnp.concatenate([bvec[::-1], jnp.zeros((period - jmax - 1, GROUP_HEADS), F32)], axis=0).T
    bias = jnp.tile(base, (1, ATTN_BLOCK))[:, :2 * ATTN_BLOCK * ATTN_BLOCK]
    bias = bias.reshape(GROUP_HEADS, ATTN_BLOCK, 2 * ATTN_BLOCK)
    rest = jnp.where(band[None], bias, -jnp.inf)
    first = jnp.where((band & (kj >= ATTN_BLOCK))[None], bias, -jnp.inf)
    return jnp.stack([first, rest], axis=0)


def _sample_bias(rel_bias_g, win, dil, buf_len, s_new):
    jmax = win // dil
    assert buf_len == jmax * dil
    width = buf_len + LANES
    bvec = _bias_by_step(rel_bias_g, jmax, dil)
    gaps = jnp.full((jmax + 1, dil - 1, GROUP_HEADS), -jnp.inf, F32)
    by_dist = jnp.concatenate([bvec[:, None, :], gaps], axis=1).reshape((jmax + 1) * dil, GROUP_HEADS)
    padded = jnp.pad(by_dist[::-1], ((s_new, width), (0, 0)), constant_values=-jnp.inf)
    rows = [padded[dil - 1 - s + s_new:dil - 1 - s + s_new + width] for s in range(s_new)]
    table = jnp.transpose(jnp.stack(rows, axis=0), (2, 0, 1))
    return table.reshape(GROUP_HEADS * s_new, width)


def _layer_a(x3, state, weights, *, chunk, pad_to):
    norm_a, w_cols, colscale, wkt, wg, bg, hgain, wout, col_idx, k_idx = weights
    b, t, d = x3.shape
    H = MLSTM_HEADS
    m = b * t
    di = wout.shape[0]
    dh = di // H
    tm = min(1024, m)
    p, gc, gr = _inproj(x3.reshape(m, d), norm_a, w_cols, colscale, wg, bg,
                        tm=tm, tn=1024, n_heads=H)
    if pad_to == t:
        kt3 = _kproj_t(x3, norm_a, wkt, tm=min(1024, t), tn=1024, scale=dh ** -0.5)
        p3 = p.reshape(b, t, -1)
        xin = x3
    else:
        extra = pad_to - t
        p3 = p.reshape(b, t, -1)
        k3 = p3[:, :, k_idx * di:(k_idx + 1) * di]
        kt3 = jnp.pad(jnp.swapaxes(k3, 1, 2), ((0, 0), (0, 0), (0, extra)))
        p3 = jnp.pad(p3, ((0, 0), (0, extra), (0, 0)))
        xin = jnp.pad(x3, ((0, 0), (0, extra), (0, 0)))
        lane = np.arange(LANES)
        pad_col = np.where(lane < H, -np.inf, 0.0).astype(np.float32)
        gc = jnp.concatenate([gc.reshape(b, t, LANES),
                              jnp.broadcast_to(pad_col, (b, extra, LANES))], axis=1).reshape(-1, LANES)
        pad_row = np.where(np.arange(8) < H, -np.inf, 0.0).astype(np.float32)[:, None, None]
        gr = jnp.concatenate([gr.reshape(8, b, t),
                              jnp.broadcast_to(pad_row, (8, b, extra))], axis=2).reshape(8, -1)
    xo, c_out, n_out, m_out = _scan(p3, col_idx, kt3, gc, gr, xin, wout, hgain, state,
                                    chunk=chunk, n_heads=H)
    return xo[:, :t], c_out, n_out, m_out[:, :H, 0]


def kernel(x_prompt, x_sample, state_mlstm_C, state_mlstm_n, state_mlstm_m, cache_kv_w128, cache_kv_w512, cache_kv_w2048, norm_a, w_in_a, b_gates_a, hnorm_a, w_out_a, norm_kv, w_kv, k_norm, norm_b, w_in_b, q_norm, rel_bias, w_out_b):
    H = MLSTM_HEADS
    bp, tp, d = x_prompt.shape
    bs, ts, _ = x_sample.shape
    di = w_out_a.shape[1]
    dh = di // H
    gw = GROUP_HEADS * ATTN_HEAD_DIM
    qw = len(GROUPS) * gw
    caches = (cache_kv_w128, cache_kv_w512, cache_kv_w2048)
    assert norm_a.shape[0] == 1 and norm_b.shape[0] == 1, "one mLSTM layer, one attention layer"
    for cb, (win, _) in zip(caches, GROUPS):
        assert cb.shape[1] == win, "window buffers must hold a full window"

    w_a = w_in_a[0]
    wq, wk, wv, wo, wz = (w_a[:, i * di:(i + 1) * di] for i in range(5))
    wg = jnp.pad(w_a[:, 5 * di:], ((0, 0), (0, LANES - 2 * H)))
    bg = jnp.pad(b_gates_a[0].astype(F32), (0, LANES - 2 * H))[None, :]
    na = norm_a[0].astype(F32)[None, :]
    hgain = hnorm_a[0].astype(F32)[None, :]
    wout_a = w_out_a[0].astype(BF16)
    w_prompt = jnp.concatenate([wq, wv, wo, wz], axis=1).astype(BF16)
    w_sample = w_a[:, :5 * di].astype(BF16)
    wkt = wk.T.astype(BF16)
    ones_p = jnp.ones((1, 4 * di), F32)
    scale_s = jnp.concatenate([jnp.ones((di,), F32), jnp.full((di,), dh ** -0.5, F32),
                               jnp.ones((3 * di,), F32)])[None, :]
    weights_p = (na, w_prompt, ones_p, wkt, wg, bg, hgain, wout_a, (0, 1, 2, 3), None)
    weights_s = (na, w_sample, scale_s, None, wg, bg, hgain, wout_a, (0, 2, 3, 4), 1)

    xp1, c_p, n_p, m_p = _layer_a(x_prompt, None, weights_p, chunk=256, pad_to=tp)
    m0 = jnp.pad(jnp.broadcast_to(state_mlstm_m[0].astype(F32)[:, :, None], (bs, H, LANES)),
                 ((0, 0), (0, 8 - H), (0, 0)))
    state_s = (state_mlstm_C[0].astype(F32), state_mlstm_n[0].astype(F32), m0)
    xs1, c_s, n_s, m_s = _layer_a(x_sample, state_s, weights_s, chunk=LANES, pad_to=LANES)

    nkv = norm_kv.astype(F32)[None, :]
    wkv = w_kv.astype(BF16)
    kgain = jnp.tile(k_norm.astype(F32), qw // ATTN_HEAD_DIM)[None, :]
    rows_p = [min(win, tp) for win, _ in GROUPS]
    dils = tuple(dil for _, dil in GROUPS)
    *kvp, kv128_p, kv512_p, kv2048_p = _proj_headnorm(
        xp1, nkv, wkv, kgain, n_norm=qw, scale=1.0, tm=512, tail_rows=rows_p,
        dils=dils, rest_by_group=True)
    kp, vp = kvp[:len(GROUPS)], kvp[len(GROUPS):]
    ks, vs, kv128_s, kv512_s, kv2048_s = _proj_headnorm(
        xs1, nkv, wkv, kgain, n_norm=qw, scale=1.0, tm=ts, tail_rows=[ts] * len(GROUPS))

    nb_ = norm_b[0].astype(F32)[None, :]
    wb = w_in_b[0].astype(BF16)
    qgain = jnp.tile(q_norm[0].astype(F32), qw // ATTN_HEAD_DIM)[None, :]
    wout_b = w_out_b[0].astype(BF16)
    qscale = ATTN_HEAD_DIM ** -0.5
    *qp, zp = _proj_headnorm(xp1, nb_, wb, qgain, n_norm=qw, scale=qscale, tm=512, dils=dils)
    qs, zs = _proj_headnorm(xs1, nb_, wb, qgain, n_norm=qw, scale=qscale, tm=ts)

    outs, lses = [], []
    for g, (win, dil) in enumerate(GROUPS):
        bias = _prompt_bias(rel_bias[:, g * GROUP_HEADS:(g + 1) * GROUP_HEADS], win, dil)
        o, lse = _attn_prompt(qp[g], kp[g], vp[g], bias, g)
        outs.append(o)
        lses.append(lse)
    y_p = _merge_out(outs, lses, zp, xp1, wout_b, tm=512)

    sbias = [_sample_bias(rel_bias[:, g * GROUP_HEADS:(g + 1) * GROUP_HEADS], win, dil,
                          caches[g].shape[1], ts) for g, (win, dil) in enumerate(GROUPS)]
    a_s = _attn_sample(qs, ks, vs, zs, caches, sbias)
    y_s = _matmul_residual(a_s.reshape(bs * ts, gw), xs1.reshape(bs * ts, d), wout_b,
                           tm=bs * ts).reshape(bs, ts, d)

    kv5 = lambda a: a.reshape(a.shape[0], a.shape[1], 2, GROUP_HEADS, ATTN_HEAD_DIM)
    return (y_p, y_s, c_p[None], n_p[None], m_p[None], c_s[None], n_s[None], m_s[None],
            kv5(kv128_p), kv5(kv512_p), kv5(kv2048_p), kv5(kv128_s), kv5(kv512_s), kv5(kv2048_s))
```

```python
import functools
import math

import numpy as np
import jax
import jax.numpy as jnp
from jax import lax
from jax.experimental import pallas as pl
from jax.experimental.pallas import tpu as pltpu

F32 = jnp.float32
BF16 = jnp.bfloat16

EPS = 1e-6
MLSTM_HEADS = 4
GROUPS = ((128, 1), (512, 4), (2048, 16))
GROUP_HEADS = 8
ATTN_HEAD_DIM = 64
ATTN_BLOCK = 128
N_BUCKETS = 32
MAX_DISTANCE = 2048

LANES = 128
MXU_DIM = 256
VMEM_LIMIT_BYTES = 56 * 1024 * 1024

NT_DIMS = (((1,), (1,)), ((), ()))


def _params(sem):
    return pltpu.CompilerParams(dimension_semantics=sem, vmem_limit_bytes=VMEM_LIMIT_BYTES)


def _rms_scale(xf):
    return lax.rsqrt(jnp.mean(xf * xf, axis=-1, keepdims=True) + EPS)


def _sigmoid(x):
    return 1.0 / (1.0 + jnp.exp(-x))


def _split_bf16(a):
    hi = a.astype(BF16)
    lo = (a - hi.astype(F32)).astype(BF16)
    return hi, lo


def _inproj_kernel(x_ref, g_ref, w_ref, cs_ref, wg_ref, bg_ref,
                   p_ref, gc_ref, gr_ref, xn_ref, *, n_heads):
    j = pl.program_id(1)

    @pl.when(j == 0)
    def _():
        xf = x_ref[...]
        xn = xf * _rms_scale(xf) * g_ref[...]
        xh, xl = _split_bf16(xn)
        wh, wl = _split_bf16(wg_ref[...])
        xn_ref[...] = xh
        gates = (jnp.dot(xh, wh, preferred_element_type=F32)
                 + jnp.dot(xl, wh, preferred_element_type=F32)
                 + jnp.dot(xh, wl, preferred_element_type=F32)) + bg_ref[...]
        lane = lax.broadcasted_iota(jnp.int32, gates.shape, 1)
        logsig = jnp.minimum(gates, 0.0) - jnp.log(1.0 + jnp.exp(-jnp.abs(gates)))
        gcol = jnp.where(lane < n_heads, gates, jnp.where(lane < 2 * n_heads, logsig, 0.0))
        gc_ref[...] = gcol
        gr_ref[...] = gcol.T[:8, :]

    acc = jnp.dot(xn_ref[...], w_ref[...], preferred_element_type=F32)
    p_ref[...] = (acc * cs_ref[...]).astype(BF16)


def _inproj(x2, g, w, colscale, wg, bg, *, tm, tn, n_heads):
    m, d = x2.shape
    n = w.shape[1]
    assert m % tm == 0 and n % tn == 0 and 2 * n_heads <= 8
    return pl.pallas_call(
        functools.partial(_inproj_kernel, n_heads=n_heads),
        grid=(m // tm, n // tn),
        in_specs=[
            pl.BlockSpec((tm, d), lambda i, j: (i, 0)),
            pl.BlockSpec((1, d), lambda i, j: (0, 0)),
            pl.BlockSpec((d, tn), lambda i, j: (0, j)),
            pl.BlockSpec((1, tn), lambda i, j: (0, j)),
            pl.BlockSpec((d, LANES), lambda i, j: (0, 0)),
            pl.BlockSpec((1, LANES), lambda i, j: (0, 0)),
        ],
        out_specs=[
            pl.BlockSpec((tm, tn), lambda i, j: (i, j)),
            pl.BlockSpec((tm, LANES), lambda i, j: (i, 0)),
            pl.BlockSpec((8, tm), lambda i, j: (0, i)),
        ],
        out_shape=[
            jax.ShapeDtypeStruct((m, n), BF16),
            jax.ShapeDtypeStruct((m, LANES), F32),
            jax.ShapeDtypeStruct((8, m), F32),
        ],
        scratch_shapes=[pltpu.VMEM((tm, d), BF16)],
        compiler_params=_params(("parallel", "arbitrary")),
        name="inproj",
    )(x2, g, w, colscale, wg, bg)


def _kproj_t_kernel(x_ref, g_ref, wt_ref, o_ref, xn_ref, *, scale):
    @pl.when(pl.program_id(2) == 0)
    def _():
        xf = x_ref[...]
        xn_ref[...] = (xf * _rms_scale(xf) * g_ref[...]).astype(BF16)

    acc = lax.dot_general(wt_ref[...], xn_ref[...], NT_DIMS, preferred_element_type=F32)
    o_ref[...] = (acc * scale).astype(BF16)


def _kproj_t(x3, g, wt, *, tm, tn, scale):
    b, t, d = x3.shape
    n = wt.shape[0]
    assert t % tm == 0 and n % tn == 0
    return pl.pallas_call(
        functools.partial(_kproj_t_kernel, scale=scale),
        grid=(b, t // tm, n // tn),
        in_specs=[
            pl.BlockSpec((None, tm, d), lambda bi, i, j: (bi, i, 0)),
            pl.BlockSpec((1, d), lambda bi, i, j: (0, 0)),
            pl.BlockSpec((tn, d), lambda bi, i, j: (j, 0)),
        ],
        out_specs=pl.BlockSpec((None, tn, tm), lambda bi, i, j: (bi, j, i)),
        out_shape=jax.ShapeDtypeStruct((b, n, t), BF16),
        scratch_shapes=[pltpu.VMEM((tm, d), BF16)],
        compiler_params=_params(("parallel", "parallel", "arbitrary")),
        name="kproj_t",
    )(x3, g, wt)


def _scan_kernel(*refs, chunk, n_heads, dh, n_chunks, has_state):
    L, H = chunk, n_heads
    dext = dh + LANES
    (q_ref, kt_ref, v_ref, o_ref, z_ref, gc_ref, gr_ref, x_ref, wout_ref, hg_ref) = refs[:10]
    pos = 10
    if has_state:
        c0_ref, n0_ref, m0_ref = refs[pos:pos + 3]
        pos += 3
    xo_ref, cout_ref, nout_ref, mout_ref = refs[pos:pos + 4]
    cext_ref, cb_ref, m_ref = refs[pos + 4:pos + 7]
    c = pl.program_id(1)

    @pl.when(c == 0)
    def _():
        if has_state:
            lane0 = lax.broadcasted_iota(jnp.int32, (dh, LANES), 1) == 0
            for h in range(H):
                ncol = jnp.broadcast_to(n0_ref[h:h + 1, :], (LANES, dh)).T
                cext_ref[h, :, :dh] = c0_ref[h]
                cext_ref[h, :, dh:] = jnp.where(lane0, ncol, 0.0)
            m_ref[...] = m0_ref[...]
        else:
            cext_ref[...] = jnp.zeros(cext_ref.shape, F32)
            m_ref[...] = jnp.zeros(m_ref.shape, F32)
        cb_ref[...] = cext_ref[...].astype(BF16)

    gc = gc_ref[...]
    gr = gr_ref[...]
    row = lax.broadcasted_iota(jnp.int32, (L, L), 0)
    col = lax.broadcasted_iota(jnp.int32, (L, L), 1)
    causal = row >= col
    lane = lax.broadcasted_iota(jnp.int32, gc.shape, 1)
    subl = lax.broadcasted_iota(jnp.int32, gr.shape, 0)
    lf_c_hi, lf_c_lo = _split_bf16(jnp.where((lane >= H) & (lane < 2 * H), gc, 0.0))
    lf_r_hi, lf_r_lo = _split_bf16(jnp.where((subl >= H) & (subl < 2 * H), gr, 0.0))
    lower = causal.astype(BF16)
    upper = (row <= col).astype(BF16)
    bc_all = (jnp.dot(lower, lf_c_hi, preferred_element_type=F32)
              + jnp.dot(lower, lf_c_lo, preferred_element_type=F32))
    br_all = (jnp.dot(lf_r_hi, upper, preferred_element_type=F32)
              + jnp.dot(lf_r_lo, upper, preferred_element_type=F32))
    ones_col = (lax.broadcasted_iota(jnp.int32, (L, LANES), 1) == 0).astype(BF16)

    y = jnp.zeros((L, wout_ref.shape[1]), F32)
    for h in range(H):
        sl = slice(h * dh, (h + 1) * dh)
        q = q_ref[:, sl]
        kt = kt_ref[sl, :]
        vext = jnp.concatenate([v_ref[:, sl], ones_col], axis=1)
        ig_r = gr[h:h + 1, :]
        b_c = bc_all[:, H + h:H + h + 1]
        b_r = br_all[H + h:H + h + 1, :]
        m_prev = m_ref[h:h + 1, 0:1]

        log_d = jnp.where(causal, b_c - b_r + ig_r, -jnp.inf)
        log_inter = b_c + m_prev
        m_t = jnp.maximum(log_inter, jnp.max(log_d, axis=1, keepdims=True))
        dmat = jnp.exp(log_d - m_t)
        inter = jnp.exp(log_inter - m_t)
        s = jnp.dot(q, kt, preferred_element_type=F32) * dmat
        numden = (jnp.dot(s.astype(BF16), vext, preferred_element_type=F32)
                  + inter * jnp.dot(q, cb_ref[h], preferred_element_type=F32))
        num = numden[:, :dh]
        den = numden[:, dh:dh + 1]
        hh = num / jnp.maximum(jnp.abs(den), jnp.exp(-m_t))

        b_last = b_r[:, L - 1:L]
        a_r = b_last - b_r + ig_r
        m_new = jnp.maximum(b_last + m_prev, jnp.max(a_r, axis=1, keepdims=True))
        w_r = jnp.exp(a_r - m_new)
        decay = jnp.exp(b_last + m_prev - m_new)
        ktw = (kt.astype(F32) * w_r).astype(BF16)
        c_new = decay * cext_ref[h] + jnp.dot(ktw, vext, preferred_element_type=F32)
        cext_ref[h] = c_new
        cb_ref[h] = c_new.astype(BF16)
        m_ref[h:h + 1, :] = jnp.broadcast_to(m_new, (1, LANES))

        hn = hh * lax.rsqrt(jnp.mean(hh * hh, axis=1, keepdims=True) + EPS)
        zf = z_ref[:, sl].astype(F32)
        gate = _sigmoid(o_ref[:, sl].astype(F32)) * (zf * _sigmoid(zf))
        hg = (hn * hg_ref[:, sl] * gate).astype(BF16)
        y = y + jnp.dot(hg, wout_ref[sl, :], preferred_element_type=F32)

    xo_ref[...] = x_ref[...] + y

    @pl.when(c == n_chunks - 1)
    def _():
        for h in range(H):
            cout_ref[h] = cext_ref[h, :, :dh]
            nout_ref[h:h + 1, :] = cext_ref[h, :, dh:].T[0:1, :]
        mout_ref[...] = m_ref[...]


def _scan(p3, col_idx, kt3, gc, gr, x3, wout, hgain, state, *, chunk, n_heads):
    b, t, _ = p3.shape
    dh = kt3.shape[1] // n_heads
    di = n_heads * dh
    d = x3.shape[2]
    nc = t // chunk
    assert t % chunk == 0
    has_state = state is not None
    qi, vi, oi, zi = col_idx

    def pspec(ci):
        return pl.BlockSpec((None, chunk, di), lambda bi, c, ci=ci: (bi, c, ci))

    in_specs = [
        pspec(qi),
        pl.BlockSpec((None, di, chunk), lambda bi, c: (bi, 0, c)),
        pspec(vi), pspec(oi), pspec(zi),
        pl.BlockSpec((chunk, LANES), lambda bi, c: (bi * nc + c, 0)),
        pl.BlockSpec((8, chunk), lambda bi, c: (0, bi * nc + c)),
        pl.BlockSpec((None, chunk, d), lambda bi, c: (bi, c, 0)),
        pl.BlockSpec((di, d), lambda bi, c: (0, 0)),
        pl.BlockSpec((1, di), lambda bi, c: (0, 0)),
    ]
    args = [p3, kt3, p3, p3, p3, gc, gr, x3, wout, hgain]
    if has_state:
        c0, n0, m0 = state
        in_specs += [
            pl.BlockSpec((None, n_heads, dh, dh), lambda bi, c: (bi, 0, 0, 0)),
            pl.BlockSpec((None, n_heads, dh), lambda bi, c: (bi, 0, 0)),
            pl.BlockSpec((None, 8, LANES), lambda bi, c: (bi, 0, 0)),
        ]
        args += [c0, n0, m0]
    return pl.pallas_call(
        functools.partial(_scan_kernel, chunk=chunk, n_heads=n_heads, dh=dh,
                          n_chunks=nc, has_state=has_state),
        grid=(b, nc),
        in_specs=in_specs,
        out_specs=[
            pl.BlockSpec((None, chunk, d), lambda bi, c: (bi, c, 0)),
            pl.BlockSpec((None, n_heads, dh, dh), lambda bi, c: (bi, 0, 0, 0)),
            pl.BlockSpec((None, n_heads, dh), lambda bi, c: (bi, 0, 0)),
            pl.BlockSpec((None, 8, LANES), lambda bi, c: (bi, 0, 0)),
        ],
        out_shape=[
            jax.ShapeDtypeStruct((b, t, d), F32),
            jax.ShapeDtypeStruct((b, n_heads, dh, dh), F32),
            jax.ShapeDtypeStruct((b, n_heads, dh), F32),
            jax.ShapeDtypeStruct((b, 8, LANES), F32),
        ],
        scratch_shapes=[
            pltpu.VMEM((n_heads, dh, dh + LANES), F32),
            pltpu.VMEM((n_heads, dh, dh + LANES), BF16),
            pltpu.VMEM((8, LANES), F32),
        ],
        compiler_params=_params(("parallel", "arbitrary")),
        name="mlstm_scan",
    )(*args)


def _headnorm(a):
    n = a.shape[1]
    r = lax.broadcasted_iota(jnp.int32, (MXU_DIM, MXU_DIM), 0)
    c = lax.broadcasted_iota(jnp.int32, (MXU_DIM, MXU_DIM), 1)
    same_head = (lax.shift_right_logical(r, 6) == lax.shift_right_logical(c, 6)).astype(BF16)
    parts = []
    for c0 in range(0, n, MXU_DIM):
        blk = a[:, c0:c0 + MXU_DIM]
        hi, lo = _split_bf16(blk * blk)
        ss = (jnp.dot(hi, same_head, preferred_element_type=F32)
              + jnp.dot(lo, same_head, preferred_element_type=F32))
        parts.append(blk * lax.rsqrt(ss * (1.0 / ATTN_HEAD_DIM) + EPS))
    return jnp.concatenate(parts, axis=1)


def _store_by_residue(ref, val, scr, dil):
    rows, width = val.shape
    if dil == 1:
        ref[0] = val.astype(BF16)
        return
    for c in range(width // LANES):
        scr[c] = val[:, c * LANES:(c + 1) * LANES]
    for r in range(dil):
        parts = [scr[c, pl.ds(r, rows // dil, stride=dil), :] for c in range(width // LANES)]
        ref[r] = jnp.concatenate(parts, axis=1).astype(BF16)


def _proj_headnorm_kernel(*refs, n_norm, scale, tails, tm, dils, rest_by_group):
    x_ref, g_ref, w_ref, hg_ref = refs[:4]
    gw = GROUP_HEADS * ATTN_HEAD_DIM
    n_groups = n_norm // gw
    pos = 4
    if dils is None:
        a_refs, r_refs = refs[pos:pos + 1], refs[pos + 1:pos + 2]
        pos += 2
    else:
        a_refs = refs[pos:pos + n_groups]
        pos += n_groups
        n_rest = n_groups if rest_by_group else 1
        r_refs = refs[pos:pos + n_rest]
        pos += n_rest
    tail_refs = refs[pos:pos + len(tails)]
    scr = refs[pos + len(tails)] if dils is not None else None

    xf = x_ref[...]
    xn = (xf * _rms_scale(xf) * g_ref[...]).astype(BF16)
    p = jnp.dot(xn, w_ref[...], preferred_element_type=F32)
    a = _headnorm(p[:, :n_norm]) * hg_ref[...]
    if scale != 1.0:
        a = a * scale
    r = p[:, n_norm:]
    if dils is None:
        a_refs[0][...] = a.astype(BF16)
        r_refs[0][...] = r.astype(BF16)
    else:
        for g, dil in enumerate(dils):
            cs = slice(g * gw, (g + 1) * gw)
            _store_by_residue(a_refs[g], a[:, cs], scr, dil)
            if rest_by_group:
                _store_by_residue(r_refs[g], r[:, cs], scr, dil)
        if not rest_by_group:
            r_refs[0][...] = r.astype(BF16)
    i = pl.program_id(1)
    for g, (rows, first_tile) in enumerate(tails):
        t_ref = tail_refs[g]
        cs = slice(g * gw, (g + 1) * gw)

        @pl.when(i >= first_tile)
        def _(t_ref=t_ref, cs=cs, rows=rows):
            if rows >= tm:
                t_ref[:, :gw] = a[:, cs]
                t_ref[:, gw:] = r[:, cs]
            else:
                t_ref[:, :gw] = a[tm - rows:, cs]
                t_ref[:, gw:] = r[tm - rows:, cs]


def _proj_headnorm(x3, g, w, hgain, *, n_norm, scale, tm, tail_rows=None, dils=None,
                   rest_by_group=False):
    b, t, d = x3.shape
    n = w.shape[1]
    assert t % tm == 0 and n_norm % MXU_DIM == 0
    n_tiles = t // tm
    gw = GROUP_HEADS * ATTN_HEAD_DIM
    out_specs, out_shape, scratch = [], [], []

    def natural(width):
        out_specs.append(pl.BlockSpec((None, tm, width), lambda bi, i: (bi, i, 0)))
        out_shape.append(jax.ShapeDtypeStruct((b, t, width), BF16))

    def by_residue(dil):
        assert tm % (dil * 16) == 0
        out_specs.append(pl.BlockSpec((None, dil, tm // dil, gw), lambda bi, i: (bi, 0, i, 0)))
        out_shape.append(jax.ShapeDtypeStruct((b, dil, t // dil, gw), BF16))

    if dils is None:
        natural(n_norm)
        natural(n - n_norm)
    else:
        assert len(dils) * gw == n_norm
        for dil in dils:
            by_residue(dil)
        if rest_by_group:
            assert n - n_norm == n_norm
            for dil in dils:
                by_residue(dil)
        else:
            natural(n - n_norm)
        scratch.append(pltpu.VMEM((gw // LANES, tm, LANES), F32))
    tails = []
    for rows in (tail_rows or ()):
        if rows >= tm:
            assert rows % tm == 0
            first = n_tiles - rows // tm
            blk = tm
        else:
            first = n_tiles - 1
            blk = rows
        tails.append((rows, first))
        out_specs.append(pl.BlockSpec(
            (None, blk, 2 * gw), lambda bi, i, first=first: (bi, jnp.maximum(i - first, 0), 0)))
        out_shape.append(jax.ShapeDtypeStruct((b, rows, 2 * gw), F32))
    return pl.pallas_call(
        functools.partial(_proj_headnorm_kernel, n_norm=n_norm, scale=scale, tails=tuple(tails),
                          tm=tm, dils=dils, rest_by_group=rest_by_group),
        grid=(b, n_tiles),
        in_specs=[
            pl.BlockSpec((None, tm, d), lambda bi, i: (bi, i, 0)),
            pl.BlockSpec((1, d), lambda bi, i: (0, 0)),
            pl.BlockSpec((d, n), lambda bi, i: (0, 0)),
            pl.BlockSpec((1, n_norm), lambda bi, i: (0, 0)),
        ],
        out_specs=out_specs,
        out_shape=out_shape,
        scratch_shapes=scratch,
        compiler_params=_params(("parallel", "arbitrary")),
        name="proj_headnorm",
    )(x3, g, w, hgain)


def _attn_prompt_kernel(q_ref, kp_ref, kc_ref, vp_ref, vc_ref, bias_ref, o_ref, lse_ref, *, nq):
    hd = ATTN_HEAD_DIM
    blk = ATTN_BLOCK
    first = jnp.where(pl.program_id(2) == 0, 0, 1)
    lane = lax.broadcasted_iota(jnp.int32, (blk, LANES), 1)
    for jb in range(nq):
        q = q_ref[jb * blk:(jb + 1) * blk, :]
        if jb == 0:
            kcat = jnp.concatenate([kp_ref[...], kc_ref[0:blk, :]], axis=0)
            vcat = jnp.concatenate([vp_ref[...], vc_ref[0:blk, :]], axis=0)
        else:
            kcat = kc_ref[(jb - 1) * blk:(jb + 1) * blk, :]
            vcat = vc_ref[(jb - 1) * blk:(jb + 1) * blk, :]
        ms, ls, outs = [], [], []
        for h in range(GROUP_HEADS):
            hs = slice(h * hd, (h + 1) * hd)
            bias = bias_ref[first, h] if jb == 0 else bias_ref[1, h]
            st = lax.dot_general(kcat[:, hs], q[:, hs], NT_DIMS, preferred_element_type=F32) + bias
            m = jnp.max(st, axis=0, keepdims=True)
            p = jnp.exp(st - m)
            ls.append(jnp.sum(p, axis=0, keepdims=True))
            ms.append(m)
            outs.append(jnp.dot(p.T.astype(BF16), vcat[:, hs], preferred_element_type=F32))
        m_all = jnp.concatenate(ms, axis=0)
        l_all = jnp.concatenate(ls, axis=0)
        stats = jnp.concatenate([m_all + jnp.log(l_all), 1.0 / l_all,
                                 jnp.zeros((LANES - 2 * GROUP_HEADS, blk), F32)], axis=0).T
        o_ref[jb * blk:(jb + 1) * blk, :] = jnp.concatenate(
            [o * stats[:, GROUP_HEADS + h:GROUP_HEADS + h + 1] for h, o in enumerate(outs)], axis=1)
        lse_ref[jb * blk:(jb + 1) * blk, :] = jnp.where(lane < GROUP_HEADS, stats, 0.0)


def _attn_prompt(q, k, v, bias, g):
    b, dil, s, gw = q.shape
    nb = s // ATTN_BLOCK
    assert s % ATTN_BLOCK == 0
    nq = min(4, nb)
    assert nb % nq == 0
    rows = nq * ATTN_BLOCK
    cur = pl.BlockSpec((None, None, rows, gw), lambda bi, r, j: (bi, r, j, 0))
    prev = pl.BlockSpec((None, None, ATTN_BLOCK, gw),
                        lambda bi, r, j: (bi, r, jnp.maximum(j * nq - 1, 0), 0))
    return pl.pallas_call(
        functools.partial(_attn_prompt_kernel, nq=nq),
        grid=(b, dil, nb // nq),
        in_specs=[
            cur, prev, cur, prev, cur,
            pl.BlockSpec(bias.shape, lambda bi, r, j: (0, 0, 0, 0)),
        ],
        out_specs=[
            pl.BlockSpec((None, None, rows, gw), lambda bi, r, j: (bi, r, j, 0)),
            pl.BlockSpec((None, None, rows, LANES), lambda bi, r, j: (bi, r, j, 0)),
        ],
        out_shape=[
            jax.ShapeDtypeStruct((b, dil, s, gw), F32),
            jax.ShapeDtypeStruct((b, dil, s, LANES), F32),
        ],
        compiler_params=_params(("parallel", "parallel", "arbitrary")),
        name="attn_prompt_g%d" % g,
    )(q, k, k, v, v, bias)


def _head_expand_matrix():
    r = lax.broadcasted_iota(jnp.int32, (LANES, GROUP_HEADS * ATTN_HEAD_DIM), 0)
    c = lax.broadcasted_iota(jnp.int32, (LANES, GROUP_HEADS * ATTN_HEAD_DIM), 1)
    return (r == lax.shift_right_logical(c, 6)).astype(BF16)


def _load_token_order(ref, scr):
    dil, per, width = ref.shape
    if dil == 1:
        return ref[0]
    n_tiles = width // LANES
    for r in range(dil):
        val = ref[r]
        for c in range(n_tiles):
            scr[c, pl.ds(r, per, stride=dil), :] = val[:, c * LANES:(c + 1) * LANES]
    return jnp.concatenate([scr[c] for c in range(n_tiles)], axis=1)


def _merge_out_kernel(o0_ref, o1_ref, o2_ref, l0_ref, l1_ref, l2_ref, z_ref, x_ref, w_ref, y_ref,
                      *scratch):
    o_refs, l_refs = (o0_ref, o1_ref, o2_ref), (l0_ref, l1_ref, l2_ref)
    os_, ls, k = [], [], 0
    for o_ref, l_ref in zip(o_refs, l_refs):
        if o_ref.shape[0] == 1:
            os_.append(o_ref[0])
            ls.append(l_ref[0])
        else:
            os_.append(_load_token_order(o_ref, scratch[k]))
            ls.append(_load_token_order(l_ref, scratch[k + 1]))
            k += 2
    lmax = jnp.maximum(jnp.maximum(ls[0], ls[1]), ls[2])
    es = [jnp.exp(l - lmax) for l in ls]
    tot = es[0] + es[1] + es[2]
    expand = _head_expand_matrix()
    o = jnp.zeros(os_[0].shape, F32)
    for e, og in zip(es, os_):
        hi, lo = _split_bf16(e / tot)
        wexp = (jnp.dot(hi, expand, preferred_element_type=F32)
                + jnp.dot(lo, expand, preferred_element_type=F32))
        o = o + wexp * og
    zf = z_ref[...].astype(F32)
    a = (o * (zf * _sigmoid(zf))).astype(BF16)
    y_ref[...] = x_ref[...] + jnp.dot(a, w_ref[...], preferred_element_type=F32)


def _merge_out(outs, lses, z3, x3, w, *, tm):
    b, t, d = x3.shape
    gw = w.shape[0]
    assert t % tm == 0
    row = lambda width: pl.BlockSpec((None, tm, width), lambda bi, i: (bi, i, 0))
    by_residue = lambda a: pl.BlockSpec((None, a.shape[1], tm // a.shape[1], a.shape[3]),
                                        lambda bi, i: (bi, 0, i, 0))
    scratch = []
    for o in outs:
        if o.shape[1] > 1:
            scratch += [pltpu.VMEM((gw // LANES, tm, LANES), F32), pltpu.VMEM((1, tm, LANES), F32)]
    return pl.pallas_call(
        _merge_out_kernel,
        grid=(b, t // tm),
        in_specs=[by_residue(o) for o in outs] + [by_residue(l) for l in lses]
        + [row(gw), row(d), pl.BlockSpec((gw, d), lambda bi, i: (0, 0))],
        out_specs=row(d),
        out_shape=jax.ShapeDtypeStruct((b, t, d), F32),
        scratch_shapes=scratch,
        compiler_params=_params(("parallel", "parallel")),
        name="merge_out",
    )(*outs, *lses, z3, x3, w)


def _attn_sample_kernel(q_ref, kn_ref, vn_ref, z_ref, c0_ref, c1_ref, c2_ref,
                        b0_ref, b1_ref, b2_ref, a_ref, ks0, vs0, ks1, vs1, ks2, vs2, *, s_new):
    gw = GROUP_HEADS * ATTN_HEAD_DIM
    rows = GROUP_HEADS * s_new
    r = lax.broadcasted_iota(jnp.int32, (rows, gw), 0)
    c = lax.broadcasted_iota(jnp.int32, (rows, gw), 1)
    head_mask = (lax.shift_right_logical(r, int(math.log2(s_new)))
                 == lax.shift_right_logical(c, int(math.log2(ATTN_HEAD_DIM))))
    caches = ((c0_ref, b0_ref, ks0, vs0), (c1_ref, b1_ref, ks1, vs1), (c2_ref, b2_ref, ks2, vs2))
    pad_rows = jnp.zeros((LANES - s_new, gw), F32)
    outs, lses = [], []
    for g, (c_ref, b_ref, ks, vs) in enumerate(caches):
        buf_len = c_ref.shape[2]
        cs = slice(g * gw, (g + 1) * gw)
        ks[:, :buf_len] = c_ref[0].astype(BF16)
        vs[:, :buf_len] = c_ref[1].astype(BF16)
        ks[:, buf_len:] = jnp.concatenate([kn_ref[:, cs].astype(F32), pad_rows], axis=0).T.astype(BF16)
        vs[:, buf_len:] = jnp.concatenate([vn_ref[:, cs].astype(F32), pad_rows], axis=0).T.astype(BF16)
        qg = q_ref[:, cs].astype(F32)
        qbd = jnp.where(head_mask, jnp.concatenate([qg] * GROUP_HEADS, axis=0), 0.0).astype(BF16)
        s = jnp.dot(qbd, ks[...], preferred_element_type=F32) + b_ref[...]
        m = jnp.max(s, axis=1, keepdims=True)
        p = jnp.exp(s - m)
        l = jnp.sum(p, axis=1, keepdims=True)
        outs.append(lax.dot_general(p.astype(BF16), vs[...], NT_DIMS, preferred_element_type=F32) / l)
        lses.append(m + jnp.log(l))
    lmax = jnp.maximum(jnp.maximum(lses[0], lses[1]), lses[2])
    es = [jnp.exp(l - lmax) for l in lses]
    tot = es[0] + es[1] + es[2]
    o = jnp.zeros((rows, gw), F32)
    for e, og in zip(es, outs):
        o = o + (e / tot) * og
    o = jnp.where(head_mask, o, 0.0)
    folded = o[0:s_new, :]
    for h in range(1, GROUP_HEADS):
        folded = folded + o[h * s_new:(h + 1) * s_new, :]
    zf = z_ref[...].astype(F32)
    a_ref[...] = (folded * (zf * _sigmoid(zf))).astype(BF16)


def _attn_sample(q, kn, vn, z, caches, biases):
    b, s_new, qw = q.shape
    gw = GROUP_HEADS * ATTN_HEAD_DIM
    assert s_new % 8 == 0
    cache2 = [jnp.transpose(cb, (0, 2, 3, 4, 1)).reshape(b, 2, gw, cb.shape[1]) for cb in caches]
    in_specs = [
        pl.BlockSpec((None, s_new, qw), lambda bi: (bi, 0, 0)),
        pl.BlockSpec((None, s_new, qw), lambda bi: (bi, 0, 0)),
        pl.BlockSpec((None, s_new, qw), lambda bi: (bi, 0, 0)),
        pl.BlockSpec((None, s_new, gw), lambda bi: (bi, 0, 0)),
    ]
    for cb in cache2:
        in_specs.append(pl.BlockSpec((None, 2, gw, cb.shape[3]), lambda bi: (bi, 0, 0, 0)))
    for bt in biases:
        in_specs.append(pl.BlockSpec(bt.shape, lambda bi: (0, 0)))
    scratch = []
    for cb in cache2:
        scratch += [pltpu.VMEM((gw, cb.shape[3] + LANES), BF16)] * 2
    return pl.pallas_call(
        functools.partial(_attn_sample_kernel, s_new=s_new),
        grid=(b,),
        in_specs=in_specs,
        out_specs=pl.BlockSpec((None, s_new, gw), lambda bi: (bi, 0, 0)),
        out_shape=jax.ShapeDtypeStruct((b, s_new, gw), BF16),
        scratch_shapes=scratch,
        compiler_params=_params(("arbitrary",)),
        name="attn_sample",
    )(q, kn, vn, z, *cache2, *biases)


def _matmul_residual_kernel(a_ref, x_ref, w_ref, y_ref):
    y_ref[...] = x_ref[...] + jnp.dot(a_ref[...], w_ref[...], preferred_element_type=F32)


def _matmul_residual(a2, x2, w, *, tm):
    m, d = x2.shape
    kdim = a2.shape[1]
    assert m % tm == 0
    return pl.pallas_call(
        _matmul_residual_kernel,
        grid=(m // tm,),
        in_specs=[
            pl.BlockSpec((tm, kdim), lambda i: (i, 0)),
            pl.BlockSpec((tm, d), lambda i: (i, 0)),
            pl.BlockSpec((kdim, d), lambda i: (0, 0)),
        ],
        out_specs=pl.BlockSpec((tm, d), lambda i: (i, 0)),
        out_shape=jax.ShapeDtypeStruct((m, d), F32),
        compiler_params=_params(("parallel",)),
        name="matmul_residual",
    )(a2, x2, w)


def _t5_bucket_np(dist):
    exact = N_BUCKETS // 2
    d = np.maximum(dist, 1).astype(np.float32)
    large = exact + (np.log(d / np.float32(exact)) / np.float32(math.log(MAX_DISTANCE / exact))
                     * np.float32(N_BUCKETS - exact)).astype(np.int32)
    return np.where(dist < exact, dist, np.minimum(large, N_BUCKETS - 1)).astype(np.int32)


def _bias_by_step(rel_bias_g, jmax, dil):
    return rel_bias_g.astype(F32)[_t5_bucket_np(np.arange(jmax + 1) * dil)]


def _prompt_bias(rel_bias_g, win, dil):
    jmax = win // dil
    assert jmax == ATTN_BLOCK
    qi = np.arange(ATTN_BLOCK)[:, None]
    kj = np.arange(2 * ATTN_BLOCK)[None, :]
    rel = qi + ATTN_BLOCK - kj
    band = (rel >= 0) & (rel <= jmax)
    bvec = _bias_by_step(rel_bias_g, jmax, dil)
    period = 2 * ATTN_BLOCK + 1
    base = jnp.concatenate([bvec[::-1], jnp.zeros((period - jmax - 1, GROUP_HEADS), F32)], axis=0).T
    bias = jnp.tile(base, (1, ATTN_BLOCK))[:, :2 * ATTN_BLOCK * ATTN_BLOCK]
    bias = bias.reshape(GROUP_HEADS, ATTN_BLOCK, 2 * ATTN_BLOCK)
    rest = jnp.where(band[None], bias, -jnp.inf)
    first = jnp.where((band & (kj >= ATTN_BLOCK))[None], bias, -jnp.inf)
    return jnp.swapaxes(jnp.stack([first, rest], axis=0), 2, 3)


def _sample_bias(rel_bias_g, win, dil, buf_len, s_new):
    jmax = win // dil
    assert buf_len == jmax * dil
    width = buf_len + LANES
    bvec = _bias_by_step(rel_bias_g, jmax, dil)
    gaps = jnp.full((jmax + 1, dil - 1, GROUP_HEADS), -jnp.inf, F32)
    by_dist = jnp.concatenate([bvec[:, None, :], gaps], axis=1).reshape((jmax + 1) * dil, GROUP_HEADS)
    padded = jnp.pad(by_dist[::-1], ((s_new, width), (0, 0)), constant_values=-jnp.inf)
    rows = [padded[dil - 1 - s + s_new:dil - 1 - s + s_new + width] for s in range(s_new)]
    table = jnp.transpose(jnp.stack(rows, axis=0), (2, 0, 1))
    return table.reshape(GROUP_HEADS * s_new, width)


def _layer_a(x3, state, weights, *, chunk, pad_to):
    norm_a, w_cols, colscale, wkt, wg, bg, hgain, wout, col_idx, k_idx = weights
    b, t, d = x3.shape
    H = MLSTM_HEADS
    m = b * t
    di = wout.shape[0]
    dh = di // H
    tm = min(1024, m)
    p, gc, gr = _inproj(x3.reshape(m, d), norm_a, w_cols, colscale, wg, bg,
                        tm=tm, tn=2048, n_heads=H)
    if pad_to == t:
        kt3 = _kproj_t(x3, norm_a, wkt, tm=min(1024, t), tn=1024, scale=dh ** -0.5)
        p3 = p.reshape(b, t, -1)
        xin = x3
    else:
        extra = pad_to - t
        p3 = p.reshape(b, t, -1)
        k3 = p3[:, :, k_idx * di:(k_idx + 1) * di]
        kt3 = jnp.pad(jnp.swapaxes(k3, 1, 2), ((0, 0), (0, 0), (0, extra)))
        p3 = jnp.pad(p3, ((0, 0), (0, extra), (0, 0)))
        xin = jnp.pad(x3, ((0, 0), (0, extra), (0, 0)))
        lane = np.arange(LANES)
        pad_col = np.where(lane < H, -np.inf, 0.0).astype(np.float32)
        gc = jnp.concatenate([gc.reshape(b, t, LANES),
                              jnp.broadcast_to(pad_col, (b, extra, LANES))], axis=1).reshape(-1, LANES)
        pad_row = np.where(np.arange(8) < H, -np.inf, 0.0).astype(np.float32)[:, None, None]
        gr = jnp.concatenate([gr.reshape(8, b, t),
                              jnp.broadcast_to(pad_row, (8, b, extra))], axis=2).reshape(8, -1)
    xo, c_out, n_out, m_out = _scan(p3, col_idx, kt3, gc, gr, xin, wout, hgain, state,
                                    chunk=chunk, n_heads=H)
    return xo[:, :t], c_out, n_out, m_out[:, :H, 0]


def kernel(x_prompt, x_sample, state_mlstm_C, state_mlstm_n, state_mlstm_m, cache_kv_w128, cache_kv_w512, cache_kv_w2048, norm_a, w_in_a, b_gates_a, hnorm_a, w_out_a, norm_kv, w_kv, k_norm, norm_b, w_in_b, q_norm, rel_bias, w_out_b):
    H = MLSTM_HEADS
    bp, tp, d = x_prompt.shape
    bs, ts, _ = x_sample.shape
    di = w_out_a.shape[1]
    dh = di // H
    gw = GROUP_HEADS * ATTN_HEAD_DIM
    qw = len(GROUPS) * gw
    caches = (cache_kv_w128, cache_kv_w512, cache_kv_w2048)
    assert norm_a.shape[0] == 1 and norm_b.shape[0] == 1, "one mLSTM layer, one attention layer"
    for cb, (win, _) in zip(caches, GROUPS):
        assert cb.shape[1] == win, "window buffers must hold a full window"

    w_a = w_in_a[0]
    wq, wk, wv, wo, wz = (w_a[:, i * di:(i + 1) * di] for i in range(5))
    wg = jnp.pad(w_a[:, 5 * di:], ((0, 0), (0, LANES - 2 * H)))
    bg = jnp.pad(b_gates_a[0].astype(F32), (0, LANES - 2 * H))[None, :]
    na = norm_a[0].astype(F32)[None, :]
    hgain = hnorm_a[0].astype(F32)[None, :]
    wout_a = w_out_a[0].astype(BF16)
    w_prompt = jnp.concatenate([wq, wv, wo, wz], axis=1).astype(BF16)
    w_sample = w_a[:, :5 * di].astype(BF16)
    wkt = wk.T.astype(BF16)
    ones_p = jnp.ones((1, 4 * di), F32)
    scale_s = jnp.concatenate([jnp.ones((di,), F32), jnp.full((di,), dh ** -0.5, F32),
                               jnp.ones((3 * di,), F32)])[None, :]
    weights_p = (na, w_prompt, ones_p, wkt, wg, bg, hgain, wout_a, (0, 1, 2, 3), None)
    weights_s = (na, w_sample, scale_s, None, wg, bg, hgain, wout_a, (0, 2, 3, 4), 1)

    xp1, c_p, n_p, m_p = _layer_a(x_prompt, None, weights_p, chunk=256, pad_to=tp)
    m0 = jnp.pad(jnp.broadcast_to(state_mlstm_m[0].astype(F32)[:, :, None], (bs, H, LANES)),
                 ((0, 0), (0, 8 - H), (0, 0)))
    state_s = (state_mlstm_C[0].astype(F32), state_mlstm_n[0].astype(F32), m0)
    xs1, c_s, n_s, m_s = _layer_a(x_sample, state_s, weights_s, chunk=LANES, pad_to=LANES)

    nkv = norm_kv.astype(F32)[None, :]
    wkv = w_kv.astype(BF16)
    kgain = jnp.tile(k_norm.astype(F32), qw // ATTN_HEAD_DIM)[None, :]
    rows_p = [min(win, tp) for win, _ in GROUPS]
    dils = tuple(dil for _, dil in GROUPS)
    *kvp, kv128_p, kv512_p, kv2048_p = _proj_headnorm(
        xp1, nkv, wkv, kgain, n_norm=qw, scale=1.0, tm=512, tail_rows=rows_p,
        dils=dils, rest_by_group=True)
    kp, vp = kvp[:len(GROUPS)], kvp[len(GROUPS):]
    ks, vs, kv128_s, kv512_s, kv2048_s = _proj_headnorm(
        xs1, nkv, wkv, kgain, n_norm=qw, scale=1.0, tm=ts, tail_rows=[ts] * len(GROUPS))

    nb_ = norm_b[0].astype(F32)[None, :]
    wb = w_in_b[0].astype(BF16)
    qgain = jnp.tile(q_norm[0].astype(F32), qw // ATTN_HEAD_DIM)[None, :]
    wout_b = w_out_b[0].astype(BF16)
    qscale = ATTN_HEAD_DIM ** -0.5
    *qp, zp = _proj_headnorm(xp1, nb_, wb, qgain, n_norm=qw, scale=qscale, tm=512, dils=dils)
    qs, zs = _proj_headnorm(xs1, nb_, wb, qgain, n_norm=qw, scale=qscale, tm=ts)

    outs, lses = [], []
    for g, (win, dil) in enumerate(GROUPS):
        bias = _prompt_bias(rel_bias[:, g * GROUP_HEADS:(g + 1) * GROUP_HEADS], win, dil)
        o, lse = _attn_prompt(qp[g], kp[g], vp[g], bias, g)
        outs.append(o)
        lses.append(lse)
    y_p = _merge_out(outs, lses, zp, xp1, wout_b, tm=512)

    sbias = [_sample_bias(rel_bias[:, g * GROUP_HEADS:(g + 1) * GROUP_HEADS], win, dil,
                          caches[g].shape[1], ts) for g, (win, dil) in enumerate(GROUPS)]
    a_s = _attn_sample(qs, ks, vs, zs, caches, sbias)
    y_s = _matmul_residual(a_s.reshape(bs * ts, gw), xs1.reshape(bs * ts, d), wout_b,
                           tm=bs * ts).reshape(bs, ts, d)

    kv5 = lambda a: a.reshape(a.shape[0], a.shape[1], 2, GROUP_HEADS, ATTN_HEAD_DIM)
    return (y_p, y_s, c_p[None], n_p[None], m_p[None], c_s[None], n_s[None], m_s[None],
            kv5(kv128_p), kv5(kv512_p), kv5(kv2048_p), kv5(kv128_s), kv5(kv512_s), kv5(kv2048_s))
```

```python
import functools
import math

import numpy as np
import jax
import jax.numpy as jnp
from jax import lax
from jax.experimental import pallas as pl
from jax.experimental.pallas import tpu as pltpu

F32 = jnp.float32
BF16 = jnp.bfloat16

EPS = 1e-6
MLSTM_HEADS = 4
GROUPS = ((128, 1), (512, 4), (2048, 16))
GROUP_HEADS = 8
ATTN_HEAD_DIM = 64
ATTN_BLOCK = 128
N_BUCKETS = 32
MAX_DISTANCE = 2048

LANES = 128
MXU_DIM = 256
VMEM_LIMIT_BYTES = 56 * 1024 * 1024

NT_DIMS = (((1,), (1,)), ((), ()))


def _params(sem):
    return pltpu.CompilerParams(dimension_semantics=sem, vmem_limit_bytes=VMEM_LIMIT_BYTES)


def _rms_scale(xf):
    return lax.rsqrt(jnp.mean(xf * xf, axis=-1, keepdims=True) + EPS)


def _sigmoid(x):
    return 1.0 / (1.0 + jnp.exp(-x))


def _split_bf16(a):
    hi = a.astype(BF16)
    lo = (a - hi.astype(F32)).astype(BF16)
    return hi, lo


def _inproj_kernel(x_ref, g_ref, w_ref, cs_ref, wg_ref, bg_ref,
                   p_ref, gc_ref, gr_ref, xn_ref, *, n_heads):
    j = pl.program_id(1)

    @pl.when(j == 0)
    def _():
        xf = x_ref[...]
        xn = xf * _rms_scale(xf) * g_ref[...]
        xh, xl = _split_bf16(xn)
        wh, wl = _split_bf16(wg_ref[...])
        xn_ref[...] = xh
        gates = (jnp.dot(xh, wh, preferred_element_type=F32)
                 + jnp.dot(xl, wh, preferred_element_type=F32)
                 + jnp.dot(xh, wl, preferred_element_type=F32)) + bg_ref[...]
        lane = lax.broadcasted_iota(jnp.int32, gates.shape, 1)
        logsig = jnp.minimum(gates, 0.0) - jnp.log(1.0 + jnp.exp(-jnp.abs(gates)))
        gcol = jnp.where(lane < n_heads, gates, jnp.where(lane < 2 * n_heads, logsig, 0.0))
        gc_ref[...] = gcol
        gr_ref[...] = gcol.T[:8, :]

    acc = jnp.dot(xn_ref[...], w_ref[...], preferred_element_type=F32)
    p_ref[...] = (acc * cs_ref[...]).astype(BF16)


def _inproj(x2, g, w, colscale, wg, bg, *, tm, tn, n_heads):
    m, d = x2.shape
    n = w.shape[1]
    assert m % tm == 0 and n % tn == 0 and 2 * n_heads <= 8
    return pl.pallas_call(
        functools.partial(_inproj_kernel, n_heads=n_heads),
        grid=(m // tm, n // tn),
        in_specs=[
            pl.BlockSpec((tm, d), lambda i, j: (i, 0)),
            pl.BlockSpec((1, d), lambda i, j: (0, 0)),
            pl.BlockSpec((d, tn), lambda i, j: (0, j)),
            pl.BlockSpec((1, tn), lambda i, j: (0, j)),
            pl.BlockSpec((d, LANES), lambda i, j: (0, 0)),
            pl.BlockSpec((1, LANES), lambda i, j: (0, 0)),
        ],
        out_specs=[
            pl.BlockSpec((tm, tn), lambda i, j: (i, j)),
            pl.BlockSpec((tm, LANES), lambda i, j: (i, 0)),
            pl.BlockSpec((8, tm), lambda i, j: (0, i)),
        ],
        out_shape=[
            jax.ShapeDtypeStruct((m, n), BF16),
            jax.ShapeDtypeStruct((m, LANES), F32),
            jax.ShapeDtypeStruct((8, m), F32),
        ],
        scratch_shapes=[pltpu.VMEM((tm, d), BF16)],
        compiler_params=_params(("parallel", "arbitrary")),
        name="inproj",
    )(x2, g, w, colscale, wg, bg)


def _kproj_t_kernel(x_ref, g_ref, wt_ref, o_ref, xn_ref, *, scale):
    @pl.when(pl.program_id(2) == 0)
    def _():
        xf = x_ref[...]
        xn_ref[...] = (xf * _rms_scale(xf) * g_ref[...]).astype(BF16)

    acc = lax.dot_general(wt_ref[...], xn_ref[...], NT_DIMS, preferred_element_type=F32)
    o_ref[...] = (acc * scale).astype(BF16)


def _kproj_t(x3, g, wt, *, tm, tn, scale):
    b, t, d = x3.shape
    n = wt.shape[0]
    assert t % tm == 0 and n % tn == 0
    return pl.pallas_call(
        functools.partial(_kproj_t_kernel, scale=scale),
        grid=(b, t // tm, n // tn),
        in_specs=[
            pl.BlockSpec((None, tm, d), lambda bi, i, j: (bi, i, 0)),
            pl.BlockSpec((1, d), lambda bi, i, j: (0, 0)),
            pl.BlockSpec((tn, d), lambda bi, i, j: (j, 0)),
        ],
        out_specs=pl.BlockSpec((None, tn, tm), lambda bi, i, j: (bi, j, i)),
        out_shape=jax.ShapeDtypeStruct((b, n, t), BF16),
        scratch_shapes=[pltpu.VMEM((tm, d), BF16)],
        compiler_params=_params(("parallel", "parallel", "arbitrary")),
        name="kproj_t",
    )(x3, g, wt)


def _scan_kernel(*refs, chunk, n_heads, dh, n_chunks, has_state):
    L, H = chunk, n_heads
    dext = dh + LANES
    (q_ref, kt_ref, v_ref, o_ref, z_ref, gc_ref, gr_ref, x_ref, wout_ref, hg_ref) = refs[:10]
    pos = 10
    if has_state:
        c0_ref, n0_ref, m0_ref = refs[pos:pos + 3]
        pos += 3
    xo_ref, cout_ref, nout_ref, mout_ref = refs[pos:pos + 4]
    cext_ref, cb_ref, m_ref = refs[pos + 4:pos + 7]
    c = pl.program_id(1)

    @pl.when(c == 0)
    def _():
        if has_state:
            lane0 = lax.broadcasted_iota(jnp.int32, (dh, LANES), 1) == 0
            for h in range(H):
                ncol = jnp.broadcast_to(n0_ref[h:h + 1, :], (LANES, dh)).T
                cext_ref[h, :, :dh] = c0_ref[h]
                cext_ref[h, :, dh:] = jnp.where(lane0, ncol, 0.0)
            m_ref[...] = m0_ref[...]
        else:
            cext_ref[...] = jnp.zeros(cext_ref.shape, F32)
            m_ref[...] = jnp.zeros(m_ref.shape, F32)
        cb_ref[...] = cext_ref[...].astype(BF16)

    gc = gc_ref[...]
    gr = gr_ref[...]
    row = lax.broadcasted_iota(jnp.int32, (L, L), 0)
    col = lax.broadcasted_iota(jnp.int32, (L, L), 1)
    causal = row >= col
    lane = lax.broadcasted_iota(jnp.int32, gc.shape, 1)
    subl = lax.broadcasted_iota(jnp.int32, gr.shape, 0)
    lf_c_hi, lf_c_lo = _split_bf16(jnp.where((lane >= H) & (lane < 2 * H), gc, 0.0))
    lf_r_hi, lf_r_lo = _split_bf16(jnp.where((subl >= H) & (subl < 2 * H), gr, 0.0))
    lower = causal.astype(BF16)
    upper = (row <= col).astype(BF16)
    bc_all = (jnp.dot(lower, lf_c_hi, preferred_element_type=F32)
              + jnp.dot(lower, lf_c_lo, preferred_element_type=F32))
    br_all = (jnp.dot(lf_r_hi, upper, preferred_element_type=F32)
              + jnp.dot(lf_r_lo, upper, preferred_element_type=F32))
    ones_col = (lax.broadcasted_iota(jnp.int32, (L, LANES), 1) == 0).astype(BF16)

    y = jnp.zeros((L, wout_ref.shape[1]), F32)
    for h in range(H):
        sl = slice(h * dh, (h + 1) * dh)
        q = q_ref[:, sl]
        kt = kt_ref[sl, :]
        vext = jnp.concatenate([v_ref[:, sl], ones_col], axis=1)
        ig_r = gr[h:h + 1, :]
        b_c = bc_all[:, H + h:H + h + 1]
        b_r = br_all[H + h:H + h + 1, :]
        m_prev = m_ref[h:h + 1, 0:1]

        log_d = jnp.where(causal, b_c - b_r + ig_r, -jnp.inf)
        log_inter = b_c + m_prev
        m_t = jnp.maximum(log_inter, jnp.max(log_d, axis=1, keepdims=True))
        dmat = jnp.exp(log_d - m_t)
        inter = jnp.exp(log_inter - m_t)
        s = jnp.dot(q, kt, preferred_element_type=F32) * dmat
        numden = (jnp.dot(s.astype(BF16), vext, preferred_element_type=F32)
                  + inter * jnp.dot(q, cb_ref[h], preferred_element_type=F32))
        num = numden[:, :dh]
        den = numden[:, dh:dh + 1]
        hh = num / jnp.maximum(jnp.abs(den), jnp.exp(-m_t))

        b_last = b_r[:, L - 1:L]
        a_r = b_last - b_r + ig_r
        m_new = jnp.maximum(b_last + m_prev, jnp.max(a_r, axis=1, keepdims=True))
        w_r = jnp.exp(a_r - m_new)
        decay = jnp.exp(b_last + m_prev - m_new)
        ktw = (kt.astype(F32) * w_r).astype(BF16)
        c_new = decay * cext_ref[h] + jnp.dot(ktw, vext, preferred_element_type=F32)
        cext_ref[h] = c_new
        cb_ref[h] = c_new.astype(BF16)
        m_ref[h:h + 1, :] = jnp.broadcast_to(m_new, (1, LANES))

        hn = hh * lax.rsqrt(jnp.mean(hh * hh, axis=1, keepdims=True) + EPS)
        zf = z_ref[:, sl].astype(F32)
        gate = _sigmoid(o_ref[:, sl].astype(F32)) * (zf * _sigmoid(zf))
        hg = (hn * hg_ref[:, sl] * gate).astype(BF16)
        y = y + jnp.dot(hg, wout_ref[sl, :], preferred_element_type=F32)

    xo_ref[...] = x_ref[...] + y

    @pl.when(c == n_chunks - 1)
    def _():
        for h in range(H):
            cout_ref[h] = cext_ref[h, :, :dh]
            nout_ref[h:h + 1, :] = cext_ref[h, :, dh:].T[0:1, :]
        mout_ref[...] = m_ref[...]


def _scan(p3, col_idx, kt3, gc, gr, x3, wout, hgain, state, *, chunk, n_heads):
    b, t, _ = p3.shape
    dh = kt3.shape[1] // n_heads
    di = n_heads * dh
    d = x3.shape[2]
    nc = t // chunk
    assert t % chunk == 0
    has_state = state is not None
    qi, vi, oi, zi = col_idx

    def pspec(ci):
        return pl.BlockSpec((None, chunk, di), lambda bi, c, ci=ci: (bi, c, ci))

    in_specs = [
        pspec(qi),
        pl.BlockSpec((None, di, chunk), lambda bi, c: (bi, 0, c)),
        pspec(vi), pspec(oi), pspec(zi),
        pl.BlockSpec((chunk, LANES), lambda bi, c: (bi * nc + c, 0)),
        pl.BlockSpec((None, 8, chunk), lambda bi, c: (bi, 0, c)),
        pl.BlockSpec((None, chunk, d), lambda bi, c: (bi, c, 0)),
        pl.BlockSpec((di, d), lambda bi, c: (0, 0)),
        pl.BlockSpec((1, di), lambda bi, c: (0, 0)),
    ]
    args = [p3, kt3, p3, p3, p3, gc, gr, x3, wout, hgain]
    if has_state:
        c0, n0, m0 = state
        in_specs += [
            pl.BlockSpec((None, n_heads, dh, dh), lambda bi, c: (bi, 0, 0, 0)),
            pl.BlockSpec((None, n_heads, dh), lambda bi, c: (bi, 0, 0)),
            pl.BlockSpec((None, 8, LANES), lambda bi, c: (bi, 0, 0)),
        ]
        args += [c0, n0, m0]
    return pl.pallas_call(
        functools.partial(_scan_kernel, chunk=chunk, n_heads=n_heads, dh=dh,
                          n_chunks=nc, has_state=has_state),
        grid=(b, nc),
        in_specs=in_specs,
        out_specs=[
            pl.BlockSpec((None, chunk, d), lambda bi, c: (bi, c, 0)),
            pl.BlockSpec((None, n_heads, dh, dh), lambda bi, c: (bi, 0, 0, 0)),
            pl.BlockSpec((None, n_heads, dh), lambda bi, c: (bi, 0, 0)),
            pl.BlockSpec((None, 8, LANES), lambda bi, c: (bi, 0, 0)),
        ],
        out_shape=[
            jax.ShapeDtypeStruct((b, t, d), F32),
            jax.ShapeDtypeStruct((b, n_heads, dh, dh), F32),
            jax.ShapeDtypeStruct((b, n_heads, dh), F32),
            jax.ShapeDtypeStruct((b, 8, LANES), F32),
        ],
        scratch_shapes=[
            pltpu.VMEM((n_heads, dh, dh + LANES), F32),
            pltpu.VMEM((n_heads, dh, dh + LANES), BF16),
            pltpu.VMEM((8, LANES), F32),
        ],
        compiler_params=_params(("parallel", "arbitrary")),
        name="mlstm_scan",
    )(*args)


def _headnorm(a):
    n = a.shape[1]
    r = lax.broadcasted_iota(jnp.int32, (MXU_DIM, MXU_DIM), 0)
    c = lax.broadcasted_iota(jnp.int32, (MXU_DIM, MXU_DIM), 1)
    same_head = (lax.shift_right_logical(r, 6) == lax.shift_right_logical(c, 6)).astype(BF16)
    parts = []
    for c0 in range(0, n, MXU_DIM):
        blk = a[:, c0:c0 + MXU_DIM]
        ss = jnp.dot((blk * blk).astype(BF16), same_head, preferred_element_type=F32)
        parts.append(blk * lax.rsqrt(ss * (1.0 / ATTN_HEAD_DIM) + EPS))
    return jnp.concatenate(parts, axis=1)


def _store_by_residue(ref, val, scr, dil):
    rows, width = val.shape
    if dil == 1:
        ref[0] = val.astype(BF16)
        return
    for c in range(width // LANES):
        scr[c] = val[:, c * LANES:(c + 1) * LANES]
    for r in range(dil):
        parts = [scr[c, pl.ds(r, rows // dil, stride=dil), :] for c in range(width // LANES)]
        ref[r] = jnp.concatenate(parts, axis=1).astype(BF16)


def _proj_headnorm_kernel(*refs, n_norm, scale, tails, tm, dils, rest_by_group):
    x_ref, g_ref, w_ref, hg_ref = refs[:4]
    gw = GROUP_HEADS * ATTN_HEAD_DIM
    n_groups = n_norm // gw
    pos = 4
    if dils is None:
        a_refs, r_refs = refs[pos:pos + 1], refs[pos + 1:pos + 2]
        pos += 2
    else:
        a_refs = refs[pos:pos + n_groups]
        pos += n_groups
        n_rest = n_groups if rest_by_group else 1
        r_refs = refs[pos:pos + n_rest]
        pos += n_rest
    tail_refs = refs[pos:pos + len(tails)]
    scr = refs[pos + len(tails)] if dils is not None else None

    xf = x_ref[...]
    xn = (xf * _rms_scale(xf) * g_ref[...]).astype(BF16)
    p = jnp.dot(xn, w_ref[...], preferred_element_type=F32)
    a = _headnorm(p[:, :n_norm]) * hg_ref[...]
    if scale != 1.0:
        a = a * scale
    r = p[:, n_norm:]
    if dils is None:
        a_refs[0][...] = a.astype(BF16)
        r_refs[0][...] = r.astype(BF16)
    else:
        for g, dil in enumerate(dils):
            cs = slice(g * gw, (g + 1) * gw)
            _store_by_residue(a_refs[g], a[:, cs], scr, dil)
            if rest_by_group:
                _store_by_residue(r_refs[g], r[:, cs], scr, dil)
        if not rest_by_group:
            r_refs[0][...] = r.astype(BF16)
    i = pl.program_id(1)
    for g, (rows, first_tile) in enumerate(tails):
        t_ref = tail_refs[g]
        cs = slice(g * gw, (g + 1) * gw)

        @pl.when(i >= first_tile)
        def _(t_ref=t_ref, cs=cs, rows=rows):
            if rows >= tm:
                t_ref[:, :gw] = a[:, cs]
                t_ref[:, gw:] = r[:, cs]
            else:
                t_ref[:, :gw] = a[tm - rows:, cs]
                t_ref[:, gw:] = r[tm - rows:, cs]


def _proj_headnorm(x3, g, w, hgain, *, n_norm, scale, tm, tail_rows=None, dils=None,
                   rest_by_group=False):
    b, t, d = x3.shape
    n = w.shape[1]
    assert t % tm == 0 and n_norm % MXU_DIM == 0
    n_tiles = t // tm
    gw = GROUP_HEADS * ATTN_HEAD_DIM
    out_specs, out_shape, scratch = [], [], []

    def natural(width):
        out_specs.append(pl.BlockSpec((None, tm, width), lambda bi, i: (bi, i, 0)))
        out_shape.append(jax.ShapeDtypeStruct((b, t, width), BF16))

    def by_residue(dil):
        assert tm % (dil * 16) == 0
        out_specs.append(pl.BlockSpec((None, dil, tm // dil, gw), lambda bi, i: (bi, 0, i, 0)))
        out_shape.append(jax.ShapeDtypeStruct((b, dil, t // dil, gw), BF16))

    if dils is None:
        natural(n_norm)
        natural(n - n_norm)
    else:
        assert len(dils) * gw == n_norm
        for dil in dils:
            by_residue(dil)
        if rest_by_group:
            assert n - n_norm == n_norm
            for dil in dils:
                by_residue(dil)
        else:
            natural(n - n_norm)
        scratch.append(pltpu.VMEM((gw // LANES, tm, LANES), F32))
    tails = []
    for rows in (tail_rows or ()):
        if rows >= tm:
            assert rows % tm == 0
            first = n_tiles - rows // tm
            blk = tm
        else:
            first = n_tiles - 1
            blk = rows
        tails.append((rows, first))
        out_specs.append(pl.BlockSpec(
            (None, blk, 2 * gw), lambda bi, i, first=first: (bi, jnp.maximum(i - first, 0), 0)))
        out_shape.append(jax.ShapeDtypeStruct((b, rows, 2 * gw), F32))
    return pl.pallas_call(
        functools.partial(_proj_headnorm_kernel, n_norm=n_norm, scale=scale, tails=tuple(tails),
                          tm=tm, dils=dils, rest_by_group=rest_by_group),
        grid=(b, n_tiles),
        in_specs=[
            pl.BlockSpec((None, tm, d), lambda bi, i: (bi, i, 0)),
            pl.BlockSpec((1, d), lambda bi, i: (0, 0)),
            pl.BlockSpec((d, n), lambda bi, i: (0, 0)),
            pl.BlockSpec((1, n_norm), lambda bi, i: (0, 0)),
        ],
        out_specs=out_specs,
        out_shape=out_shape,
        scratch_shapes=scratch,
        compiler_params=_params(("parallel", "arbitrary")),
        name="proj_headnorm",
    )(x3, g, w, hgain)


def _attn_prompt_kernel(q_ref, kp_ref, kc_ref, vp_ref, vc_ref, bias_ref, o_ref, lse_ref, *, nq):
    blk = ATTN_BLOCK
    first = jnp.where(pl.program_id(2) == 0, 0, 1)
    low_half = lax.broadcasted_iota(jnp.int32, (blk, LANES), 1) < ATTN_HEAD_DIM
    for jb in range(nq):
        q = q_ref[jb * blk:(jb + 1) * blk, :]
        if jb == 0:
            kcat = jnp.concatenate([kp_ref[...], kc_ref[0:blk, :]], axis=0)
            vcat = jnp.concatenate([vp_ref[...], vc_ref[0:blk, :]], axis=0)
        else:
            kcat = kc_ref[(jb - 1) * blk:(jb + 1) * blk, :]
            vcat = vc_ref[(jb - 1) * blk:(jb + 1) * blk, :]
        lses, outs = [], []
        for j in range(GROUP_HEADS // 2):
            ps = slice(j * LANES, (j + 1) * LANES)
            qf = q[:, ps].astype(F32)
            kpair, vpair = kcat[:, ps], vcat[:, ps]
            pair = []
            for half in range(2):
                h = 2 * j + half
                qh = jnp.where(low_half if half == 0 else ~low_half, qf, 0.0).astype(BF16)
                bias = bias_ref[first, h] if jb == 0 else bias_ref[1, h]
                st = lax.dot_general(kpair, qh, NT_DIMS, preferred_element_type=F32) + bias
                m = jnp.max(st, axis=0, keepdims=True)
                p = jnp.exp(st - m)
                l = jnp.sum(p, axis=0, keepdims=True)
                pn = (p * (1.0 / l)).T.astype(BF16)
                pair.append(jnp.dot(pn, vpair, preferred_element_type=F32))
                lses.append(m + jnp.log(l))
            outs.append(jnp.where(low_half, pair[0], pair[1]))
        o_ref[jb * blk:(jb + 1) * blk, :] = jnp.concatenate(outs, axis=1).astype(o_ref.dtype)
        lse_ref[jb * blk:(jb + 1) * blk, :] = jnp.concatenate(
            lses + [jnp.zeros((LANES - GROUP_HEADS, blk), F32)], axis=0).T


def _attn_prompt(q, k, v, bias, g):
    b, dil, s, gw = q.shape
    nb = s // ATTN_BLOCK
    assert s % ATTN_BLOCK == 0
    nq = min(4, nb)
    assert nb % nq == 0
    rows = nq * ATTN_BLOCK
    cur = pl.BlockSpec((None, None, rows, gw), lambda bi, r, j: (bi, r, j, 0))
    prev = pl.BlockSpec((None, None, ATTN_BLOCK, gw),
                        lambda bi, r, j: (bi, r, jnp.maximum(j * nq - 1, 0), 0))
    return pl.pallas_call(
        functools.partial(_attn_prompt_kernel, nq=nq),
        grid=(b, dil, nb // nq),
        in_specs=[
            cur, prev, cur, prev, cur,
            pl.BlockSpec(bias.shape, lambda bi, r, j: (0, 0, 0, 0)),
        ],
        out_specs=[
            pl.BlockSpec((None, None, rows, gw), lambda bi, r, j: (bi, r, j, 0)),
            pl.BlockSpec((None, None, rows, LANES), lambda bi, r, j: (bi, r, j, 0)),
        ],
        out_shape=[
            jax.ShapeDtypeStruct((b, dil, s, gw), BF16),
            jax.ShapeDtypeStruct((b, dil, s, LANES), F32),
        ],
        compiler_params=_params(("parallel", "parallel", "arbitrary")),
        name="attn_prompt_g%d" % g,
    )(q, k, k, v, v, bias)


def _head_expand_matrix():
    r = lax.broadcasted_iota(jnp.int32, (LANES, GROUP_HEADS * ATTN_HEAD_DIM), 0)
    c = lax.broadcasted_iota(jnp.int32, (LANES, GROUP_HEADS * ATTN_HEAD_DIM), 1)
    return (r == lax.shift_right_logical(c, 6)).astype(BF16)


def _load_token_order(ref, scr):
    dil, per, width = ref.shape
    if dil == 1:
        return ref[0].astype(F32)
    n_tiles = width // LANES
    for r in range(dil):
        val = ref[r].astype(F32)
        for c in range(n_tiles):
            scr[c, pl.ds(r, per, stride=dil), :] = val[:, c * LANES:(c + 1) * LANES]
    return jnp.concatenate([scr[c] for c in range(n_tiles)], axis=1)


def _merge_out_kernel(o0_ref, o1_ref, o2_ref, l0_ref, l1_ref, l2_ref, z_ref, x_ref, w_ref, y_ref,
                      *scratch):
    o_refs, l_refs = (o0_ref, o1_ref, o2_ref), (l0_ref, l1_ref, l2_ref)
    os_, ls, k = [], [], 0
    for o_ref, l_ref in zip(o_refs, l_refs):
        if o_ref.shape[0] == 1:
            os_.append(o_ref[0].astype(F32))
            ls.append(l_ref[0])
        else:
            os_.append(_load_token_order(o_ref, scratch[k]))
            ls.append(_load_token_order(l_ref, scratch[k + 1]))
            k += 2
    lmax = jnp.maximum(jnp.maximum(ls[0], ls[1]), ls[2])
    es = [jnp.exp(l - lmax) for l in ls]
    tot = es[0] + es[1] + es[2]
    expand = _head_expand_matrix()
    o = jnp.zeros(os_[0].shape, F32)
    for e, og in zip(es, os_):
        hi, lo = _split_bf16(e / tot)
        wexp = (jnp.dot(hi, expand, preferred_element_type=F32)
                + jnp.dot(lo, expand, preferred_element_type=F32))
        o = o + wexp * og
    zf = z_ref[...].astype(F32)
    a = (o * (zf * _sigmoid(zf))).astype(BF16)
    y_ref[...] = x_ref[...] + jnp.dot(a, w_ref[...], preferred_element_type=F32)


def _merge_out(outs, lses, z3, x3, w, *, tm):
    b, t, d = x3.shape
    gw = w.shape[0]
    assert t % tm == 0
    row = lambda width: pl.BlockSpec((None, tm, width), lambda bi, i: (bi, i, 0))
    by_residue = lambda a: pl.BlockSpec((None, a.shape[1], tm // a.shape[1], a.shape[3]),
                                        lambda bi, i: (bi, 0, i, 0))
    scratch = []
    for o in outs:
        if o.shape[1] > 1:
            scratch += [pltpu.VMEM((gw // LANES, tm, LANES), F32), pltpu.VMEM((1, tm, LANES), F32)]
    return pl.pallas_call(
        _merge_out_kernel,
        grid=(b, t // tm),
        in_specs=[by_residue(o) for o in outs] + [by_residue(l) for l in lses]
        + [row(gw), row(d), pl.BlockSpec((gw, d), lambda bi, i: (0, 0))],
        out_specs=row(d),
        out_shape=jax.ShapeDtypeStruct((b, t, d), F32),
        scratch_shapes=scratch,
        compiler_params=_params(("parallel", "parallel")),
        name="merge_out",
    )(*outs, *lses, z3, x3, w)


def _attn_sample_kernel(q_ref, kn_ref, vn_ref, z_ref, c0_ref, c1_ref, c2_ref,
                        b0_ref, b1_ref, b2_ref, a_ref, ks0, vs0, ks1, vs1, ks2, vs2, *, s_new):
    gw = GROUP_HEADS * ATTN_HEAD_DIM
    rows = GROUP_HEADS * s_new
    r = lax.broadcasted_iota(jnp.int32, (rows, gw), 0)
    c = lax.broadcasted_iota(jnp.int32, (rows, gw), 1)
    head_mask = (lax.shift_right_logical(r, int(math.log2(s_new)))
                 == lax.shift_right_logical(c, int(math.log2(ATTN_HEAD_DIM))))
    caches = ((c0_ref, b0_ref, ks0, vs0), (c1_ref, b1_ref, ks1, vs1), (c2_ref, b2_ref, ks2, vs2))
    pad_rows = jnp.zeros((LANES - s_new, gw), F32)
    outs, lses = [], []
    for g, (c_ref, b_ref, ks, vs) in enumerate(caches):
        buf_len = c_ref.shape[2]
        cs = slice(g * gw, (g + 1) * gw)
        ks[:, :buf_len] = c_ref[0].astype(BF16)
        vs[:, :buf_len] = c_ref[1].astype(BF16)
        ks[:, buf_len:] = jnp.concatenate([kn_ref[:, cs].astype(F32), pad_rows], axis=0).T.astype(BF16)
        vs[:, buf_len:] = jnp.concatenate([vn_ref[:, cs].astype(F32), pad_rows], axis=0).T.astype(BF16)
        qg = q_ref[:, cs].astype(F32)
        qbd = jnp.where(head_mask, jnp.concatenate([qg] * GROUP_HEADS, axis=0), 0.0).astype(BF16)
        s = jnp.dot(qbd, ks[...], preferred_element_type=F32) + b_ref[...]
        m = jnp.max(s, axis=1, keepdims=True)
        p = jnp.exp(s - m)
        l = jnp.sum(p, axis=1, keepdims=True)
        outs.append(lax.dot_general(p.astype(BF16), vs[...], NT_DIMS, preferred_element_type=F32) / l)
        lses.append(m + jnp.log(l))
    lmax = jnp.maximum(jnp.maximum(lses[0], lses[1]), lses[2])
    es = [jnp.exp(l - lmax) for l in lses]
    tot = es[0] + es[1] + es[2]
    o = jnp.zeros((rows, gw), F32)
    for e, og in zip(es, outs):
        o = o + (e / tot) * og
    o = jnp.where(head_mask, o, 0.0)
    folded = o[0:s_new, :]
    for h in range(1, GROUP_HEADS):
        folded = folded + o[h * s_new:(h + 1) * s_new, :]
    zf = z_ref[...].astype(F32)
    a_ref[...] = (folded * (zf * _sigmoid(zf))).astype(BF16)


def _attn_sample(q, kn, vn, z, caches, biases):
    b, s_new, qw = q.shape
    gw = GROUP_HEADS * ATTN_HEAD_DIM
    assert s_new % 8 == 0
    cache2 = [jnp.transpose(cb, (0, 2, 3, 4, 1)).reshape(b, 2, gw, cb.shape[1]) for cb in caches]
    in_specs = [
        pl.BlockSpec((None, s_new, qw), lambda bi: (bi, 0, 0)),
        pl.BlockSpec((None, s_new, qw), lambda bi: (bi, 0, 0)),
        pl.BlockSpec((None, s_new, qw), lambda bi: (bi, 0, 0)),
        pl.BlockSpec((None, s_new, gw), lambda bi: (bi, 0, 0)),
    ]
    for cb in cache2:
        in_specs.append(pl.BlockSpec((None, 2, gw, cb.shape[3]), lambda bi: (bi, 0, 0, 0)))
    for bt in biases:
        in_specs.append(pl.BlockSpec(bt.shape, lambda bi: (0, 0)))
    scratch = []
    for cb in cache2:
        scratch += [pltpu.VMEM((gw, cb.shape[3] + LANES), BF16)] * 2
    return pl.pallas_call(
        functools.partial(_attn_sample_kernel, s_new=s_new),
        grid=(b,),
        in_specs=in_specs,
        out_specs=pl.BlockSpec((None, s_new, gw), lambda bi: (bi, 0, 0)),
        out_shape=jax.ShapeDtypeStruct((b, s_new, gw), BF16),
        scratch_shapes=scratch,
        compiler_params=_params(("arbitrary",)),
        name="attn_sample",
    )(q, kn, vn, z, *cache2, *biases)


def _matmul_residual_kernel(a_ref, x_ref, w_ref, y_ref):
    y_ref[...] = x_ref[...] + jnp.dot(a_ref[...], w_ref[...], preferred_element_type=F32)


def _matmul_residual(a2, x2, w, *, tm):
    m, d = x2.shape
    kdim = a2.shape[1]
    assert m % tm == 0
    return pl.pallas_call(
        _matmul_residual_kernel,
        grid=(m // tm,),
        in_specs=[
            pl.BlockSpec((tm, kdim), lambda i: (i, 0)),
            pl.BlockSpec((tm, d), lambda i: (i, 0)),
            pl.BlockSpec((kdim, d), lambda i: (0, 0)),
        ],
        out_specs=pl.BlockSpec((tm, d), lambda i: (i, 0)),
        out_shape=jax.ShapeDtypeStruct((m, d), F32),
        compiler_params=_params(("parallel",)),
        name="matmul_residual",
    )(a2, x2, w)


def _t5_bucket_np(dist):
    exact = N_BUCKETS // 2
    d = np.maximum(dist, 1).astype(np.float32)
    large = exact + (np.log(d / np.float32(exact)) / np.float32(math.log(MAX_DISTANCE / exact))
                     * np.float32(N_BUCKETS - exact)).astype(np.int32)
    return np.where(dist < exact, dist, np.minimum(large, N_BUCKETS - 1)).astype(np.int32)


def _bias_by_step(rel_bias_g, jmax, dil):
    return rel_bias_g.astype(F32)[_t5_bucket_np(np.arange(jmax + 1) * dil)]


def _prompt_bias(rel_bias_g, win, dil):
    jmax = win // dil
    assert jmax == ATTN_BLOCK
    qi = np.arange(ATTN_BLOCK)[:, None]
    kj = np.arange(2 * ATTN_BLOCK)[None, :]
    rel = qi + ATTN_BLOCK - kj
    band = (rel >= 0) & (rel <= jmax)
    bvec = _bias_by_step(rel_bias_g, jmax, dil)
    period = 2 * ATTN_BLOCK + 1
    base = jnp.concatenate([bvec[::-1], jnp.zeros((period - jmax - 1, GROUP_HEADS), F32)], axis=0).T
    bias = jnp.tile(base, (1, ATTN_BLOCK))[:, :2 * ATTN_BLOCK * ATTN_BLOCK]
    bias = bias.reshape(GROUP_HEADS, ATTN_BLOCK, 2 * ATTN_BLOCK)
    rest = jnp.where(band[None], bias, -jnp.inf)
    first = jnp.where((band & (kj >= ATTN_BLOCK))[None], bias, -jnp.inf)
    return jnp.swapaxes(jnp.stack([first, rest], axis=0), 2, 3)


def _sample_bias(rel_bias_g, win, dil, buf_len, s_new):
    jmax = win // dil
    assert buf_len == jmax * dil
    width = buf_len + LANES
    bvec = _bias_by_step(rel_bias_g, jmax, dil)
    gaps = jnp.full((jmax + 1, dil - 1, GROUP_HEADS), -jnp.inf, F32)
    by_dist = jnp.concatenate([bvec[:, None, :], gaps], axis=1).reshape((jmax + 1) * dil, GROUP_HEADS)
    padded = jnp.pad(by_dist[::-1], ((s_new, width), (0, 0)), constant_values=-jnp.inf)
    rows = [padded[dil - 1 - s + s_new:dil - 1 - s + s_new + width] for s in range(s_new)]
    table = jnp.transpose(jnp.stack(rows, axis=0), (2, 0, 1))
    return table.reshape(GROUP_HEADS * s_new, width)


def _layer_a(x3, state, weights, *, chunk, pad_to):
    norm_a, w_cols, colscale, wkt, wg, bg, hgain, wout, col_idx, k_idx = weights
    b, t, d = x3.shape
    H = MLSTM_HEADS
    m = b * t
    di = wout.shape[0]
    dh = di // H
    tm = min(1024, m)
    p, gc, gr = _inproj(x3.reshape(m, d), norm_a, w_cols, colscale, wg, bg,
                        tm=tm, tn=2048, n_heads=H)
    if pad_to == t:
        kt3 = _kproj_t(x3, norm_a, wkt, tm=min(1024, t), tn=1024, scale=dh ** -0.5)
        p3 = p.reshape(b, t, -1)
        xin = x3
    else:
        extra = pad_to - t
        p3 = p.reshape(b, t, -1)
        k3 = p3[:, :, k_idx * di:(k_idx + 1) * di]
        kt3 = jnp.pad(jnp.swapaxes(k3, 1, 2), ((0, 0), (0, 0), (0, extra)))
        p3 = jnp.pad(p3, ((0, 0), (0, extra), (0, 0)))
        xin = jnp.pad(x3, ((0, 0), (0, extra), (0, 0)))
        lane = np.arange(LANES)
        pad_col = np.where(lane < H, -np.inf, 0.0).astype(np.float32)
        gc = jnp.concatenate([gc.reshape(b, t, LANES),
                              jnp.broadcast_to(pad_col, (b, extra, LANES))], axis=1).reshape(-1, LANES)
        pad_row = np.where(np.arange(8) < H, -np.inf, 0.0).astype(np.float32)[:, None, None]
        gr = jnp.concatenate([gr.reshape(8, b, t),
                              jnp.broadcast_to(pad_row, (8, b, extra))], axis=2).reshape(8, -1)
    gr3 = jnp.swapaxes(gr.reshape(8, b, pad_to), 0, 1)
    xo, c_out, n_out, m_out = _scan(p3, col_idx, kt3, gc, gr3, xin, wout, hgain, state,
                                    chunk=chunk, n_heads=H)
    return xo[:, :t], c_out, n_out, m_out[:, :H, 0]


def kernel(x_prompt, x_sample, state_mlstm_C, state_mlstm_n, state_mlstm_m, cache_kv_w128, cache_kv_w512, cache_kv_w2048, norm_a, w_in_a, b_gates_a, hnorm_a, w_out_a, norm_kv, w_kv, k_norm, norm_b, w_in_b, q_norm, rel_bias, w_out_b):
    H = MLSTM_HEADS
    bp, tp, d = x_prompt.shape
    bs, ts, _ = x_sample.shape
    di = w_out_a.shape[1]
    dh = di // H
    gw = GROUP_HEADS * ATTN_HEAD_DIM
    qw = len(GROUPS) * gw
    caches = (cache_kv_w128, cache_kv_w512, cache_kv_w2048)
    assert norm_a.shape[0] == 1 and norm_b.shape[0] == 1, "one mLSTM layer, one attention layer"
    for cb, (win, _) in zip(caches, GROUPS):
        assert cb.shape[1] == win, "window buffers must hold a full window"

    w_a = w_in_a[0]
    wq, wk, wv, wo, wz = (w_a[:, i * di:(i + 1) * di] for i in range(5))
    wg = jnp.pad(w_a[:, 5 * di:], ((0, 0), (0, LANES - 2 * H)))
    bg = jnp.pad(b_gates_a[0].astype(F32), (0, LANES - 2 * H))[None, :]
    na = norm_a[0].astype(F32)[None, :]
    hgain = hnorm_a[0].astype(F32)[None, :]
    wout_a = w_out_a[0].astype(BF16)
    w_prompt = jnp.concatenate([wq, wv, wo, wz], axis=1).astype(BF16)
    w_sample = w_a[:, :5 * di].astype(BF16)
    wkt = wk.T.astype(BF16)
    ones_p = jnp.ones((1, 4 * di), F32)
    scale_s = jnp.concatenate([jnp.ones((di,), F32), jnp.full((di,), dh ** -0.5, F32),
                               jnp.ones((3 * di,), F32)])[None, :]
    weights_p = (na, w_prompt, ones_p, wkt, wg, bg, hgain, wout_a, (0, 1, 2, 3), None)
    weights_s = (na, w_sample, scale_s, None, wg, bg, hgain, wout_a, (0, 2, 3, 4), 1)

    xp1, c_p, n_p, m_p = _layer_a(x_prompt, None, weights_p, chunk=256, pad_to=tp)
    m0 = jnp.pad(jnp.broadcast_to(state_mlstm_m[0].astype(F32)[:, :, None], (bs, H, LANES)),
                 ((0, 0), (0, 8 - H), (0, 0)))
    state_s = (state_mlstm_C[0].astype(F32), state_mlstm_n[0].astype(F32), m0)
    xs1, c_s, n_s, m_s = _layer_a(x_sample, state_s, weights_s, chunk=16, pad_to=16)

    nkv = norm_kv.astype(F32)[None, :]
    wkv = w_kv.astype(BF16)
    kgain = jnp.tile(k_norm.astype(F32), qw // ATTN_HEAD_DIM)[None, :]
    rows_p = [min(win, tp) for win, _ in GROUPS]
    dils = tuple(dil for _, dil in GROUPS)
    *kvp, kv128_p, kv512_p, kv2048_p = _proj_headnorm(
        xp1, nkv, wkv, kgain, n_norm=qw, scale=1.0, tm=512, tail_rows=rows_p,
        dils=dils, rest_by_group=True)
    kp, vp = kvp[:len(GROUPS)], kvp[len(GROUPS):]
    ks, vs, kv128_s, kv512_s, kv2048_s = _proj_headnorm(
        xs1, nkv, wkv, kgain, n_norm=qw, scale=1.0, tm=ts, tail_rows=[ts] * len(GROUPS))

    nb_ = norm_b[0].astype(F32)[None, :]
    wb = w_in_b[0].astype(BF16)
    qgain = jnp.tile(q_norm[0].astype(F32), qw // ATTN_HEAD_DIM)[None, :]
    wout_b = w_out_b[0].astype(BF16)
    qscale = ATTN_HEAD_DIM ** -0.5
    *qp, zp = _proj_headnorm(xp1, nb_, wb, qgain, n_norm=qw, scale=qscale, tm=512, dils=dils)
    qs, zs = _proj_headnorm(xs1, nb_, wb, qgain, n_norm=qw, scale=qscale, tm=ts)

    outs, lses = [], []
    for g, (win, dil) in enumerate(GROUPS):
        bias = _prompt_bias(rel_bias[:, g * GROUP_HEADS:(g + 1) * GROUP_HEADS], win, dil)
        o, lse = _attn_prompt(qp[g], kp[g], vp[g], bias, g)
        outs.append(o)
        lses.append(lse)
    y_p = _merge_out(outs, lses, zp, xp1, wout_b, tm=512)

    sbias = [_sample_bias(rel_bias[:, g * GROUP_HEADS:(g + 1) * GROUP_HEADS], win, dil,
                          caches[g].shape[1], ts) for g, (win, dil) in enumerate(GROUPS)]
    a_s = _attn_sample(qs, ks, vs, zs, caches, sbias)
    y_s = _matmul_residual(a_s.reshape(bs * ts, gw), xs1.reshape(bs * ts, d), wout_b,
                           tm=bs * ts).reshape(bs, ts, d)

    kv5 = lambda a: a.reshape(a.shape[0], a.shape[1], 2, GROUP_HEADS, ATTN_HEAD_DIM)
    return (y_p, y_s, c_p[None], n_p[None], m_p[None], c_s[None], n_s[None], m_s[None],
            kv5(kv128_p), kv5(kv512_p), kv5(kv2048_p), kv5(kv128_s), kv5(kv512_s), kv5(kv2048_s))
```

```python
import functools
import math

import numpy as np
import jax
import jax.numpy as jnp
from jax import lax
from jax.experimental import pallas as pl
from jax.experimental.pallas import tpu as pltpu

F32 = jnp.float32
BF16 = jnp.bfloat16

EPS = 1e-6
MLSTM_HEADS = 4
GROUPS = ((128, 1), (512, 4), (2048, 16))
GROUP_HEADS = 8
ATTN_HEAD_DIM = 64
ATTN_BLOCK = 128
N_BUCKETS = 32
MAX_DISTANCE = 2048

LANES = 128
MXU_DIM = 256
VMEM_LIMIT_BYTES = 56 * 1024 * 1024

NT_DIMS = (((1,), (1,)), ((), ()))


def _params(sem):
    return pltpu.CompilerParams(dimension_semantics=sem, vmem_limit_bytes=VMEM_LIMIT_BYTES)


def _rms_scale(xf):
    return lax.rsqrt(jnp.mean(xf * xf, axis=-1, keepdims=True) + EPS)


def _sigmoid(x):
    return 1.0 / (1.0 + jnp.exp(-x))


def _split_bf16(a):
    hi = a.astype(BF16)
    lo = (a - hi.astype(F32)).astype(BF16)
    return hi, lo


def _inproj_kernel(*refs, n_heads, k_tile, k_scale, use_colscale):
    x_ref, g_ref, wt_ref, cs_ref, wgt_ref, bg_ref, p_ref, gc_ref, gr_ref = refs[:9]
    kt_ref = refs[9] if k_tile is not None else None
    xn_ref = refs[-1]
    j = pl.program_id(2)

    @pl.when(j == 0)
    def _():
        xf = x_ref[...]
        xn = xf * _rms_scale(xf) * g_ref[...]
        xh, xl = _split_bf16(xn)
        wh, wl = _split_bf16(wgt_ref[...])
        xn_ref[...] = xh
        gates = (lax.dot_general(xh, wh, NT_DIMS, preferred_element_type=F32)
                 + lax.dot_general(xl, wh, NT_DIMS, preferred_element_type=F32)
                 + lax.dot_general(xh, wl, NT_DIMS, preferred_element_type=F32)) + bg_ref[...]
        lane = lax.broadcasted_iota(jnp.int32, gates.shape, 1)
        logsig = jnp.minimum(gates, 0.0) - jnp.log(1.0 + jnp.exp(-jnp.abs(gates)))
        gcol = jnp.where(lane < n_heads, gates, jnp.where(lane < 2 * n_heads, logsig, 0.0))
        gc_ref[...] = gcol
        gr_ref[...] = gcol.T[:8, :]

    def token_major():
        acc = lax.dot_general(xn_ref[...], wt_ref[...], NT_DIMS, preferred_element_type=F32)
        if use_colscale:
            acc = acc * cs_ref[...]
        p_ref[...] = acc.astype(BF16)

    if k_tile is None:
        token_major()
    else:
        pl.when(j != k_tile)(token_major)

        @pl.when(j == k_tile)
        def _():
            acc = lax.dot_general(wt_ref[...], xn_ref[...], NT_DIMS, preferred_element_type=F32)
            kt_ref[...] = (acc * k_scale).astype(BF16)


def _inproj(x3, g, wt, colscale, wgt, bg, *, tm, tn, n_heads, k_tile=None, k_scale=1.0):
    b, t, d = x3.shape
    n = wt.shape[0]
    assert t % tm == 0 and n % tn == 0 and 2 * n_heads <= 8
    nj = n // tn
    if k_tile is None:
        p_tile = lambda j: j
        n_out = n
    else:
        assert 1 <= k_tile < nj
        p_tile = lambda j: jnp.where(j < k_tile, j, j - 1)
        n_out = n - tn
    out_specs = [
        pl.BlockSpec((None, tm, tn), lambda bi, i, j: (bi, i, p_tile(j))),
        pl.BlockSpec((None, tm, LANES), lambda bi, i, j: (bi, i, 0)),
        pl.BlockSpec((None, 8, tm), lambda bi, i, j: (bi, 0, i)),
    ]
    out_shape = [
        jax.ShapeDtypeStruct((b, t, n_out), BF16),
        jax.ShapeDtypeStruct((b, t, LANES), F32),
        jax.ShapeDtypeStruct((b, 8, t), F32),
    ]
    if k_tile is not None:
        out_specs.append(pl.BlockSpec((None, tn, tm), lambda bi, i, j: (bi, 0, i)))
        out_shape.append(jax.ShapeDtypeStruct((b, tn, t), BF16))
    use_colscale = colscale is not None
    if colscale is None:
        colscale = jnp.ones((1, n), F32)
    return pl.pallas_call(
        functools.partial(_inproj_kernel, n_heads=n_heads, k_tile=k_tile, k_scale=k_scale,
                          use_colscale=use_colscale),
        grid=(b, t // tm, nj),
        in_specs=[
            pl.BlockSpec((None, tm, d), lambda bi, i, j: (bi, i, 0)),
            pl.BlockSpec((1, d), lambda bi, i, j: (0, 0)),
            pl.BlockSpec((tn, d), lambda bi, i, j: (j, 0)),
            pl.BlockSpec((1, tn), lambda bi, i, j: (0, j)),
            pl.BlockSpec((LANES, d), lambda bi, i, j: (0, 0)),
            pl.BlockSpec((1, LANES), lambda bi, i, j: (0, 0)),
        ],
        out_specs=out_specs,
        out_shape=out_shape,
        scratch_shapes=[pltpu.VMEM((tm, d), BF16)],
        compiler_params=_params(("parallel", "parallel", "arbitrary")),
        name="inproj",
    )(x3, g, wt, colscale, wgt, bg)


def _scan_kernel(*refs, chunk, n_heads, dh, n_chunks, has_state):
    L, H = chunk, n_heads
    dext = dh + LANES
    (q_ref, kt_ref, v_ref, o_ref, z_ref, gc_ref, gr_ref, x_ref, wout_ref, hg_ref) = refs[:10]
    pos = 10
    if has_state:
        c0_ref, n0_ref, m0_ref = refs[pos:pos + 3]
        pos += 3
    xo_ref, cout_ref, nout_ref, mout_ref = refs[pos:pos + 4]
    cext_ref, cb_ref, m_ref = refs[pos + 4:pos + 7]
    c = pl.program_id(1)

    @pl.when(c == 0)
    def _():
        if has_state:
            lane0 = lax.broadcasted_iota(jnp.int32, (dh, LANES), 1) == 0
            for h in range(H):
                ncol = jnp.broadcast_to(n0_ref[h:h + 1, :], (LANES, dh)).T
                cext_ref[h, :, :dh] = c0_ref[h]
                cext_ref[h, :, dh:] = jnp.where(lane0, ncol, 0.0)
            m_ref[...] = m0_ref[...]
        else:
            cext_ref[...] = jnp.zeros(cext_ref.shape, F32)
            m_ref[...] = jnp.zeros(m_ref.shape, F32)
        cb_ref[...] = cext_ref[...].astype(BF16)

    gc = gc_ref[...]
    gr = gr_ref[...]
    row = lax.broadcasted_iota(jnp.int32, (L, L), 0)
    col = lax.broadcasted_iota(jnp.int32, (L, L), 1)
    causal = row >= col
    lane = lax.broadcasted_iota(jnp.int32, gc.shape, 1)
    subl = lax.broadcasted_iota(jnp.int32, gr.shape, 0)
    lf_c_hi, lf_c_lo = _split_bf16(jnp.where((lane >= H) & (lane < 2 * H), gc, 0.0))
    lf_r_hi, lf_r_lo = _split_bf16(jnp.where((subl >= H) & (subl < 2 * H), gr, 0.0))
    lower = causal.astype(BF16)
    upper = (row <= col).astype(BF16)
    bc_all = (jnp.dot(lower, lf_c_hi, preferred_element_type=F32)
              + jnp.dot(lower, lf_c_lo, preferred_element_type=F32))
    br_all = (jnp.dot(lf_r_hi, upper, preferred_element_type=F32)
              + jnp.dot(lf_r_lo, upper, preferred_element_type=F32))
    ones_col = (lax.broadcasted_iota(jnp.int32, (L, LANES), 1) == 0).astype(BF16)
    y = jnp.zeros((L, wout_ref.shape[1]), F32)

    for h in range(H):
        sl = slice(h * dh, (h + 1) * dh)
        q = q_ref[:, sl]
        kt = kt_ref[sl, :]
        vext = jnp.concatenate([v_ref[:, sl], ones_col], axis=1)
        ig_r = gr[h:h + 1, :]
        b_c = bc_all[:, H + h:H + h + 1]
        b_r = br_all[H + h:H + h + 1, :]
        m_prev = m_ref[h:h + 1, 0:1]

        log_d = jnp.where(causal, b_c - b_r + ig_r, -jnp.inf)
        log_inter = b_c + m_prev
        m_t = jnp.maximum(log_inter, jnp.max(log_d, axis=1, keepdims=True))
        dmat = jnp.exp(log_d - m_t)
        inter = jnp.exp(log_inter - m_t)
        s = jnp.dot(q, kt, preferred_element_type=F32) * dmat
        numden = (jnp.dot(s.astype(BF16), vext, preferred_element_type=F32)
                  + inter * jnp.dot(q, cb_ref[h], preferred_element_type=F32))
        num = numden[:, :dh]
        den = numden[:, dh:dh + 1]
        hh = num / jnp.maximum(jnp.abs(den), jnp.exp(-m_t))

        b_last = b_r[:, L - 1:L]
        a_r = b_last - b_r + ig_r
        m_new = jnp.maximum(b_last + m_prev, jnp.max(a_r, axis=1, keepdims=True))
        w_r = jnp.exp(a_r - m_new)
        decay = jnp.exp(b_last + m_prev - m_new)
        ktw = (kt.astype(F32) * w_r).astype(BF16)
        c_new = decay * cext_ref[h] + jnp.dot(ktw, vext, preferred_element_type=F32)
        cext_ref[h] = c_new
        cb_ref[h] = c_new.astype(BF16)
        m_ref[h:h + 1, :] = jnp.broadcast_to(m_new, (1, LANES))

        hn = hh * lax.rsqrt(jnp.mean(hh * hh, axis=1, keepdims=True) + EPS)
        zf = z_ref[:, sl].astype(F32)
        gate = _sigmoid(o_ref[:, sl].astype(F32)) * (zf * _sigmoid(zf))
        hg = (hn * hg_ref[:, sl] * gate).astype(BF16)
        y = y + jnp.dot(hg, wout_ref[sl, :], preferred_element_type=F32)

    xo_ref[...] = x_ref[...] + y

    @pl.when(c == n_chunks - 1)
    def _():
        for h in range(H):
            cout_ref[h] = cext_ref[h, :, :dh]
            nout_ref[h:h + 1, :] = cext_ref[h, :, dh:].T[0:1, :]
        mout_ref[...] = m_ref[...]


def _scan(p3, col_idx, kt3, gc, gr, x3, wout, hgain, state, *, chunk, n_heads):
    b, t, _ = p3.shape
    dh = kt3.shape[1] // n_heads
    di = n_heads * dh
    d = x3.shape[2]
    nc = t // chunk
    assert t % chunk == 0
    has_state = state is not None
    qi, vi, oi, zi = col_idx

    def pspec(ci):
        return pl.BlockSpec((None, chunk, di), lambda bi, c, ci=ci: (bi, c, ci))

    in_specs = [
        pspec(qi),
        pl.BlockSpec((None, di, chunk), lambda bi, c: (bi, 0, c)),
        pspec(vi), pspec(oi), pspec(zi),
        pl.BlockSpec((None, chunk, LANES), lambda bi, c: (bi, c, 0)),
        pl.BlockSpec((None, 8, chunk), lambda bi, c: (bi, 0, c)),
        pl.BlockSpec((None, chunk, d), lambda bi, c: (bi, c, 0)),
        pl.BlockSpec((di, d), lambda bi, c: (0, 0)),
        pl.BlockSpec((1, di), lambda bi, c: (0, 0)),
    ]
    args = [p3, kt3, p3, p3, p3, gc, gr, x3, wout, hgain]
    if has_state:
        c0, n0, m0 = state
        in_specs += [
            pl.BlockSpec((None, n_heads, dh, dh), lambda bi, c: (bi, 0, 0, 0)),
            pl.BlockSpec((None, n_heads, dh), lambda bi, c: (bi, 0, 0)),
            pl.BlockSpec((None, 8, LANES), lambda bi, c: (bi, 0, 0)),
        ]
        args += [c0, n0, m0]
    return pl.pallas_call(
        functools.partial(_scan_kernel, chunk=chunk, n_heads=n_heads, dh=dh,
                          n_chunks=nc, has_state=has_state),
        grid=(b, nc),
        in_specs=in_specs,
        out_specs=[
            pl.BlockSpec((None, chunk, d), lambda bi, c: (bi, c, 0)),
            pl.BlockSpec((None, n_heads, dh, dh), lambda bi, c: (bi, 0, 0, 0)),
            pl.BlockSpec((None, n_heads, dh), lambda bi, c: (bi, 0, 0)),
            pl.BlockSpec((None, 8, LANES), lambda bi, c: (bi, 0, 0)),
        ],
        out_shape=[
            jax.ShapeDtypeStruct((b, t, d), F32),
            jax.ShapeDtypeStruct((b, n_heads, dh, dh), F32),
            jax.ShapeDtypeStruct((b, n_heads, dh), F32),
            jax.ShapeDtypeStruct((b, 8, LANES), F32),
        ],
        scratch_shapes=[
            pltpu.VMEM((n_heads, dh, dh + LANES), F32),
            pltpu.VMEM((n_heads, dh, dh + LANES), BF16),
            pltpu.VMEM((8, LANES), F32),
        ],
        compiler_params=_params(("parallel", "arbitrary")),
        name="mlstm_scan",
    )(*args)


def _headnorm(a):
    n = a.shape[1]
    r = lax.broadcasted_iota(jnp.int32, (MXU_DIM, MXU_DIM), 0)
    c = lax.broadcasted_iota(jnp.int32, (MXU_DIM, MXU_DIM), 1)
    same_head = (lax.shift_right_logical(r, 6) == lax.shift_right_logical(c, 6)).astype(BF16)
    parts = []
    for c0 in range(0, n, MXU_DIM):
        blk = a[:, c0:c0 + MXU_DIM]
        ss = jnp.dot((blk * blk).astype(BF16), same_head, preferred_element_type=F32)
        parts.append(blk * lax.rsqrt(ss * (1.0 / ATTN_HEAD_DIM) + EPS))
    return jnp.concatenate(parts, axis=1)


def _store_by_residue(ref, val, scr, dil):
    rows, width = val.shape
    if dil == 1:
        ref[0] = val.astype(BF16)
        return
    for c in range(width // LANES):
        scr[c] = val[:, c * LANES:(c + 1) * LANES]
    for r in range(dil):
        parts = [scr[c, pl.ds(r, rows // dil, stride=dil), :] for c in range(width // LANES)]
        ref[r] = jnp.concatenate(parts, axis=1).astype(BF16)


def _proj_headnorm_kernel(*refs, n_norm, scale, tails, tm, dils, rest_by_group):
    x_ref, g_ref, w_ref, hg_ref = refs[:4]
    gw = GROUP_HEADS * ATTN_HEAD_DIM
    n_groups = n_norm // gw
    n = w_ref.shape[1]
    pos = 4
    if dils is None:
        a_refs, r_refs = refs[pos:pos + 1], refs[pos + 1:pos + 2]
        pos += 2
    else:
        a_refs = refs[pos:pos + n_groups]
        pos += n_groups
        n_rest = n_groups if rest_by_group else 1
        r_refs = refs[pos:pos + n_rest]
        pos += n_rest
    tail_refs = refs[pos:pos + len(tails)]
    scr = refs[pos + len(tails)] if dils is not None else None

    xf = x_ref[...]
    xn = (xf * _rms_scale(xf) * g_ref[...]).astype(BF16)

    def project(c0, width):
        return jnp.dot(xn, w_ref[:, c0:c0 + width], preferred_element_type=F32)

    for g in range(n_groups):
        cs = slice(g * gw, (g + 1) * gw)
        a = _headnorm(project(g * gw, gw)) * hg_ref[:, cs]
        if scale != 1.0:
            a = a * scale
        r = project(n_norm + g * gw, gw) if (rest_by_group or dils is None) and n - n_norm == n_norm \
            else None
        if dils is None:
            a_refs[0][:, cs] = a.astype(BF16)
            if r is not None:
                r_refs[0][:, cs] = r.astype(BF16)
        else:
            _store_by_residue(a_refs[g], a, scr, dils[g])
            if rest_by_group:
                _store_by_residue(r_refs[g], r, scr, dils[g])
        if tails:
            rows, t_ref = tails[g][0], tail_refs[g]
            t_ref[:, :gw] = a[max(tm - rows, 0):, :]
            t_ref[:, gw:] = r[max(tm - rows, 0):, :]
    if n - n_norm != n_norm:
        r_refs[0][...] = project(n_norm, n - n_norm).astype(BF16)


def _proj_headnorm(x3, g, w, hgain, *, n_norm, scale, tm, tail_rows=None, dils=None,
                   rest_by_group=False):
    b, t, d = x3.shape
    n = w.shape[1]
    assert t % tm == 0 and n_norm % MXU_DIM == 0
    n_tiles = t // tm
    gw = GROUP_HEADS * ATTN_HEAD_DIM
    out_specs, out_shape, scratch = [], [], []

    def natural(width):
        out_specs.append(pl.BlockSpec((None, tm, width), lambda bi, i: (bi, i, 0)))
        out_shape.append(jax.ShapeDtypeStruct((b, t, width), BF16))

    def by_residue(dil):
        assert tm % (dil * 16) == 0
        out_specs.append(pl.BlockSpec((None, dil, tm // dil, gw), lambda bi, i: (bi, 0, i, 0)))
        out_shape.append(jax.ShapeDtypeStruct((b, dil, t // dil, gw), BF16))

    if dils is None:
        natural(n_norm)
        natural(n - n_norm)
    else:
        assert len(dils) * gw == n_norm
        for dil in dils:
            by_residue(dil)
        if rest_by_group:
            assert n - n_norm == n_norm
            for dil in dils:
                by_residue(dil)
        else:
            natural(n - n_norm)
        scratch.append(pltpu.VMEM((gw // LANES, tm, LANES), F32))
    tails = []
    for rows in (tail_rows or ()):
        if rows >= tm:
            assert rows % tm == 0
            first = n_tiles - rows // tm
            blk = tm
        else:
            first = n_tiles - 1
            blk = rows
        tails.append((rows, first))
        out_specs.append(pl.BlockSpec(
            (None, blk, 2 * gw), lambda bi, i, first=first: (bi, jnp.maximum(i - first, 0), 0)))
        out_shape.append(jax.ShapeDtypeStruct((b, rows, 2 * gw), F32))
    return pl.pallas_call(
        functools.partial(_proj_headnorm_kernel, n_norm=n_norm, scale=scale, tails=tuple(tails),
                          tm=tm, dils=dils, rest_by_group=rest_by_group),
        grid=(b, n_tiles),
        in_specs=[
            pl.BlockSpec((None, tm, d), lambda bi, i: (bi, i, 0)),
            pl.BlockSpec((1, d), lambda bi, i: (0, 0)),
            pl.BlockSpec((d, n), lambda bi, i: (0, 0)),
            pl.BlockSpec((1, n_norm), lambda bi, i: (0, 0)),
        ],
        out_specs=out_specs,
        out_shape=out_shape,
        scratch_shapes=scratch,
        compiler_params=_params(("parallel", "arbitrary")),
        name="proj_headnorm",
    )(x3, g, w, hgain)


def _attn_prompt_kernel(q_ref, kp_ref, kc_ref, vp_ref, vc_ref, bias_ref, o_ref, lse_ref, *, nq):
    blk = ATTN_BLOCK
    first = jnp.where(pl.program_id(2) == 0, 0, 1)
    low_half = lax.broadcasted_iota(jnp.int32, (blk, LANES), 1) < ATTN_HEAD_DIM
    for jb in range(nq):
        q = q_ref[jb * blk:(jb + 1) * blk, :]
        if jb == 0:
            kcat = jnp.concatenate([kp_ref[...], kc_ref[0:blk, :]], axis=0)
            vcat = jnp.concatenate([vp_ref[...], vc_ref[0:blk, :]], axis=0)
        else:
            kcat = kc_ref[(jb - 1) * blk:(jb + 1) * blk, :]
            vcat = vc_ref[(jb - 1) * blk:(jb + 1) * blk, :]
        lses, outs = [], []
        for j in range(GROUP_HEADS // 2):
            ps = slice(j * LANES, (j + 1) * LANES)
            qf = q[:, ps].astype(F32)
            kpair, vpair = kcat[:, ps], vcat[:, ps]
            pair = []
            for half in range(2):
                h = 2 * j + half
                qh = jnp.where(low_half if half == 0 else ~low_half, qf, 0.0).astype(BF16)
                bias = bias_ref[first, h] if jb == 0 else bias_ref[1, h]
                st = lax.dot_general(kpair, qh, NT_DIMS, preferred_element_type=F32) + bias
                m = jnp.max(st, axis=0, keepdims=True)
                p = jnp.exp(st - m)
                l = jnp.sum(p, axis=0, keepdims=True)
                pn = (p * (1.0 / l)).T.astype(BF16)
                pair.append(jnp.dot(pn, vpair, preferred_element_type=F32))
                lses.append(m + jnp.log(l))
            outs.append(jnp.where(low_half, pair[0], pair[1]))
        o_ref[jb * blk:(jb + 1) * blk, :] = jnp.concatenate(outs, axis=1).astype(o_ref.dtype)
        lse_ref[jb * blk:(jb + 1) * blk, :] = jnp.concatenate(
            lses + [jnp.zeros((LANES - GROUP_HEADS, blk), F32)], axis=0).T


def _attn_prompt(q, k, v, bias, g):
    b, dil, s, gw = q.shape
    nb = s // ATTN_BLOCK
    assert s % ATTN_BLOCK == 0
    nq = min(4, nb)
    assert nb % nq == 0
    rows = nq * ATTN_BLOCK
    cur = pl.BlockSpec((None, None, rows, gw), lambda bi, r, j: (bi, r, j, 0))
    prev = pl.BlockSpec((None, None, ATTN_BLOCK, gw),
                        lambda bi, r, j: (bi, r, jnp.maximum(j * nq - 1, 0), 0))
    return pl.pallas_call(
        functools.partial(_attn_prompt_kernel, nq=nq),
        grid=(b, dil, nb // nq),
        in_specs=[
            cur, prev, cur, prev, cur,
            pl.BlockSpec(bias.shape, lambda bi, r, j: (0, 0, 0, 0)),
        ],
        out_specs=[
            pl.BlockSpec((None, None, rows, gw), lambda bi, r, j: (bi, r, j, 0)),
            pl.BlockSpec((None, None, rows, LANES), lambda bi, r, j: (bi, r, j, 0)),
        ],
        out_shape=[
            jax.ShapeDtypeStruct((b, dil, s, gw), BF16),
            jax.ShapeDtypeStruct((b, dil, s, LANES), F32),
        ],
        compiler_params=_params(("parallel", "parallel", "arbitrary")),
        name="attn_prompt_g%d" % g,
    )(q, k, k, v, v, bias)


def _head_expand_matrix():
    r = lax.broadcasted_iota(jnp.int32, (LANES, GROUP_HEADS * ATTN_HEAD_DIM), 0)
    c = lax.broadcasted_iota(jnp.int32, (LANES, GROUP_HEADS * ATTN_HEAD_DIM), 1)
    return (r == lax.shift_right_logical(c, 6)).astype(BF16)


def _load_token_order(ref, scr):
    dil, per, width = ref.shape
    if dil == 1:
        return ref[0].astype(F32)
    n_tiles = width // LANES
    for r in range(dil):
        val = ref[r].astype(F32)
        for c in range(n_tiles):
            scr[c, pl.ds(r, per, stride=dil), :] = val[:, c * LANES:(c + 1) * LANES]
    return jnp.concatenate([scr[c] for c in range(n_tiles)], axis=1)


def _merge_out_kernel(o0_ref, o1_ref, o2_ref, l0_ref, l1_ref, l2_ref, z_ref, x_ref, w_ref, y_ref,
                      *scratch):
    o_refs, l_refs = (o0_ref, o1_ref, o2_ref), (l0_ref, l1_ref, l2_ref)
    os_, ls, k = [], [], 0
    for o_ref, l_ref in zip(o_refs, l_refs):
        if o_ref.shape[0] == 1:
            os_.append(o_ref[0].astype(F32))
            ls.append(l_ref[0])
        else:
            os_.append(_load_token_order(o_ref, scratch[k]))
            ls.append(_load_token_order(l_ref, scratch[k + 1]))
            k += 2
    lmax = jnp.maximum(jnp.maximum(ls[0], ls[1]), ls[2])
    es = [jnp.exp(l - lmax) for l in ls]
    tot = es[0] + es[1] + es[2]
    expand = _head_expand_matrix()
    o = jnp.zeros(os_[0].shape, F32)
    for e, og in zip(es, os_):
        hi, lo = _split_bf16(e / tot)
        wexp = (jnp.dot(hi, expand, preferred_element_type=F32)
                + jnp.dot(lo, expand, preferred_element_type=F32))
        o = o + wexp * og
    zf = z_ref[...].astype(F32)
    a = (o * (zf * _sigmoid(zf))).astype(BF16)
    y_ref[...] = x_ref[...] + jnp.dot(a, w_ref[...], preferred_element_type=F32)


def _merge_out(outs, lses, z3, x3, w, *, tm):
    b, t, d = x3.shape
    gw = w.shape[0]
    assert t % tm == 0
    row = lambda width: pl.BlockSpec((None, tm, width), lambda bi, i: (bi, i, 0))
    by_residue = lambda a: pl.BlockSpec((None, a.shape[1], tm // a.shape[1], a.shape[3]),
                                        lambda bi, i: (bi, 0, i, 0))
    scratch = []
    for o in outs:
        if o.shape[1] > 1:
            scratch += [pltpu.VMEM((gw // LANES, tm, LANES), F32), pltpu.VMEM((1, tm, LANES), F32)]
    return pl.pallas_call(
        _merge_out_kernel,
        grid=(b, t // tm),
        in_specs=[by_residue(o) for o in outs] + [by_residue(l) for l in lses]
        + [row(gw), row(d), pl.BlockSpec((gw, d), lambda bi, i: (0, 0))],
        out_specs=row(d),
        out_shape=jax.ShapeDtypeStruct((b, t, d), F32),
        scratch_shapes=scratch,
        compiler_params=_params(("parallel", "parallel")),
        name="merge_out",
    )(*outs, *lses, z3, x3, w)


def _attn_sample_kernel(q_ref, kn_ref, vn_ref, z_ref, c0_ref, c1_ref, c2_ref,
                        b0_ref, b1_ref, b2_ref, a_ref, ks0, vs0, ks1, vs1, ks2, vs2, *, s_new):
    gw = GROUP_HEADS * ATTN_HEAD_DIM
    rows = GROUP_HEADS * s_new
    r = lax.broadcasted_iota(jnp.int32, (rows, gw), 0)
    c = lax.broadcasted_iota(jnp.int32, (rows, gw), 1)
    head_mask = (lax.shift_right_logical(r, int(math.log2(s_new)))
                 == lax.shift_right_logical(c, int(math.log2(ATTN_HEAD_DIM))))
    caches = ((c0_ref, b0_ref, ks0, vs0), (c1_ref, b1_ref, ks1, vs1), (c2_ref, b2_ref, ks2, vs2))
    pad_rows = jnp.zeros((LANES - s_new, gw), F32)
    outs, lses = [], []
    for g, (c_ref, b_ref, ks, vs) in enumerate(caches):
        buf_len = c_ref.shape[2]
        cs = slice(g * gw, (g + 1) * gw)
        ks[:, :buf_len] = c_ref[0].astype(BF16)
        vs[:, :buf_len] = c_ref[1].astype(BF16)
        ks[:, buf_len:] = jnp.concatenate([kn_ref[:, cs].astype(F32), pad_rows], axis=0).T.astype(BF16)
        vs[:, buf_len:] = jnp.concatenate([vn_ref[:, cs].astype(F32), pad_rows], axis=0).T.astype(BF16)
        qg = q_ref[:, cs].astype(F32)
        qbd = jnp.where(head_mask, jnp.concatenate([qg] * GROUP_HEADS, axis=0), 0.0).astype(BF16)
        s = jnp.dot(qbd, ks[...], preferred_element_type=F32) + b_ref[...]
        m = jnp.max(s, axis=1, keepdims=True)
        p = jnp.exp(s - m)
        l = jnp.sum(p, axis=1, keepdims=True)
        outs.append(lax.dot_general(p.astype(BF16), vs[...], NT_DIMS, preferred_element_type=F32) / l)
        lses.append(m + jnp.log(l))
    lmax = jnp.maximum(jnp.maximum(lses[0], lses[1]), lses[2])
    es = [jnp.exp(l - lmax) for l in lses]
    tot = es[0] + es[1] + es[2]
    o = jnp.zeros((rows, gw), F32)
    for e, og in zip(es, outs):
        o = o + (e / tot) * og
    o = jnp.where(head_mask, o, 0.0)
    folded = o[0:s_new, :]
    for h in range(1, GROUP_HEADS):
        folded = folded + o[h * s_new:(h + 1) * s_new, :]
    zf = z_ref[...].astype(F32)
    a_ref[...] = (folded * (zf * _sigmoid(zf))).astype(BF16)


def _attn_sample(q, kn, vn, z, caches, biases):
    b, s_new, qw = q.shape
    gw = GROUP_HEADS * ATTN_HEAD_DIM
    assert s_new % 8 == 0
    cache2 = [jnp.transpose(cb, (0, 2, 3, 4, 1)).reshape(b, 2, gw, cb.shape[1]) for cb in caches]
    in_specs = [
        pl.BlockSpec((None, s_new, qw), lambda bi: (bi, 0, 0)),
        pl.BlockSpec((None, s_new, qw), lambda bi: (bi, 0, 0)),
        pl.BlockSpec((None, s_new, qw), lambda bi: (bi, 0, 0)),
        pl.BlockSpec((None, s_new, gw), lambda bi: (bi, 0, 0)),
    ]
    for cb in cache2:
        in_specs.append(pl.BlockSpec((None, 2, gw, cb.shape[3]), lambda bi: (bi, 0, 0, 0)))
    for bt in biases:
        in_specs.append(pl.BlockSpec(bt.shape, lambda bi: (0, 0)))
    scratch = []
    for cb in cache2:
        scratch += [pltpu.VMEM((gw, cb.shape[3] + LANES), BF16)] * 2
    return pl.pallas_call(
        functools.partial(_attn_sample_kernel, s_new=s_new),
        grid=(b,),
        in_specs=in_specs,
        out_specs=pl.BlockSpec((None, s_new, gw), lambda bi: (bi, 0, 0)),
        out_shape=jax.ShapeDtypeStruct((b, s_new, gw), BF16),
        scratch_shapes=scratch,
        compiler_params=_params(("arbitrary",)),
        name="attn_sample",
    )(q, kn, vn, z, *cache2, *biases)


def _matmul_residual_kernel(a_ref, x_ref, w_ref, y_ref):
    y_ref[...] = x_ref[...] + jnp.dot(a_ref[...], w_ref[...], preferred_element_type=F32)


def _matmul_residual(a2, x2, w, *, tm):
    m, d = x2.shape
    kdim = a2.shape[1]
    assert m % tm == 0
    return pl.pallas_call(
        _matmul_residual_kernel,
        grid=(m // tm,),
        in_specs=[
            pl.BlockSpec((tm, kdim), lambda i: (i, 0)),
            pl.BlockSpec((tm, d), lambda i: (i, 0)),
            pl.BlockSpec((kdim, d), lambda i: (0, 0)),
        ],
        out_specs=pl.BlockSpec((tm, d), lambda i: (i, 0)),
        out_shape=jax.ShapeDtypeStruct((m, d), F32),
        compiler_params=_params(("parallel",)),
        name="matmul_residual",
    )(a2, x2, w)


def _t5_bucket_np(dist):
    exact = N_BUCKETS // 2
    d = np.maximum(dist, 1).astype(np.float32)
    large = exact + (np.log(d / np.float32(exact)) / np.float32(math.log(MAX_DISTANCE / exact))
                     * np.float32(N_BUCKETS - exact)).astype(np.int32)
    return np.where(dist < exact, dist, np.minimum(large, N_BUCKETS - 1)).astype(np.int32)


def _bias_by_step(rel_bias_g, jmax, dil):
    return rel_bias_g.astype(F32)[_t5_bucket_np(np.arange(jmax + 1) * dil)]


def _prompt_bias(rel_bias_g, win, dil):
    jmax = win // dil
    assert jmax == ATTN_BLOCK
    qi = np.arange(ATTN_BLOCK)[:, None]
    kj = np.arange(2 * ATTN_BLOCK)[None, :]
    rel = qi + ATTN_BLOCK - kj
    band = (rel >= 0) & (rel <= jmax)
    bvec = _bias_by_step(rel_bias_g, jmax, dil)
    period = 2 * ATTN_BLOCK + 1
    base = jnp.concatenate([bvec[::-1], jnp.zeros((period - jmax - 1, GROUP_HEADS), F32)], axis=0).T
    bias = jnp.tile(base, (1, ATTN_BLOCK))[:, :2 * ATTN_BLOCK * ATTN_BLOCK]
    bias = bias.reshape(GROUP_HEADS, ATTN_BLOCK, 2 * ATTN_BLOCK)
    rest = jnp.where(band[None], bias, -jnp.inf)
    first = jnp.where((band & (kj >= ATTN_BLOCK))[None], bias, -jnp.inf)
    return jnp.swapaxes(jnp.stack([first, rest], axis=0), 2, 3)


def _sample_bias(rel_bias_g, win, dil, buf_len, s_new):
    jmax = win // dil
    assert buf_len == jmax * dil
    width = buf_len + LANES
    bvec = _bias_by_step(rel_bias_g, jmax, dil)
    gaps = jnp.full((jmax + 1, dil - 1, GROUP_HEADS), -jnp.inf, F32)
    by_dist = jnp.concatenate([bvec[:, None, :], gaps], axis=1).reshape((jmax + 1) * dil, GROUP_HEADS)
    padded = jnp.pad(by_dist[::-1], ((s_new, width), (0, 0)), constant_values=-jnp.inf)
    rows = [padded[dil - 1 - s + s_new:dil - 1 - s + s_new + width] for s in range(s_new)]
    table = jnp.transpose(jnp.stack(rows, axis=0), (2, 0, 1))
    return table.reshape(GROUP_HEADS * s_new, width)


def _layer_a(x3, state, weights, *, chunk, pad_to):
    norm_a, wt, wgt, bg, hgain, wout = weights
    b, t, d = x3.shape
    H = MLSTM_HEADS
    di = wout.shape[0]
    dh = di // H
    k_scale = dh ** -0.5
    k_idx = 1
    if pad_to == t:
        p3, gc3, gr3, kt3 = _inproj(x3, norm_a, wt, None, wgt, bg, tm=min(1024, t), tn=di,
                                    n_heads=H, k_tile=k_idx, k_scale=k_scale)
        col_idx = (0, 1, 2, 3)
        xin = x3
    else:
        m = b * t
        extra = pad_to - t
        colscale = jnp.concatenate([jnp.ones((di,), F32), jnp.full((di,), k_scale, F32),
                                    jnp.ones((3 * di,), F32)])[None, :]
        p, gc, gr = _inproj(x3.reshape(1, m, d), norm_a, wt, colscale, wgt, bg,
                            tm=m, tn=di, n_heads=H)
        col_idx = (0, 2, 3, 4)
        p3 = p.reshape(b, t, -1)
        k3 = p3[:, :, k_idx * di:(k_idx + 1) * di]
        kt3 = jnp.pad(jnp.swapaxes(k3, 1, 2), ((0, 0), (0, 0), (0, extra)))
        p3 = jnp.pad(p3, ((0, 0), (0, extra), (0, 0)))
        xin = jnp.pad(x3, ((0, 0), (0, extra), (0, 0)))
        lane = np.arange(LANES)
        pad_col = np.where(lane < H, -np.inf, 0.0).astype(np.float32)
        gc3 = jnp.concatenate([gc.reshape(b, t, LANES),
                               jnp.broadcast_to(pad_col, (b, extra, LANES))], axis=1)
        pad_row = np.where(np.arange(8) < H, -np.inf, 0.0).astype(np.float32)[:, None, None]
        gr3 = jnp.swapaxes(jnp.concatenate([gr.reshape(8, b, t),
                                            jnp.broadcast_to(pad_row, (8, b, extra))], axis=2), 0, 1)
    xo, c_out, n_out, m_out = _scan(p3, col_idx, kt3, gc3, gr3, xin, wout, hgain, state,
                                    chunk=chunk, n_heads=H)
    return xo[:, :t], c_out, n_out, m_out[:, :H, 0]


def kernel(x_prompt, x_sample, state_mlstm_C, state_mlstm_n, state_mlstm_m, cache_kv_w128, cache_kv_w512, cache_kv_w2048, norm_a, w_in_a, b_gates_a, hnorm_a, w_out_a, norm_kv, w_kv, k_norm, norm_b, w_in_b, q_norm, rel_bias, w_out_b):
    H = MLSTM_HEADS
    bp, tp, d = x_prompt.shape
    bs, ts, _ = x_sample.shape
    di = w_out_a.shape[1]
    dh = di // H
    gw = GROUP_HEADS * ATTN_HEAD_DIM
    qw = len(GROUPS) * gw
    caches = (cache_kv_w128, cache_kv_w512, cache_kv_w2048)
    assert norm_a.shape[0] == 1 and norm_b.shape[0] == 1, "one mLSTM layer, one attention layer"
    for cb, (win, _) in zip(caches, GROUPS):
        assert cb.shape[1] == win, "window buffers must hold a full window"

    w_at = w_in_a[0].T
    wt_a = w_at[:5 * di].astype(BF16)
    wgt = jnp.pad(w_at[5 * di:].astype(F32), ((0, LANES - 2 * H), (0, 0)))
    bg = jnp.pad(b_gates_a[0].astype(F32), (0, LANES - 2 * H))[None, :]
    na = norm_a[0].astype(F32)[None, :]
    hgain = hnorm_a[0].astype(F32)[None, :]
    wout_a = w_out_a[0].astype(BF16)
    weights_a = (na, wt_a, wgt, bg, hgain, wout_a)

    xp1, c_p, n_p, m_p = _layer_a(x_prompt, None, weights_a, chunk=256, pad_to=tp)
    m0 = jnp.pad(jnp.broadcast_to(state_mlstm_m[0].astype(F32)[:, :, None], (bs, H, LANES)),
                 ((0, 0), (0, 8 - H), (0, 0)))
    state_s = (state_mlstm_C[0].astype(F32), state_mlstm_n[0].astype(F32), m0)
    xs1, c_s, n_s, m_s = _layer_a(x_sample, state_s, weights_a, chunk=16, pad_to=16)

    nkv = norm_kv.astype(F32)[None, :]
    wkv = w_kv.astype(BF16)
    kgain = jnp.tile(k_norm.astype(F32), qw // ATTN_HEAD_DIM)[None, :]
    rows_p = [min(win, tp) for win, _ in GROUPS]
    dils = tuple(dil for _, dil in GROUPS)
    *kvp, kv128_p, kv512_p, kv2048_p = _proj_headnorm(
        xp1, nkv, wkv, kgain, n_norm=qw, scale=1.0, tm=512, tail_rows=rows_p,
        dils=dils, rest_by_group=True)
    kp, vp = kvp[:len(GROUPS)], kvp[len(GROUPS):]
    ks, vs, kv128_s, kv512_s, kv2048_s = _proj_headnorm(
        xs1, nkv, wkv, kgain, n_norm=qw, scale=1.0, tm=ts, tail_rows=[ts] * len(GROUPS))

    nb_ = norm_b[0].astype(F32)[None, :]
    wb = w_in_b[0].astype(BF16)
    qgain = jnp.tile(q_norm[0].astype(F32), qw // ATTN_HEAD_DIM)[None, :]
    wout_b = w_out_b[0].astype(BF16)
    qscale = ATTN_HEAD_DIM ** -0.5
    *qp, zp = _proj_headnorm(xp1, nb_, wb, qgain, n_norm=qw, scale=qscale, tm=512, dils=dils)
    qs, zs = _proj_headnorm(xs1, nb_, wb, qgain, n_norm=qw, scale=qscale, tm=ts)

    outs, lses = [], []
    for g, (win, dil) in enumerate(GROUPS):
        bias = _prompt_bias(rel_bias[:, g * GROUP_HEADS:(g + 1) * GROUP_HEADS], win, dil)
        o, lse = _attn_prompt(qp[g], kp[g], vp[g], bias, g)
        outs.append(o)
        lses.append(lse)
    y_p = _merge_out(outs, lses, zp, xp1, wout_b, tm=512)

    sbias = [_sample_bias(rel_bias[:, g * GROUP_HEADS:(g + 1) * GROUP_HEADS], win, dil,
                          caches[g].shape[1], ts) for g, (win, dil) in enumerate(GROUPS)]
    a_s = _attn_sample(qs, ks, vs, zs, caches, sbias)
    y_s = _matmul_residual(a_s.reshape(bs * ts, gw), xs1.reshape(bs * ts, d), wout_b,
                           tm=bs * ts).reshape(bs, ts, d)

    kv5 = lambda a: a.reshape(a.shape[0], a.shape[1], 2, GROUP_HEADS, ATTN_HEAD_DIM)
    return (y_p, y_s, c_p[None], n_p[None], m_p[None], c_s[None], n_s[None], m_s[None],
            kv5(kv128_p), kv5(kv512_p), kv5(kv2048_p), kv5(kv128_s), kv5(kv512_s), kv5(kv2048_s))
```

```python
import functools
import math

import numpy as np
import jax
import jax.numpy as jnp
from jax import lax
from jax.experimental import pallas as pl
from jax.experimental.pallas import tpu as pltpu

F32 = jnp.float32
BF16 = jnp.bfloat16

EPS = 1e-6
MLSTM_HEADS = 4
GROUPS = ((128, 1), (512, 4), (2048, 16))
GROUP_HEADS = 8
ATTN_HEAD_DIM = 64
ATTN_BLOCK = 128
N_BUCKETS = 32
MAX_DISTANCE = 2048

LANES = 128
MXU_DIM = 256
VMEM_LIMIT_BYTES = 56 * 1024 * 1024

NT_DIMS = (((1,), (1,)), ((), ()))


def _params(sem):
    return pltpu.CompilerParams(dimension_semantics=sem, vmem_limit_bytes=VMEM_LIMIT_BYTES)


def _rms_scale(xf):
    return lax.rsqrt(jnp.mean(xf * xf, axis=-1, keepdims=True) + EPS)


def _sigmoid(x):
    return 1.0 / (1.0 + jnp.exp(-x))


def _split_bf16(a):
    hi = a.astype(BF16)
    lo = (a - hi.astype(F32)).astype(BF16)
    return hi, lo


def _inproj_kernel(*refs, n_heads, k_tile, k_scale, use_colscale):
    x_ref, g_ref, wt_ref, cs_ref, wgt_ref, bg_ref, p_ref, gc_ref, gr_ref = refs[:9]
    kt_ref = refs[9] if k_tile is not None else None
    xn_ref = refs[-1]
    j = pl.program_id(2)

    @pl.when(j == 0)
    def _():
        xf = x_ref[...]
        xn = xf * _rms_scale(xf) * g_ref[...]
        xh, xl = _split_bf16(xn)
        wh, wl = _split_bf16(wgt_ref[...])
        xn_ref[...] = xh
        gates = (lax.dot_general(xh, wh, NT_DIMS, preferred_element_type=F32)
                 + lax.dot_general(xl, wh, NT_DIMS, preferred_element_type=F32)
                 + lax.dot_general(xh, wl, NT_DIMS, preferred_element_type=F32)) + bg_ref[...]
        lane = lax.broadcasted_iota(jnp.int32, gates.shape, 1)
        logsig = jnp.minimum(gates, 0.0) - jnp.log(1.0 + jnp.exp(-jnp.abs(gates)))
        gcol = jnp.where(lane < n_heads, gates, jnp.where(lane < 2 * n_heads, logsig, 0.0))
        gc_ref[...] = gcol
        gr_ref[...] = gcol.T[:8, :]

    def token_major():
        acc = lax.dot_general(xn_ref[...], wt_ref[...], NT_DIMS, preferred_element_type=F32)
        if use_colscale:
            acc = acc * cs_ref[...]
        p_ref[...] = acc.astype(BF16)

    if k_tile is None:
        token_major()
    else:
        pl.when(j != k_tile)(token_major)

        @pl.when(j == k_tile)
        def _():
            acc = lax.dot_general(wt_ref[...], xn_ref[...], NT_DIMS, preferred_element_type=F32)
            kt_ref[...] = (acc * k_scale).astype(BF16)


def _inproj(x3, g, wt, colscale, wgt, bg, *, tm, tn, n_heads, k_tile=None, k_scale=1.0):
    b, t, d = x3.shape
    n = wt.shape[0]
    assert t % tm == 0 and n % tn == 0 and 2 * n_heads <= 8
    nj = n // tn
    if k_tile is None:
        p_tile = lambda j: j
        n_out = n
    else:
        assert 1 <= k_tile < nj
        p_tile = lambda j: jnp.where(j < k_tile, j, j - 1)
        n_out = n - tn
    out_specs = [
        pl.BlockSpec((None, tm, tn), lambda bi, i, j: (bi, i, p_tile(j))),
        pl.BlockSpec((None, tm, LANES), lambda bi, i, j: (bi, i, 0)),
        pl.BlockSpec((None, 8, tm), lambda bi, i, j: (bi, 0, i)),
    ]
    out_shape = [
        jax.ShapeDtypeStruct((b, t, n_out), BF16),
        jax.ShapeDtypeStruct((b, t, LANES), F32),
        jax.ShapeDtypeStruct((b, 8, t), F32),
    ]
    if k_tile is not None:
        out_specs.append(pl.BlockSpec((None, tn, tm), lambda bi, i, j: (bi, 0, i)))
        out_shape.append(jax.ShapeDtypeStruct((b, tn, t), BF16))
    use_colscale = colscale is not None
    if colscale is None:
        colscale = jnp.ones((1, n), F32)
    return pl.pallas_call(
        functools.partial(_inproj_kernel, n_heads=n_heads, k_tile=k_tile, k_scale=k_scale,
                          use_colscale=use_colscale),
        grid=(b, t // tm, nj),
        in_specs=[
            pl.BlockSpec((None, tm, d), lambda bi, i, j: (bi, i, 0)),
            pl.BlockSpec((1, d), lambda bi, i, j: (0, 0)),
            pl.BlockSpec((tn, d), lambda bi, i, j: (j, 0)),
            pl.BlockSpec((1, tn), lambda bi, i, j: (0, j)),
            pl.BlockSpec((LANES, d), lambda bi, i, j: (0, 0)),
            pl.BlockSpec((1, LANES), lambda bi, i, j: (0, 0)),
        ],
        out_specs=out_specs,
        out_shape=out_shape,
        scratch_shapes=[pltpu.VMEM((tm, d), BF16)],
        compiler_params=_params(("parallel", "parallel", "arbitrary")),
        name="inproj",
    )(x3, g, wt, colscale, wgt, bg)


def _scan_kernel(*refs, chunk, n_heads, dh, n_chunks, has_state):
    L, H = chunk, n_heads
    dext = dh + LANES
    (q_ref, kt_ref, v_ref, o_ref, z_ref, gc_ref, gr_ref, x_ref, wout_ref, hg_ref) = refs[:10]
    pos = 10
    if has_state:
        c0_ref, n0_ref, m0_ref = refs[pos:pos + 3]
        pos += 3
    xo_ref, cout_ref, nout_ref, mout_ref = refs[pos:pos + 4]
    cext_ref, cb_ref, m_ref = refs[pos + 4:pos + 7]
    c = pl.program_id(1)

    @pl.when(c == 0)
    def _():
        if has_state:
            lane0 = lax.broadcasted_iota(jnp.int32, (dh, LANES), 1) == 0
            for h in range(H):
                ncol = jnp.broadcast_to(n0_ref[h:h + 1, :], (LANES, dh)).T
                cext_ref[h, :, :dh] = c0_ref[h]
                cext_ref[h, :, dh:] = jnp.where(lane0, ncol, 0.0)
            m_ref[...] = m0_ref[...]
        else:
            cext_ref[...] = jnp.zeros(cext_ref.shape, F32)
            m_ref[...] = jnp.zeros(m_ref.shape, F32)
        cb_ref[...] = cext_ref[...].astype(BF16)

    gc = gc_ref[...]
    gr = gr_ref[...]
    row = lax.broadcasted_iota(jnp.int32, (L, L), 0)
    col = lax.broadcasted_iota(jnp.int32, (L, L), 1)
    causal = row >= col
    lane = lax.broadcasted_iota(jnp.int32, gc.shape, 1)
    subl = lax.broadcasted_iota(jnp.int32, gr.shape, 0)
    lf_c_hi, lf_c_lo = _split_bf16(jnp.where((lane >= H) & (lane < 2 * H), gc, 0.0))
    lf_r_hi, lf_r_lo = _split_bf16(jnp.where((subl >= H) & (subl < 2 * H), gr, 0.0))
    lower = causal.astype(BF16)
    upper = (row <= col).astype(BF16)
    bc_all = (jnp.dot(lower, lf_c_hi, preferred_element_type=F32)
              + jnp.dot(lower, lf_c_lo, preferred_element_type=F32))
    br_all = (jnp.dot(lf_r_hi, upper, preferred_element_type=F32)
              + jnp.dot(lf_r_lo, upper, preferred_element_type=F32))
    ones_col = (lax.broadcasted_iota(jnp.int32, (L, LANES), 1) == 0).astype(BF16)
    y = jnp.zeros((L, wout_ref.shape[1]), F32)

    for h in range(H):
        sl = slice(h * dh, (h + 1) * dh)
        q = q_ref[:, sl]
        kt = kt_ref[sl, :]
        vext = jnp.concatenate([v_ref[:, sl], ones_col], axis=1)
        ig_r = gr[h:h + 1, :]
        b_c = bc_all[:, H + h:H + h + 1]
        b_r = br_all[H + h:H + h + 1, :]
        m_prev = m_ref[h:h + 1, 0:1]

        log_d = jnp.where(causal, b_c - b_r + ig_r, -jnp.inf)
        log_inter = b_c + m_prev
        m_t = jnp.maximum(log_inter, jnp.max(log_d, axis=1, keepdims=True))
        dmat = jnp.exp(log_d - m_t)
        inter = jnp.exp(log_inter - m_t)
        s = jnp.dot(q, kt, preferred_element_type=F32) * dmat
        numden = (jnp.dot(s.astype(BF16), vext, preferred_element_type=F32)
                  + inter * jnp.dot(q, cb_ref[h], preferred_element_type=F32))
        num = numden[:, :dh]
        den = numden[:, dh:dh + 1]
        hh = num / jnp.maximum(jnp.abs(den), jnp.exp(-m_t))

        b_last = b_r[:, L - 1:L]
        a_r = b_last - b_r + ig_r
        m_new = jnp.maximum(b_last + m_prev, jnp.max(a_r, axis=1, keepdims=True))
        w_r = jnp.exp(a_r - m_new)
        decay = jnp.exp(b_last + m_prev - m_new)
        ktw = (kt.astype(F32) * w_r).astype(BF16)
        c_new = decay * cext_ref[h] + jnp.dot(ktw, vext, preferred_element_type=F32)
        cext_ref[h] = c_new
        cb_ref[h] = c_new.astype(BF16)
        m_ref[h:h + 1, :] = jnp.broadcast_to(m_new, (1, LANES))

        hn = hh * lax.rsqrt(jnp.mean(hh * hh, axis=1, keepdims=True) + EPS)
        zf = z_ref[:, sl].astype(F32)
        gate = _sigmoid(o_ref[:, sl].astype(F32)) * (zf * _sigmoid(zf))
        hg = (hn * hg_ref[:, sl] * gate).astype(BF16)
        y = y + jnp.dot(hg, wout_ref[sl, :], preferred_element_type=F32)

    xo_ref[...] = x_ref[...] + y

    @pl.when(c == n_chunks - 1)
    def _():
        for h in range(H):
            cout_ref[h] = cext_ref[h, :, :dh]
            nout_ref[h:h + 1, :] = cext_ref[h, :, dh:].T[0:1, :]
        mout_ref[...] = m_ref[...]


def _scan(p3, col_idx, kt3, gc, gr, x3, wout, hgain, state, *, chunk, n_heads):
    b, t, _ = p3.shape
    dh = kt3.shape[1] // n_heads
    di = n_heads * dh
    d = x3.shape[2]
    nc = t // chunk
    assert t % chunk == 0
    has_state = state is not None
    qi, vi, oi, zi = col_idx

    def pspec(ci):
        return pl.BlockSpec((None, chunk, di), lambda bi, c, ci=ci: (bi, c, ci))

    in_specs = [
        pspec(qi),
        pl.BlockSpec((None, di, chunk), lambda bi, c: (bi, 0, c)),
        pspec(vi), pspec(oi), pspec(zi),
        pl.BlockSpec((None, chunk, LANES), lambda bi, c: (bi, c, 0)),
        pl.BlockSpec((None, 8, chunk), lambda bi, c: (bi, 0, c)),
        pl.BlockSpec((None, chunk, d), lambda bi, c: (bi, c, 0)),
        pl.BlockSpec((di, d), lambda bi, c: (0, 0)),
        pl.BlockSpec((1, di), lambda bi, c: (0, 0)),
    ]
    args = [p3, kt3, p3, p3, p3, gc, gr, x3, wout, hgain]
    if has_state:
        c0, n0, m0 = state
        in_specs += [
            pl.BlockSpec((None, n_heads, dh, dh), lambda bi, c: (bi, 0, 0, 0)),
            pl.BlockSpec((None, n_heads, dh), lambda bi, c: (bi, 0, 0)),
            pl.BlockSpec((None, 8, LANES), lambda bi, c: (bi, 0, 0)),
        ]
        args += [c0, n0, m0]
    return pl.pallas_call(
        functools.partial(_scan_kernel, chunk=chunk, n_heads=n_heads, dh=dh,
                          n_chunks=nc, has_state=has_state),
        grid=(b, nc),
        in_specs=in_specs,
        out_specs=[
            pl.BlockSpec((None, chunk, d), lambda bi, c: (bi, c, 0)),
            pl.BlockSpec((None, n_heads, dh, dh), lambda bi, c: (bi, 0, 0, 0)),
            pl.BlockSpec((None, n_heads, dh), lambda bi, c: (bi, 0, 0)),
            pl.BlockSpec((None, 8, LANES), lambda bi, c: (bi, 0, 0)),
        ],
        out_shape=[
            jax.ShapeDtypeStruct((b, t, d), F32),
            jax.ShapeDtypeStruct((b, n_heads, dh, dh), F32),
            jax.ShapeDtypeStruct((b, n_heads, dh), F32),
            jax.ShapeDtypeStruct((b, 8, LANES), F32),
        ],
        scratch_shapes=[
            pltpu.VMEM((n_heads, dh, dh + LANES), F32),
            pltpu.VMEM((n_heads, dh, dh + LANES), BF16),
            pltpu.VMEM((8, LANES), F32),
        ],
        compiler_params=_params(("parallel", "arbitrary")),
        name="mlstm_scan",
    )(*args)


def _headnorm(a):
    n = a.shape[1]
    r = lax.broadcasted_iota(jnp.int32, (MXU_DIM, MXU_DIM), 0)
    c = lax.broadcasted_iota(jnp.int32, (MXU_DIM, MXU_DIM), 1)
    same_head = (lax.shift_right_logical(r, 6) == lax.shift_right_logical(c, 6)).astype(BF16)
    parts = []
    for c0 in range(0, n, MXU_DIM):
        blk = a[:, c0:c0 + MXU_DIM]
        ss = jnp.dot((blk * blk).astype(BF16), same_head, preferred_element_type=F32)
        parts.append(blk * lax.rsqrt(ss * (1.0 / ATTN_HEAD_DIM) + EPS))
    return jnp.concatenate(parts, axis=1)


def _store_by_residue(ref, val, scr, dil):
    rows, width = val.shape
    if dil == 1:
        ref[0] = val.astype(BF16)
        return
    for c in range(width // LANES):
        scr[c] = val[:, c * LANES:(c + 1) * LANES]
    for r in range(dil):
        parts = [scr[c, pl.ds(r, rows // dil, stride=dil), :] for c in range(width // LANES)]
        ref[r] = jnp.concatenate(parts, axis=1).astype(BF16)


def _proj_headnorm_kernel(*refs, n_norm, scale, tails, tm, dils, rest_by_group):
    x_ref, g_ref, w_ref, hg_ref = refs[:4]
    gw = GROUP_HEADS * ATTN_HEAD_DIM
    n_groups = n_norm // gw
    n = w_ref.shape[1]
    pos = 4
    if dils is None:
        a_refs, r_refs = refs[pos:pos + 1], refs[pos + 1:pos + 2]
        pos += 2
    else:
        a_refs = refs[pos:pos + n_groups]
        pos += n_groups
        n_rest = n_groups if rest_by_group else 1
        r_refs = refs[pos:pos + n_rest]
        pos += n_rest
    tail_refs = refs[pos:pos + len(tails)]
    scr = refs[pos + len(tails)] if dils is not None else None

    xf = x_ref[...]
    xn = (xf * _rms_scale(xf) * g_ref[...]).astype(BF16)

    def project(c0, width):
        return jnp.dot(xn, w_ref[:, c0:c0 + width], preferred_element_type=F32)

    for g in range(n_groups):
        cs = slice(g * gw, (g + 1) * gw)
        a = _headnorm(project(g * gw, gw)) * hg_ref[:, cs]
        if scale != 1.0:
            a = a * scale
        r = project(n_norm + g * gw, gw) if (rest_by_group or dils is None) and n - n_norm == n_norm \
            else None
        if dils is None:
            a_refs[0][:, cs] = a.astype(BF16)
            if r is not None:
                r_refs[0][:, cs] = r.astype(BF16)
        else:
            _store_by_residue(a_refs[g], a, scr, dils[g])
            if rest_by_group:
                _store_by_residue(r_refs[g], r, scr, dils[g])
        if tails:
            rows, t_ref = tails[g][0], tail_refs[g]
            t_ref[:, :gw] = a[max(tm - rows, 0):, :]
            t_ref[:, gw:] = r[max(tm - rows, 0):, :]
    if n - n_norm != n_norm:
        r_refs[0][...] = project(n_norm, n - n_norm).astype(BF16)


def _proj_headnorm(x3, g, w, hgain, *, n_norm, scale, tm, tail_rows=None, dils=None,
                   rest_by_group=False):
    b, t, d = x3.shape
    n = w.shape[1]
    assert t % tm == 0 and n_norm % MXU_DIM == 0
    n_tiles = t // tm
    gw = GROUP_HEADS * ATTN_HEAD_DIM
    out_specs, out_shape, scratch = [], [], []

    def natural(width):
        out_specs.append(pl.BlockSpec((None, tm, width), lambda bi, i: (bi, i, 0)))
        out_shape.append(jax.ShapeDtypeStruct((b, t, width), BF16))

    def by_residue(dil):
        assert tm % (dil * 16) == 0
        out_specs.append(pl.BlockSpec((None, dil, tm // dil, gw), lambda bi, i: (bi, 0, i, 0)))
        out_shape.append(jax.ShapeDtypeStruct((b, dil, t // dil, gw), BF16))

    if dils is None:
        natural(n_norm)
        natural(n - n_norm)
    else:
        assert len(dils) * gw == n_norm
        for dil in dils:
            by_residue(dil)
        if rest_by_group:
            assert n - n_norm == n_norm
            for dil in dils:
                by_residue(dil)
        else:
            natural(n - n_norm)
        scratch.append(pltpu.VMEM((gw // LANES, tm, LANES), F32))
    tails = []
    for rows in (tail_rows or ()):
        if rows >= tm:
            assert rows % tm == 0
            first = n_tiles - rows // tm
            blk = tm
        else:
            first = n_tiles - 1
            blk = rows
        tails.append((rows, first))
        out_specs.append(pl.BlockSpec(
            (None, blk, 2 * gw), lambda bi, i, first=first: (bi, jnp.maximum(i - first, 0), 0)))
        out_shape.append(jax.ShapeDtypeStruct((b, rows, 2 * gw), F32))
    return pl.pallas_call(
        functools.partial(_proj_headnorm_kernel, n_norm=n_norm, scale=scale, tails=tuple(tails),
                          tm=tm, dils=dils, rest_by_group=rest_by_group),
        grid=(b, n_tiles),
        in_specs=[
            pl.BlockSpec((None, tm, d), lambda bi, i: (bi, i, 0)),
            pl.BlockSpec((1, d), lambda bi, i: (0, 0)),
            pl.BlockSpec((d, n), lambda bi, i: (0, 0)),
            pl.BlockSpec((1, n_norm), lambda bi, i: (0, 0)),
        ],
        out_specs=out_specs,
        out_shape=out_shape,
        scratch_shapes=scratch,
        compiler_params=_params(("parallel", "arbitrary")),
        name="proj_headnorm",
    )(x3, g, w, hgain)


def _attn_prompt_kernel(q_ref, kp_ref, kc_ref, vp_ref, vc_ref, bias_ref, o_ref, lse_ref, *, nq):
    blk = ATTN_BLOCK
    first = jnp.where(pl.program_id(2) == 0, 0, 1)
    low_half = lax.broadcasted_iota(jnp.int32, (blk, LANES), 1) < ATTN_HEAD_DIM
    for jb in range(nq):
        q = q_ref[jb * blk:(jb + 1) * blk, :]
        if jb == 0:
            kcat = jnp.concatenate([kp_ref[...], kc_ref[0:blk, :]], axis=0)
            vcat = jnp.concatenate([vp_ref[...], vc_ref[0:blk, :]], axis=0)
        else:
            kcat = kc_ref[(jb - 1) * blk:(jb + 1) * blk, :]
            vcat = vc_ref[(jb - 1) * blk:(jb + 1) * blk, :]
        lses, outs = [], []
        for j in range(GROUP_HEADS // 2):
            ps = slice(j * LANES, (j + 1) * LANES)
            qf = q[:, ps].astype(F32)
            kpair, vpair = kcat[:, ps], vcat[:, ps]
            pair = []
            for half in range(2):
                h = 2 * j + half
                qh = jnp.where(low_half if half == 0 else ~low_half, qf, 0.0).astype(BF16)
                bias = bias_ref[first, h] if jb == 0 else bias_ref[1, h]
                st = lax.dot_general(kpair, qh, NT_DIMS, preferred_element_type=F32) + bias
                m = jnp.max(st, axis=0, keepdims=True)
                p = jnp.exp(st - m)
                l = jnp.sum(p, axis=0, keepdims=True)
                pn = (p * (1.0 / l)).T.astype(BF16)
                pair.append(jnp.dot(pn, vpair, preferred_element_type=F32))
                lses.append(m + jnp.log(l))
            outs.append(jnp.where(low_half, pair[0], pair[1]))
        o_ref[jb * blk:(jb + 1) * blk, :] = jnp.concatenate(outs, axis=1).astype(o_ref.dtype)
        lse_ref[jb * blk:(jb + 1) * blk, :] = jnp.concatenate(
            lses + [jnp.zeros((LANES - GROUP_HEADS, blk), F32)], axis=0).T


def _attn_prompt(q, k, v, bias, g):
    b, dil, s, gw = q.shape
    nb = s // ATTN_BLOCK
    assert s % ATTN_BLOCK == 0
    nq = min(8, nb)
    assert nb % nq == 0
    rows = nq * ATTN_BLOCK
    cur = pl.BlockSpec((None, None, rows, gw), lambda bi, r, j: (bi, r, j, 0))
    prev = pl.BlockSpec((None, None, ATTN_BLOCK, gw),
                        lambda bi, r, j: (bi, r, jnp.maximum(j * nq - 1, 0), 0))
    return pl.pallas_call(
        functools.partial(_attn_prompt_kernel, nq=nq),
        grid=(b, dil, nb // nq),
        in_specs=[
            cur, prev, cur, prev, cur,
            pl.BlockSpec(bias.shape, lambda bi, r, j: (0, 0, 0, 0)),
        ],
        out_specs=[
            pl.BlockSpec((None, None, rows, gw), lambda bi, r, j: (bi, r, j, 0)),
            pl.BlockSpec((None, None, rows, LANES), lambda bi, r, j: (bi, r, j, 0)),
        ],
        out_shape=[
            jax.ShapeDtypeStruct((b, dil, s, gw), BF16),
            jax.ShapeDtypeStruct((b, dil, s, LANES), F32),
        ],
        compiler_params=_params(("parallel", "parallel", "arbitrary")),
        name="attn_prompt_g%d" % g,
    )(q, k, k, v, v, bias)


def _head_expand_matrix():
    r = lax.broadcasted_iota(jnp.int32, (LANES, GROUP_HEADS * ATTN_HEAD_DIM), 0)
    c = lax.broadcasted_iota(jnp.int32, (LANES, GROUP_HEADS * ATTN_HEAD_DIM), 1)
    return (r == lax.shift_right_logical(c, 6)).astype(BF16)


def _load_token_order(ref, scr):
    dil, per, width = ref.shape
    if dil == 1:
        return ref[0].astype(F32)
    n_tiles = width // LANES
    for r in range(dil):
        val = ref[r].astype(F32)
        for c in range(n_tiles):
            scr[c, pl.ds(r, per, stride=dil), :] = val[:, c * LANES:(c + 1) * LANES]
    return jnp.concatenate([scr[c] for c in range(n_tiles)], axis=1)


def _merge_out_kernel(o0_ref, o1_ref, o2_ref, l0_ref, l1_ref, l2_ref, z_ref, x_ref, w_ref, y_ref,
                      *scratch):
    o_refs, l_refs = (o0_ref, o1_ref, o2_ref), (l0_ref, l1_ref, l2_ref)
    os_, ls, k = [], [], 0
    for o_ref, l_ref in zip(o_refs, l_refs):
        if o_ref.shape[0] == 1:
            os_.append(o_ref[0].astype(F32))
            ls.append(l_ref[0])
        else:
            os_.append(_load_token_order(o_ref, scratch[k]))
            ls.append(_load_token_order(l_ref, scratch[k + 1]))
            k += 2
    lmax = jnp.maximum(jnp.maximum(ls[0], ls[1]), ls[2])
    es = [jnp.exp(l - lmax) for l in ls]
    tot = es[0] + es[1] + es[2]
    expand = _head_expand_matrix()
    o = jnp.zeros(os_[0].shape, F32)
    for e, og in zip(es, os_):
        hi, lo = _split_bf16(e / tot)
        wexp = (jnp.dot(hi, expand, preferred_element_type=F32)
                + jnp.dot(lo, expand, preferred_element_type=F32))
        o = o + wexp * og
    zf = z_ref[...].astype(F32)
    a = (o * (zf * _sigmoid(zf))).astype(BF16)
    y_ref[...] = x_ref[...] + jnp.dot(a, w_ref[...], preferred_element_type=F32)


def _merge_out(outs, lses, z3, x3, w, *, tm):
    b, t, d = x3.shape
    gw = w.shape[0]
    assert t % tm == 0
    row = lambda width: pl.BlockSpec((None, tm, width), lambda bi, i: (bi, i, 0))
    by_residue = lambda a: pl.BlockSpec((None, a.shape[1], tm // a.shape[1], a.shape[3]),
                                        lambda bi, i: (bi, 0, i, 0))
    scratch = []
    for o in outs:
        if o.shape[1] > 1:
            scratch += [pltpu.VMEM((gw // LANES, tm, LANES), F32), pltpu.VMEM((1, tm, LANES), F32)]
    return pl.pallas_call(
        _merge_out_kernel,
        grid=(b, t // tm),
        in_specs=[by_residue(o) for o in outs] + [by_residue(l) for l in lses]
        + [row(gw), row(d), pl.BlockSpec((gw, d), lambda bi, i: (0, 0))],
        out_specs=row(d),
        out_shape=jax.ShapeDtypeStruct((b, t, d), F32),
        scratch_shapes=scratch,
        compiler_params=_params(("parallel", "parallel")),
        name="merge_out",
    )(*outs, *lses, z3, x3, w)


def _attn_sample_kernel(q_ref, kn_ref, vn_ref, z_ref, c0_ref, c1_ref, c2_ref,
                        b0_ref, b1_ref, b2_ref, a_ref, ks0, vs0, ks1, vs1, ks2, vs2, *, s_new):
    gw = GROUP_HEADS * ATTN_HEAD_DIM
    rows = GROUP_HEADS * s_new
    r = lax.broadcasted_iota(jnp.int32, (rows, gw), 0)
    c = lax.broadcasted_iota(jnp.int32, (rows, gw), 1)
    head_mask = (lax.shift_right_logical(r, int(math.log2(s_new)))
                 == lax.shift_right_logical(c, int(math.log2(ATTN_HEAD_DIM))))
    caches = ((c0_ref, b0_ref, ks0, vs0), (c1_ref, b1_ref, ks1, vs1), (c2_ref, b2_ref, ks2, vs2))
    pad_rows = jnp.zeros((LANES - s_new, gw), F32)
    outs, lses = [], []
    for g, (c_ref, b_ref, ks, vs) in enumerate(caches):
        buf_len = c_ref.shape[2]
        cs = slice(g * gw, (g + 1) * gw)
        ks[:, :buf_len] = c_ref[0].astype(BF16)
        vs[:, :buf_len] = c_ref[1].astype(BF16)
        ks[:, buf_len:] = jnp.concatenate([kn_ref[:, cs].astype(F32), pad_rows], axis=0).T.astype(BF16)
        vs[:, buf_len:] = jnp.concatenate([vn_ref[:, cs].astype(F32), pad_rows], axis=0).T.astype(BF16)
        qg = q_ref[:, cs].astype(F32)
        qbd = jnp.where(head_mask, jnp.concatenate([qg] * GROUP_HEADS, axis=0), 0.0).astype(BF16)
        s = jnp.dot(qbd, ks[...], preferred_element_type=F32) + b_ref[...]
        m = jnp.max(s, axis=1, keepdims=True)
        p = jnp.exp(s - m)
        l = jnp.sum(p, axis=1, keepdims=True)
        outs.append(lax.dot_general(p.astype(BF16), vs[...], NT_DIMS, preferred_element_type=F32) / l)
        lses.append(m + jnp.log(l))
    lmax = jnp.maximum(jnp.maximum(lses[0], lses[1]), lses[2])
    es = [jnp.exp(l - lmax) for l in lses]
    tot = es[0] + es[1] + es[2]
    o = jnp.zeros((rows, gw), F32)
    for e, og in zip(es, outs):
        o = o + (e / tot) * og
    o = jnp.where(head_mask, o, 0.0)
    folded = o[0:s_new, :]
    for h in range(1, GROUP_HEADS):
        folded = folded + o[h * s_new:(h + 1) * s_new, :]
    zf = z_ref[...].astype(F32)
    a_ref[...] = (folded * (zf * _sigmoid(zf))).astype(BF16)


def _attn_sample(q, kn, vn, z, caches, biases):
    b, s_new, qw = q.shape
    gw = GROUP_HEADS * ATTN_HEAD_DIM
    assert s_new % 8 == 0
    cache2 = [jnp.transpose(cb, (0, 2, 3, 4, 1)).reshape(b, 2, gw, cb.shape[1]) for cb in caches]
    in_specs = [
        pl.BlockSpec((None, s_new, qw), lambda bi: (bi, 0, 0)),
        pl.BlockSpec((None, s_new, qw), lambda bi: (bi, 0, 0)),
        pl.BlockSpec((None, s_new, qw), lambda bi: (bi, 0, 0)),
        pl.BlockSpec((None, s_new, gw), lambda bi: (bi, 0, 0)),
    ]
    for cb in cache2:
        in_specs.append(pl.BlockSpec((None, 2, gw, cb.shape[3]), lambda bi: (bi, 0, 0, 0)))
    for bt in biases:
        in_specs.append(pl.BlockSpec(bt.shape, lambda bi: (0, 0)))
    scratch = []
    for cb in cache2:
        scratch += [pltpu.VMEM((gw, cb.shape[3] + LANES), BF16)] * 2
    return pl.pallas_call(
        functools.partial(_attn_sample_kernel, s_new=s_new),
        grid=(b,),
        in_specs=in_specs,
        out_specs=pl.BlockSpec((None, s_new, gw), lambda bi: (bi, 0, 0)),
        out_shape=jax.ShapeDtypeStruct((b, s_new, gw), BF16),
        scratch_shapes=scratch,
        compiler_params=_params(("arbitrary",)),
        name="attn_sample",
    )(q, kn, vn, z, *cache2, *biases)


def _matmul_residual_kernel(a_ref, x_ref, w_ref, y_ref):
    y_ref[...] = x_ref[...] + jnp.dot(a_ref[...], w_ref[...], preferred_element_type=F32)


def _matmul_residual(a2, x2, w, *, tm):
    m, d = x2.shape
    kdim = a2.shape[1]
    assert m % tm == 0
    return pl.pallas_call(
        _matmul_residual_kernel,
        grid=(m // tm,),
        in_specs=[
            pl.BlockSpec((tm, kdim), lambda i: (i, 0)),
            pl.BlockSpec((tm, d), lambda i: (i, 0)),
            pl.BlockSpec((kdim, d), lambda i: (0, 0)),
        ],
        out_specs=pl.BlockSpec((tm, d), lambda i: (i, 0)),
        out_shape=jax.ShapeDtypeStruct((m, d), F32),
        compiler_params=_params(("parallel",)),
        name="matmul_residual",
    )(a2, x2, w)


def _t5_bucket_np(dist):
    exact = N_BUCKETS // 2
    d = np.maximum(dist, 1).astype(np.float32)
    large = exact + (np.log(d / np.float32(exact)) / np.float32(math.log(MAX_DISTANCE / exact))
                     * np.float32(N_BUCKETS - exact)).astype(np.int32)
    return np.where(dist < exact, dist, np.minimum(large, N_BUCKETS - 1)).astype(np.int32)


def _bias_by_step(rel_bias_g, jmax, dil):
    return rel_bias_g.astype(F32)[_t5_bucket_np(np.arange(jmax + 1) * dil)]


def _prompt_bias(rel_bias_g, win, dil):
    jmax = win // dil
    assert jmax == ATTN_BLOCK
    qi = np.arange(ATTN_BLOCK)[:, None]
    kj = np.arange(2 * ATTN_BLOCK)[None, :]
    rel = qi + ATTN_BLOCK - kj
    band = (rel >= 0) & (rel <= jmax)
    bvec = _bias_by_step(rel_bias_g, jmax, dil)
    period = 2 * ATTN_BLOCK + 1
    base = jnp.concatenate([bvec[::-1], jnp.zeros((period - jmax - 1, GROUP_HEADS), F32)], axis=0).T
    bias = jnp.tile(base, (1, ATTN_BLOCK))[:, :2 * ATTN_BLOCK * ATTN_BLOCK]
    bias = bias.reshape(GROUP_HEADS, ATTN_BLOCK, 2 * ATTN_BLOCK)
    rest = jnp.where(band[None], bias, -jnp.inf)
    first = jnp.where((band & (kj >= ATTN_BLOCK))[None], bias, -jnp.inf)
    return jnp.swapaxes(jnp.stack([first, rest], axis=0), 2, 3)


def _sample_bias(rel_bias_g, win, dil, buf_len, s_new):
    jmax = win // dil
    assert buf_len == jmax * dil
    width = buf_len + LANES
    bvec = _bias_by_step(rel_bias_g, jmax, dil)
    gaps = jnp.full((jmax + 1, dil - 1, GROUP_HEADS), -jnp.inf, F32)
    by_dist = jnp.concatenate([bvec[:, None, :], gaps], axis=1).reshape((jmax + 1) * dil, GROUP_HEADS)
    padded = jnp.pad(by_dist[::-1], ((s_new, width), (0, 0)), constant_values=-jnp.inf)
    rows = [padded[dil - 1 - s + s_new:dil - 1 - s + s_new + width] for s in range(s_new)]
    table = jnp.transpose(jnp.stack(rows, axis=0), (2, 0, 1))
    return table.reshape(GROUP_HEADS * s_new, width)


def _layer_a(x3, state, weights, *, chunk, pad_to):
    norm_a, wt, wgt, bg, hgain, wout = weights
    b, t, d = x3.shape
    H = MLSTM_HEADS
    di = wout.shape[0]
    dh = di // H
    k_scale = dh ** -0.5
    k_idx = 1
    if pad_to == t:
        p3, gc3, gr3, kt3 = _inproj(x3, norm_a, wt, None, wgt, bg, tm=min(1024, t), tn=di,
                                    n_heads=H, k_tile=k_idx, k_scale=k_scale)
        col_idx = (0, 1, 2, 3)
        xin = x3
    else:
        m = b * t
        extra = pad_to - t
        colscale = jnp.concatenate([jnp.ones((di,), F32), jnp.full((di,), k_scale, F32),
                                    jnp.ones((3 * di,), F32)])[None, :]
        p, gc, gr = _inproj(x3.reshape(1, m, d), norm_a, wt, colscale, wgt, bg,
                            tm=m, tn=di, n_heads=H)
        col_idx = (0, 2, 3, 4)
        p3 = p.reshape(b, t, -1)
        k3 = p3[:, :, k_idx * di:(k_idx + 1) * di]
        kt3 = jnp.pad(jnp.swapaxes(k3, 1, 2), ((0, 0), (0, 0), (0, extra)))
        p3 = jnp.pad(p3, ((0, 0), (0, extra), (0, 0)))
        xin = jnp.pad(x3, ((0, 0), (0, extra), (0, 0)))
        lane = np.arange(LANES)
        pad_col = np.where(lane < H, -np.inf, 0.0).astype(np.float32)
        gc3 = jnp.concatenate([gc.reshape(b, t, LANES),
                               jnp.broadcast_to(pad_col, (b, extra, LANES))], axis=1)
        pad_row = np.where(np.arange(8) < H, -np.inf, 0.0).astype(np.float32)[:, None, None]
        gr3 = jnp.swapaxes(jnp.concatenate([gr.reshape(8, b, t),
                                            jnp.broadcast_to(pad_row, (8, b, extra))], axis=2), 0, 1)
    xo, c_out, n_out, m_out = _scan(p3, col_idx, kt3, gc3, gr3, xin, wout, hgain, state,
                                    chunk=chunk, n_heads=H)
    return xo[:, :t], c_out, n_out, m_out[:, :H, 0]


def kernel(x_prompt, x_sample, state_mlstm_C, state_mlstm_n, state_mlstm_m, cache_kv_w128, cache_kv_w512, cache_kv_w2048, norm_a, w_in_a, b_gates_a, hnorm_a, w_out_a, norm_kv, w_kv, k_norm, norm_b, w_in_b, q_norm, rel_bias, w_out_b):
    H = MLSTM_HEADS
    bp, tp, d = x_prompt.shape
    bs, ts, _ = x_sample.shape
    di = w_out_a.shape[1]
    dh = di // H
    gw = GROUP_HEADS * ATTN_HEAD_DIM
    qw = len(GROUPS) * gw
    caches = (cache_kv_w128, cache_kv_w512, cache_kv_w2048)
    assert norm_a.shape[0] == 1 and norm_b.shape[0] == 1, "one mLSTM layer, one attention layer"
    for cb, (win, _) in zip(caches, GROUPS):
        assert cb.shape[1] == win, "window buffers must hold a full window"

    w_at = w_in_a[0].T
    wt_a = w_at[:5 * di].astype(BF16)
    wgt = jnp.pad(w_at[5 * di:].astype(F32), ((0, LANES - 2 * H), (0, 0)))
    bg = jnp.pad(b_gates_a[0].astype(F32), (0, LANES - 2 * H))[None, :]
    na = norm_a[0].astype(F32)[None, :]
    hgain = hnorm_a[0].astype(F32)[None, :]
    wout_a = w_out_a[0].astype(BF16)
    weights_a = (na, wt_a, wgt, bg, hgain, wout_a)

    xp1, c_p, n_p, m_p = _layer_a(x_prompt, None, weights_a, chunk=256, pad_to=tp)
    m0 = jnp.pad(jnp.broadcast_to(state_mlstm_m[0].astype(F32)[:, :, None], (bs, H, LANES)),
                 ((0, 0), (0, 8 - H), (0, 0)))
    state_s = (state_mlstm_C[0].astype(F32), state_mlstm_n[0].astype(F32), m0)
    xs1, c_s, n_s, m_s = _layer_a(x_sample, state_s, weights_a, chunk=16, pad_to=16)

    nkv = norm_kv.astype(F32)[None, :]
    wkv = w_kv.astype(BF16)
    kgain = jnp.tile(k_norm.astype(F32), qw // ATTN_HEAD_DIM)[None, :]
    rows_p = [min(win, tp) for win, _ in GROUPS]
    dils = tuple(dil for _, dil in GROUPS)
    *kvp, kv128_p, kv512_p, kv2048_p = _proj_headnorm(
        xp1, nkv, wkv, kgain, n_norm=qw, scale=1.0, tm=512, tail_rows=rows_p,
        dils=dils, rest_by_group=True)
    kp, vp = kvp[:len(GROUPS)], kvp[len(GROUPS):]
    xs1_flat = xs1.reshape(1, bs * ts, d)
    by_seq = lambda a: a.reshape(bs, ts, a.shape[-1])
    ks, vs, kv128_s, kv512_s, kv2048_s = map(by_seq, _proj_headnorm(
        xs1_flat, nkv, wkv, kgain, n_norm=qw, scale=1.0, tm=bs * ts,
        tail_rows=[bs * ts] * len(GROUPS)))

    nb_ = norm_b[0].astype(F32)[None, :]
    wb = w_in_b[0].astype(BF16)
    qgain = jnp.tile(q_norm[0].astype(F32), qw // ATTN_HEAD_DIM)[None, :]
    wout_b = w_out_b[0].astype(BF16)
    qscale = ATTN_HEAD_DIM ** -0.5
    *qp, zp = _proj_headnorm(xp1, nb_, wb, qgain, n_norm=qw, scale=qscale, tm=512, dils=dils)
    qs, zs = map(by_seq, _proj_headnorm(xs1_flat, nb_, wb, qgain, n_norm=qw, scale=qscale,
                                        tm=bs * ts))

    outs, lses = [], []
    for g, (win, dil) in enumerate(GROUPS):
        bias = _prompt_bias(rel_bias[:, g * GROUP_HEADS:(g + 1) * GROUP_HEADS], win, dil)
        o, lse = _attn_prompt(qp[g], kp[g], vp[g], bias, g)
        outs.append(o)
        lses.append(lse)
    y_p = _merge_out(outs, lses, zp, xp1, wout_b, tm=1024)

    sbias = [_sample_bias(rel_bias[:, g * GROUP_HEADS:(g + 1) * GROUP_HEADS], win, dil,
                          caches[g].shape[1], ts) for g, (win, dil) in enumerate(GROUPS)]
    a_s = _attn_sample(qs, ks, vs, zs, caches, sbias)
    y_s = _matmul_residual(a_s.reshape(bs * ts, gw), xs1.reshape(bs * ts, d), wout_b,
                           tm=bs * ts).reshape(bs, ts, d)

    kv5 = lambda a: a.reshape(a.shape[0], a.shape[1], 2, GROUP_HEADS, ATTN_HEAD_DIM)
    return (y_p, y_s, c_p[None], n_p[None], m_p[None], c_s[None], n_s[None], m_s[None],
            kv5(kv128_p), kv5(kv512_p), kv5(kv2048_p), kv5(kv128_s), kv5(kv512_s), kv5(kv2048_s))
```

```python
import functools
import math

import numpy as np
import jax
import jax.numpy as jnp
from jax import lax
from jax.experimental import pallas as pl
from jax.experimental.pallas import tpu as pltpu

F32 = jnp.float32
BF16 = jnp.bfloat16

EPS = 1e-6
MLSTM_HEADS = 4
GROUPS = ((128, 1), (512, 4), (2048, 16))
GROUP_HEADS = 8
ATTN_HEAD_DIM = 64
ATTN_BLOCK = 128
N_BUCKETS = 32
MAX_DISTANCE = 2048

LANES = 128
MXU_DIM = 256
VMEM_LIMIT_BYTES = 56 * 1024 * 1024

NT_DIMS = (((1,), (1,)), ((), ()))


def _params(sem):
    return pltpu.CompilerParams(dimension_semantics=sem, vmem_limit_bytes=VMEM_LIMIT_BYTES)


def _rms_scale(xf):
    return lax.rsqrt(jnp.mean(xf * xf, axis=-1, keepdims=True) + EPS)


def _sigmoid(x):
    return 1.0 / (1.0 + jnp.exp(-x))


def _split_bf16(a):
    hi = a.astype(BF16)
    lo = (a - hi.astype(F32)).astype(BF16)
    return hi, lo


def _inproj_kernel(*refs, n_heads, k_tile, k_scale, use_colscale):
    x_ref, g_ref, wt_ref, cs_ref, wgt_ref, bg_ref, p_ref, gc_ref, gr_ref = refs[:9]
    kt_ref = refs[9] if k_tile is not None else None
    xn_ref = refs[-1]
    j = pl.program_id(2)

    @pl.when(j == 0)
    def _():
        xf = x_ref[...]
        xn = xf * _rms_scale(xf) * g_ref[...]
        xh, xl = _split_bf16(xn)
        wh, wl = _split_bf16(wgt_ref[...])
        xn_ref[...] = xh
        gates = (lax.dot_general(xh, wh, NT_DIMS, preferred_element_type=F32)
                 + lax.dot_general(xl, wh, NT_DIMS, preferred_element_type=F32)
                 + lax.dot_general(xh, wl, NT_DIMS, preferred_element_type=F32)) + bg_ref[...]
        lane = lax.broadcasted_iota(jnp.int32, gates.shape, 1)
        logsig = jnp.minimum(gates, 0.0) - jnp.log(1.0 + jnp.exp(-jnp.abs(gates)))
        gcol = jnp.where(lane < n_heads, gates, jnp.where(lane < 2 * n_heads, logsig, 0.0))
        gc_ref[...] = gcol
        gr_ref[...] = gcol.T[:8, :]

    def token_major():
        acc = lax.dot_general(xn_ref[...], wt_ref[...], NT_DIMS, preferred_element_type=F32)
        if use_colscale:
            acc = acc * cs_ref[...]
        p_ref[...] = acc.astype(BF16)

    if k_tile is None:
        token_major()
    else:
        pl.when(j != k_tile)(token_major)

        @pl.when(j == k_tile)
        def _():
            acc = lax.dot_general(wt_ref[...], xn_ref[...], NT_DIMS, preferred_element_type=F32)
            kt_ref[...] = (acc * k_scale).astype(BF16)


def _inproj(x3, g, wt, colscale, wgt, bg, *, tm, tn, n_heads, k_tile=None, k_scale=1.0):
    b, t, d = x3.shape
    n = wt.shape[0]
    assert t % tm == 0 and n % tn == 0 and 2 * n_heads <= 8
    nj = n // tn
    if k_tile is None:
        p_tile = lambda j: j
        n_out = n
    else:
        assert 1 <= k_tile < nj
        p_tile = lambda j: jnp.where(j < k_tile, j, j - 1)
        n_out = n - tn
    out_specs = [
        pl.BlockSpec((None, tm, tn), lambda bi, i, j: (bi, i, p_tile(j))),
        pl.BlockSpec((None, tm, LANES), lambda bi, i, j: (bi, i, 0)),
        pl.BlockSpec((None, 8, tm), lambda bi, i, j: (bi, 0, i)),
    ]
    out_shape = [
        jax.ShapeDtypeStruct((b, t, n_out), BF16),
        jax.ShapeDtypeStruct((b, t, LANES), F32),
        jax.ShapeDtypeStruct((b, 8, t), F32),
    ]
    if k_tile is not None:
        out_specs.append(pl.BlockSpec((None, tn, tm), lambda bi, i, j: (bi, 0, i)))
        out_shape.append(jax.ShapeDtypeStruct((b, tn, t), BF16))
    use_colscale = colscale is not None
    if colscale is None:
        colscale = jnp.ones((1, n), F32)
    return pl.pallas_call(
        functools.partial(_inproj_kernel, n_heads=n_heads, k_tile=k_tile, k_scale=k_scale,
                          use_colscale=use_colscale),
        grid=(b, t // tm, nj),
        in_specs=[
            pl.BlockSpec((None, tm, d), lambda bi, i, j: (bi, i, 0)),
            pl.BlockSpec((1, d), lambda bi, i, j: (0, 0)),
            pl.BlockSpec((tn, d), lambda bi, i, j: (j, 0)),
            pl.BlockSpec((1, tn), lambda bi, i, j: (0, j)),
            pl.BlockSpec((LANES, d), lambda bi, i, j: (0, 0)),
            pl.BlockSpec((1, LANES), lambda bi, i, j: (0, 0)),
        ],
        out_specs=out_specs,
        out_shape=out_shape,
        scratch_shapes=[pltpu.VMEM((tm, d), BF16)],
        compiler_params=_params(("parallel", "parallel", "arbitrary")),
        name="inproj",
    )(x3, g, wt, colscale, wgt, bg)


def _scan_kernel(*refs, chunk, n_heads, dh, n_chunks, has_state):
    L, H = chunk, n_heads
    dext = dh + LANES
    (q_ref, kt_ref, v_ref, o_ref, z_ref, gc_ref, gr_ref, x_ref, wout_ref, hg_ref) = refs[:10]
    pos = 10
    if has_state:
        c0_ref, n0_ref, m0_ref = refs[pos:pos + 3]
        pos += 3
    xo_ref, cout_ref, nout_ref, mout_ref = refs[pos:pos + 4]
    cext_ref, cb_ref, m_ref = refs[pos + 4:pos + 7]
    c = pl.program_id(1)

    @pl.when(c == 0)
    def _():
        if has_state:
            lane0 = lax.broadcasted_iota(jnp.int32, (dh, LANES), 1) == 0
            for h in range(H):
                ncol = jnp.broadcast_to(n0_ref[h:h + 1, :], (LANES, dh)).T
                cext_ref[h, :, :dh] = c0_ref[h]
                cext_ref[h, :, dh:] = jnp.where(lane0, ncol, 0.0)
            m_ref[...] = m0_ref[...]
        else:
            cext_ref[...] = jnp.zeros(cext_ref.shape, F32)
            m_ref[...] = jnp.zeros(m_ref.shape, F32)
        cb_ref[...] = cext_ref[...].astype(BF16)

    gc = gc_ref[...]
    gr = gr_ref[...]
    row = lax.broadcasted_iota(jnp.int32, (L, L), 0)
    col = lax.broadcasted_iota(jnp.int32, (L, L), 1)
    causal = row >= col
    lane = lax.broadcasted_iota(jnp.int32, gc.shape, 1)
    subl = lax.broadcasted_iota(jnp.int32, gr.shape, 0)
    lf_c_hi, lf_c_lo = _split_bf16(jnp.where((lane >= H) & (lane < 2 * H), gc, 0.0))
    lf_r_hi, lf_r_lo = _split_bf16(jnp.where((subl >= H) & (subl < 2 * H), gr, 0.0))
    lower = causal.astype(BF16)
    upper = (row <= col).astype(BF16)
    bc_all = (jnp.dot(lower, lf_c_hi, preferred_element_type=F32)
              + jnp.dot(lower, lf_c_lo, preferred_element_type=F32))
    br_all = (jnp.dot(lf_r_hi, upper, preferred_element_type=F32)
              + jnp.dot(lf_r_lo, upper, preferred_element_type=F32))
    ones_col = (lax.broadcasted_iota(jnp.int32, (L, LANES), 1) == 0).astype(BF16)
    y = jnp.zeros((L, wout_ref.shape[1]), F32)

    for h in range(H):
        sl = slice(h * dh, (h + 1) * dh)
        q = q_ref[:, sl]
        kt = kt_ref[sl, :]
        vext = jnp.concatenate([v_ref[:, sl], ones_col], axis=1)
        ig_r = gr[h:h + 1, :]
        b_c = bc_all[:, H + h:H + h + 1]
        b_r = br_all[H + h:H + h + 1, :]
        m_prev = m_ref[h:h + 1, 0:1]

        log_d = jnp.where(causal, b_c - b_r + ig_r, -jnp.inf)
        log_inter = b_c + m_prev
        m_t = jnp.maximum(log_inter, jnp.max(log_d, axis=1, keepdims=True))
        dmat = jnp.exp(log_d - m_t)
        inter = jnp.exp(log_inter - m_t)
        s = jnp.dot(q, kt, preferred_element_type=F32) * dmat
        numden = (jnp.dot(s.astype(BF16), vext, preferred_element_type=F32)
                  + inter * jnp.dot(q, cb_ref[h], preferred_element_type=F32))
        num = numden[:, :dh]
        den = numden[:, dh:dh + 1]
        hh = num / jnp.maximum(jnp.abs(den), jnp.exp(-m_t))

        b_last = b_r[:, L - 1:L]
        a_r = b_last - b_r + ig_r
        m_new = jnp.maximum(b_last + m_prev, jnp.max(a_r, axis=1, keepdims=True))
        w_r = jnp.exp(a_r - m_new)
        decay = jnp.exp(b_last + m_prev - m_new)
        ktw = (kt.astype(F32) * w_r).astype(BF16)
        c_new = decay * cext_ref[h] + jnp.dot(ktw, vext, preferred_element_type=F32)
        cext_ref[h] = c_new
        cb_ref[h] = c_new.astype(BF16)
        m_ref[h:h + 1, :] = jnp.broadcast_to(m_new, (1, LANES))

        hn = hh * lax.rsqrt(jnp.mean(hh * hh, axis=1, keepdims=True) + EPS)
        zf = z_ref[:, sl].astype(F32)
        gate = _sigmoid(o_ref[:, sl].astype(F32)) * (zf * _sigmoid(zf))
        hg = (hn * hg_ref[:, sl] * gate).astype(BF16)
        y = y + jnp.dot(hg, wout_ref[sl, :], preferred_element_type=F32)

    xo_ref[...] = x_ref[...] + y

    @pl.when(c == n_chunks - 1)
    def _():
        for h in range(H):
            cout_ref[h] = cext_ref[h, :, :dh]
            nout_ref[h:h + 1, :] = cext_ref[h, :, dh:].T[0:1, :]
        mout_ref[...] = m_ref[...]


def _scan(p3, col_idx, kt3, gc, gr, x3, wout, hgain, state, *, chunk, n_heads):
    b, t, _ = p3.shape
    dh = kt3.shape[1] // n_heads
    di = n_heads * dh
    d = x3.shape[2]
    nc = t // chunk
    assert t % chunk == 0
    has_state = state is not None
    qi, vi, oi, zi = col_idx

    def pspec(ci):
        return pl.BlockSpec((None, chunk, di), lambda bi, c, ci=ci: (bi, c, ci))

    in_specs = [
        pspec(qi),
        pl.BlockSpec((None, di, chunk), lambda bi, c: (bi, 0, c)),
        pspec(vi), pspec(oi), pspec(zi),
        pl.BlockSpec((None, chunk, LANES), lambda bi, c: (bi, c, 0)),
        pl.BlockSpec((None, 8, chunk), lambda bi, c: (bi, 0, c)),
        pl.BlockSpec((None, chunk, d), lambda bi, c: (bi, c, 0)),
        pl.BlockSpec((di, d), lambda bi, c: (0, 0)),
        pl.BlockSpec((1, di), lambda bi, c: (0, 0)),
    ]
    args = [p3, kt3, p3, p3, p3, gc, gr, x3, wout, hgain]
    if has_state:
        c0, n0, m0 = state
        in_specs += [
            pl.BlockSpec((None, n_heads, dh, dh), lambda bi, c: (bi, 0, 0, 0)),
            pl.BlockSpec((None, n_heads, dh), lambda bi, c: (bi, 0, 0)),
            pl.BlockSpec((None, 8, LANES), lambda bi, c: (bi, 0, 0)),
        ]
        args += [c0, n0, m0]
    return pl.pallas_call(
        functools.partial(_scan_kernel, chunk=chunk, n_heads=n_heads, dh=dh,
                          n_chunks=nc, has_state=has_state),
        grid=(b, nc),
        in_specs=in_specs,
        out_specs=[
            pl.BlockSpec((None, chunk, d), lambda bi, c: (bi, c, 0)),
            pl.BlockSpec((None, n_heads, dh, dh), lambda bi, c: (bi, 0, 0, 0)),
            pl.BlockSpec((None, n_heads, dh), lambda bi, c: (bi, 0, 0)),
            pl.BlockSpec((None, 8, LANES), lambda bi, c: (bi, 0, 0)),
        ],
        out_shape=[
            jax.ShapeDtypeStruct((b, t, d), F32),
            jax.ShapeDtypeStruct((b, n_heads, dh, dh), F32),
            jax.ShapeDtypeStruct((b, n_heads, dh), F32),
            jax.ShapeDtypeStruct((b, 8, LANES), F32),
        ],
        scratch_shapes=[
            pltpu.VMEM((n_heads, dh, dh + LANES), F32),
            pltpu.VMEM((n_heads, dh, dh + LANES), BF16),
            pltpu.VMEM((8, LANES), F32),
        ],
        compiler_params=_params(("parallel", "arbitrary")),
        name="mlstm_scan",
    )(*args)


def _headnorm(a):
    n = a.shape[1]
    r = lax.broadcasted_iota(jnp.int32, (MXU_DIM, MXU_DIM), 0)
    c = lax.broadcasted_iota(jnp.int32, (MXU_DIM, MXU_DIM), 1)
    same_head = (lax.shift_right_logical(r, 6) == lax.shift_right_logical(c, 6)).astype(BF16)
    parts = []
    for c0 in range(0, n, MXU_DIM):
        blk = a[:, c0:c0 + MXU_DIM]
        ss = jnp.dot((blk * blk).astype(BF16), same_head, preferred_element_type=F32)
        parts.append(blk * lax.rsqrt(ss * (1.0 / ATTN_HEAD_DIM) + EPS))
    return jnp.concatenate(parts, axis=1)


def _store_by_residue(ref, val, scr, dil):
    rows, width = val.shape
    if dil == 1:
        ref[0] = val.astype(BF16)
        return
    for c in range(width // LANES):
        scr[c] = val[:, c * LANES:(c + 1) * LANES]
    for r in range(dil):
        parts = [scr[c, pl.ds(r, rows // dil, stride=dil), :] for c in range(width // LANES)]
        ref[r] = jnp.concatenate(parts, axis=1).astype(BF16)


def _proj_headnorm_kernel(x_ref, g_ref, w_ref, hg_ref, a_ref, r_ref, *row_refs, n_norm, scale):
    gw = GROUP_HEADS * ATTN_HEAD_DIM
    xf = x_ref[...]
    xn = (xf * _rms_scale(xf) * g_ref[...]).astype(BF16)
    p = jnp.dot(xn, w_ref[...], preferred_element_type=F32)
    a = _headnorm(p[:, :n_norm]) * hg_ref[...]
    if scale != 1.0:
        a = a * scale
    r = p[:, n_norm:]
    a_ref[...] = a.astype(BF16)
    r_ref[...] = r.astype(BF16)
    for g, row_ref in enumerate(row_refs):
        cs = slice(g * gw, (g + 1) * gw)
        row_ref[:, :gw] = a[:, cs]
        row_ref[:, gw:] = r[:, cs]


def _proj_headnorm(x2, g, w, hgain, *, n_norm, scale, group_rows=False):
    m, d = x2.shape
    n = w.shape[1]
    gw = GROUP_HEADS * ATTN_HEAD_DIM
    assert n_norm % MXU_DIM == 0 and m % 16 == 0
    whole = lambda shape: pl.BlockSpec(shape, lambda i: (0, 0))
    out_specs = [whole((m, n_norm)), whole((m, n - n_norm))]
    out_shape = [jax.ShapeDtypeStruct((m, n_norm), BF16), jax.ShapeDtypeStruct((m, n - n_norm), BF16)]
    if group_rows:
        assert n - n_norm == n_norm
        for _ in range(n_norm // gw):
            out_specs.append(whole((m, 2 * gw)))
            out_shape.append(jax.ShapeDtypeStruct((m, 2 * gw), F32))
    return pl.pallas_call(
        functools.partial(_proj_headnorm_kernel, n_norm=n_norm, scale=scale),
        grid=(1,),
        in_specs=[whole((m, d)), whole((1, d)), whole((d, n)), whole((1, n_norm))],
        out_specs=out_specs,
        out_shape=out_shape,
        compiler_params=_params(("arbitrary",)),
        name="proj_headnorm",
    )(x2, g, w, hgain)


def _kvq_proj_kernel(x_ref, gkv_ref, wkv_ref, kg_ref, gb_ref, wb_ref, qg_ref, *refs,
                     qscale, tails, tm, dils):
    ng = len(dils)
    gw = GROUP_HEADS * ATTN_HEAD_DIM
    n_norm = ng * gw
    k_refs, v_refs, q_refs = refs[:ng], refs[ng:2 * ng], refs[2 * ng:3 * ng]
    z_ref = refs[3 * ng]
    tail_refs = refs[3 * ng + 1:3 * ng + 1 + len(tails)]
    scr = refs[-1]
    xf = x_ref[...]
    xhat = xf * _rms_scale(xf)
    xkv = (xhat * gkv_ref[...]).astype(BF16)
    xb = (xhat * gb_ref[...]).astype(BF16)
    for g, dil in enumerate(dils):
        cs = slice(g * gw, (g + 1) * gw)
        k = _headnorm(jnp.dot(xkv, wkv_ref[:, cs], preferred_element_type=F32)) * kg_ref[:, cs]
        v = jnp.dot(xkv, wkv_ref[:, n_norm + g * gw:n_norm + (g + 1) * gw],
                    preferred_element_type=F32)
        q = (_headnorm(jnp.dot(xb, wb_ref[:, cs], preferred_element_type=F32))
             * qg_ref[:, cs] * qscale)
        _store_by_residue(k_refs[g], k, scr, dil)
        _store_by_residue(v_refs[g], v, scr, dil)
        _store_by_residue(q_refs[g], q, scr, dil)
        rows, t_ref = tails[g][0], tail_refs[g]
        t_ref[:, :gw] = k[max(tm - rows, 0):, :]
        t_ref[:, gw:] = v[max(tm - rows, 0):, :]
    z_ref[...] = jnp.dot(xb, wb_ref[:, n_norm:], preferred_element_type=F32).astype(BF16)


def _kvq_proj(x3, gkv, wkv, kgain, gb, wb, qgain, *, qscale, tm, tail_rows, dils):
    b, t, d = x3.shape
    gw = GROUP_HEADS * ATTN_HEAD_DIM
    ng = len(dils)
    assert t % tm == 0 and wkv.shape[1] == 2 * ng * gw and wb.shape[1] > ng * gw
    n_tiles = t // tm
    out_specs, out_shape = [], []
    for _ in range(3):
        for dil in dils:
            assert tm % (dil * 16) == 0
            out_specs.append(pl.BlockSpec((None, dil, tm // dil, gw), lambda bi, i: (bi, 0, i, 0)))
            out_shape.append(jax.ShapeDtypeStruct((b, dil, t // dil, gw), BF16))
    zw = wb.shape[1] - ng * gw
    out_specs.append(pl.BlockSpec((None, tm, zw), lambda bi, i: (bi, i, 0)))
    out_shape.append(jax.ShapeDtypeStruct((b, t, zw), BF16))
    tails = []
    for rows in tail_rows:
        if rows >= tm:
            assert rows % tm == 0
            first, blk = n_tiles - rows // tm, tm
        else:
            first, blk = n_tiles - 1, rows
        tails.append((rows, first))
        out_specs.append(pl.BlockSpec(
            (None, blk, 2 * gw), lambda bi, i, first=first: (bi, jnp.maximum(i - first, 0), 0)))
        out_shape.append(jax.ShapeDtypeStruct((b, rows, 2 * gw), F32))
    const = lambda shape: pl.BlockSpec(shape, lambda bi, i: (0, 0))
    outs = pl.pallas_call(
        functools.partial(_kvq_proj_kernel, qscale=qscale, tails=tuple(tails), tm=tm, dils=dils),
        grid=(b, n_tiles),
        in_specs=[
            pl.BlockSpec((None, tm, d), lambda bi, i: (bi, i, 0)),
            const((1, d)), const(wkv.shape), const((1, ng * gw)),
            const((1, d)), const(wb.shape), const((1, ng * gw)),
        ],
        out_specs=out_specs,
        out_shape=out_shape,
        scratch_shapes=[pltpu.VMEM((gw // LANES, tm, LANES), F32)],
        compiler_params=_params(("parallel", "arbitrary")),
        name="kvq_proj",
    )(x3, gkv, wkv, kgain, gb, wb, qgain)
    return outs[:ng], outs[ng:2 * ng], outs[2 * ng:3 * ng], outs[3 * ng], outs[3 * ng + 1:]


def _attn_prompt_kernel(q_ref, kp_ref, kc_ref, vp_ref, vc_ref, bias_ref, o_ref, lse_ref, *, nq):
    blk = ATTN_BLOCK
    first = jnp.where(pl.program_id(2) == 0, 0, 1)
    low_half = lax.broadcasted_iota(jnp.int32, (blk, LANES), 1) < ATTN_HEAD_DIM
    for rr, jb in [(rr, jb) for rr in range(q_ref.shape[0]) for jb in range(nq)]:
        q = q_ref[rr, jb * blk:(jb + 1) * blk, :]
        if jb == 0:
            kcat = jnp.concatenate([kp_ref[rr], kc_ref[rr, 0:blk, :]], axis=0)
            vcat = jnp.concatenate([vp_ref[rr], vc_ref[rr, 0:blk, :]], axis=0)
        else:
            kcat = kc_ref[rr, (jb - 1) * blk:(jb + 1) * blk, :]
            vcat = vc_ref[rr, (jb - 1) * blk:(jb + 1) * blk, :]
        lses, outs = [], []
        for j in range(GROUP_HEADS // 2):
            ps = slice(j * LANES, (j + 1) * LANES)
            qf = q[:, ps].astype(F32)
            kpair, vpair = kcat[:, ps], vcat[:, ps]
            pair = []
            for half in range(2):
                h = 2 * j + half
                qh = jnp.where(low_half if half == 0 else ~low_half, qf, 0.0).astype(BF16)
                bias = bias_ref[first, h] if jb == 0 else bias_ref[1, h]
                st = lax.dot_general(kpair, qh, NT_DIMS, preferred_element_type=F32) + bias
                m = jnp.max(st, axis=0, keepdims=True)
                p = jnp.exp(st - m)
                l = jnp.sum(p, axis=0, keepdims=True)
                pn = (p * (1.0 / l)).T.astype(BF16)
                pair.append(jnp.dot(pn, vpair, preferred_element_type=F32))
                lses.append(m + jnp.log(l))
            outs.append(jnp.where(low_half, pair[0], pair[1]))
        o_ref[rr, jb * blk:(jb + 1) * blk, :] = jnp.concatenate(outs, axis=1).astype(o_ref.dtype)
        lse_ref[rr, jb * blk:(jb + 1) * blk, :] = jnp.concatenate(
            lses + [jnp.zeros((LANES - GROUP_HEADS, blk), F32)], axis=0).T


def _attn_prompt(q, k, v, bias, g):
    b, dil, s, gw = q.shape
    nb = s // ATTN_BLOCK
    assert s % ATTN_BLOCK == 0
    blocks_per_step = 8
    nq = min(blocks_per_step, nb)
    rps = min(blocks_per_step // nq, dil)
    assert nb % nq == 0 and dil % rps == 0
    rows = nq * ATTN_BLOCK
    cur = pl.BlockSpec((None, rps, rows, gw), lambda bi, r, j: (bi, r, j, 0))
    prev = pl.BlockSpec((None, rps, ATTN_BLOCK, gw),
                        lambda bi, r, j: (bi, r, jnp.maximum(j * nq - 1, 0), 0))
    return pl.pallas_call(
        functools.partial(_attn_prompt_kernel, nq=nq),
        grid=(b, dil // rps, nb // nq),
        in_specs=[
            cur, prev, cur, prev, cur,
            pl.BlockSpec(bias.shape, lambda bi, r, j: (0, 0, 0, 0)),
        ],
        out_specs=[
            pl.BlockSpec((None, rps, rows, gw), lambda bi, r, j: (bi, r, j, 0)),
            pl.BlockSpec((None, rps, rows, LANES), lambda bi, r, j: (bi, r, j, 0)),
        ],
        out_shape=[
            jax.ShapeDtypeStruct((b, dil, s, gw), BF16),
            jax.ShapeDtypeStruct((b, dil, s, LANES), F32),
        ],
        compiler_params=_params(("parallel", "parallel", "arbitrary")),
        name="attn_prompt_g%d" % g,
    )(q, k, k, v, v, bias)


def _head_expand_matrix():
    r = lax.broadcasted_iota(jnp.int32, (LANES, GROUP_HEADS * ATTN_HEAD_DIM), 0)
    c = lax.broadcasted_iota(jnp.int32, (LANES, GROUP_HEADS * ATTN_HEAD_DIM), 1)
    return (r == lax.shift_right_logical(c, 6)).astype(BF16)


def _load_token_order(ref, scr):
    dil, per, width = ref.shape
    if dil == 1:
        return ref[0].astype(F32)
    n_tiles = width // LANES
    for r in range(dil):
        val = ref[r].astype(F32)
        for c in range(n_tiles):
            scr[c, pl.ds(r, per, stride=dil), :] = val[:, c * LANES:(c + 1) * LANES]
    return jnp.concatenate([scr[c] for c in range(n_tiles)], axis=1)


def _merge_out_kernel(o0_ref, o1_ref, o2_ref, l0_ref, l1_ref, l2_ref, z_ref, x_ref, w_ref, y_ref,
                      *scratch):
    o_refs, l_refs = (o0_ref, o1_ref, o2_ref), (l0_ref, l1_ref, l2_ref)
    os_, ls, k = [], [], 0
    for o_ref, l_ref in zip(o_refs, l_refs):
        if o_ref.shape[0] == 1:
            os_.append(o_ref[0].astype(F32))
            ls.append(l_ref[0])
        else:
            os_.append(_load_token_order(o_ref, scratch[k]))
            ls.append(_load_token_order(l_ref, scratch[k + 1]))
            k += 2
    lmax = jnp.maximum(jnp.maximum(ls[0], ls[1]), ls[2])
    es = [jnp.exp(l - lmax) for l in ls]
    tot = es[0] + es[1] + es[2]
    expand = _head_expand_matrix()
    o = jnp.zeros(os_[0].shape, F32)
    for e, og in zip(es, os_):
        hi, lo = _split_bf16(e / tot)
        wexp = (jnp.dot(hi, expand, preferred_element_type=F32)
                + jnp.dot(lo, expand, preferred_element_type=F32))
        o = o + wexp * og
    zf = z_ref[...].astype(F32)
    a = (o * (zf * _sigmoid(zf))).astype(BF16)
    y_ref[...] = x_ref[...] + jnp.dot(a, w_ref[...], preferred_element_type=F32)


def _merge_out(outs, lses, z3, x3, w, *, tm):
    b, t, d = x3.shape
    gw = w.shape[0]
    assert t % tm == 0
    row = lambda width: pl.BlockSpec((None, tm, width), lambda bi, i: (bi, i, 0))
    by_residue = lambda a: pl.BlockSpec((None, a.shape[1], tm // a.shape[1], a.shape[3]),
                                        lambda bi, i: (bi, 0, i, 0))
    scratch = []
    for o in outs:
        if o.shape[1] > 1:
            scratch += [pltpu.VMEM((gw // LANES, tm, LANES), F32), pltpu.VMEM((1, tm, LANES), F32)]
    return pl.pallas_call(
        _merge_out_kernel,
        grid=(b, t // tm),
        in_specs=[by_residue(o) for o in outs] + [by_residue(l) for l in lses]
        + [row(gw), row(d), pl.BlockSpec((gw, d), lambda bi, i: (0, 0))],
        out_specs=row(d),
        out_shape=jax.ShapeDtypeStruct((b, t, d), F32),
        scratch_shapes=scratch,
        compiler_params=_params(("parallel", "parallel")),
        name="merge_out",
    )(*outs, *lses, z3, x3, w)


def _attn_sample_kernel(q_ref, kn_ref, vn_ref, z_ref, c0_ref, c1_ref, c2_ref,
                        b0_ref, b1_ref, b2_ref, a_ref, ks0, vs0, ks1, vs1, ks2, vs2, *, s_new):
    gw = GROUP_HEADS * ATTN_HEAD_DIM
    rows = GROUP_HEADS * s_new
    r = lax.broadcasted_iota(jnp.int32, (rows, gw), 0)
    c = lax.broadcasted_iota(jnp.int32, (rows, gw), 1)
    head_mask = (lax.shift_right_logical(r, int(math.log2(s_new)))
                 == lax.shift_right_logical(c, int(math.log2(ATTN_HEAD_DIM))))
    caches = ((c0_ref, b0_ref, ks0, vs0), (c1_ref, b1_ref, ks1, vs1), (c2_ref, b2_ref, ks2, vs2))
    pad_rows = jnp.zeros((LANES - s_new, gw), F32)
    outs, lses = [], []
    for g, (c_ref, b_ref, ks, vs) in enumerate(caches):
        buf_len = c_ref.shape[2]
        cs = slice(g * gw, (g + 1) * gw)
        ks[:, :buf_len] = c_ref[0].astype(BF16)
        vs[:, :buf_len] = c_ref[1].astype(BF16)
        ks[:, buf_len:] = jnp.concatenate([kn_ref[:, cs].astype(F32), pad_rows], axis=0).T.astype(BF16)
        vs[:, buf_len:] = jnp.concatenate([vn_ref[:, cs].astype(F32), pad_rows], axis=0).T.astype(BF16)
        qg = q_ref[:, cs].astype(F32)
        qbd = jnp.where(head_mask, jnp.concatenate([qg] * GROUP_HEADS, axis=0), 0.0).astype(BF16)
        s = jnp.dot(qbd, ks[...], preferred_element_type=F32) + b_ref[...]
        m = jnp.max(s, axis=1, keepdims=True)
        p = jnp.exp(s - m)
        l = jnp.sum(p, axis=1, keepdims=True)
        outs.append(lax.dot_general(p.astype(BF16), vs[...], NT_DIMS, preferred_element_type=F32) / l)
        lses.append(m + jnp.log(l))
    lmax = jnp.maximum(jnp.maximum(lses[0], lses[1]), lses[2])
    es = [jnp.exp(l - lmax) for l in lses]
    tot = es[0] + es[1] + es[2]
    o = jnp.zeros((rows, gw), F32)
    for e, og in zip(es, outs):
        o = o + (e / tot) * og
    o = jnp.where(head_mask, o, 0.0)
    folded = o[0:s_new, :]
    for h in range(1, GROUP_HEADS):
        folded = folded + o[h * s_new:(h + 1) * s_new, :]
    zf = z_ref[...].astype(F32)
    a_ref[...] = (folded * (zf * _sigmoid(zf))).astype(BF16)


def _attn_sample(q, kn, vn, z, caches, biases):
    b, s_new, qw = q.shape
    gw = GROUP_HEADS * ATTN_HEAD_DIM
    assert s_new % 8 == 0
    cache2 = [jnp.transpose(cb, (0, 2, 3, 4, 1)).reshape(b, 2, gw, cb.shape[1]) for cb in caches]
    in_specs = [
        pl.BlockSpec((None, s_new, qw), lambda bi: (bi, 0, 0)),
        pl.BlockSpec((None, s_new, qw), lambda bi: (bi, 0, 0)),
        pl.BlockSpec((None, s_new, qw), lambda bi: (bi, 0, 0)),
        pl.BlockSpec((None, s_new, gw), lambda bi: (bi, 0, 0)),
    ]
    for cb in cache2:
        in_specs.append(pl.BlockSpec((None, 2, gw, cb.shape[3]), lambda bi: (bi, 0, 0, 0)))
    for bt in biases:
        in_specs.append(pl.BlockSpec(bt.shape, lambda bi: (0, 0)))
    scratch = []
    for cb in cache2:
        scratch += [pltpu.VMEM((gw, cb.shape[3] + LANES), BF16)] * 2
    return pl.pallas_call(
        functools.partial(_attn_sample_kernel, s_new=s_new),
        grid=(b,),
        in_specs=in_specs,
        out_specs=pl.BlockSpec((None, s_new, gw), lambda bi: (bi, 0, 0)),
        out_shape=jax.ShapeDtypeStruct((b, s_new, gw), BF16),
        scratch_shapes=scratch,
        compiler_params=_params(("arbitrary",)),
        name="attn_sample",
    )(q, kn, vn, z, *cache2, *biases)


def _matmul_residual_kernel(a_ref, x_ref, w_ref, y_ref):
    y_ref[...] = x_ref[...] + jnp.dot(a_ref[...], w_ref[...], preferred_element_type=F32)


def _matmul_residual(a2, x2, w, *, tm):
    m, d = x2.shape
    kdim = a2.shape[1]
    assert m % tm == 0
    return pl.pallas_call(
        _matmul_residual_kernel,
        grid=(m // tm,),
        in_specs=[
            pl.BlockSpec((tm, kdim), lambda i: (i, 0)),
            pl.BlockSpec((tm, d), lambda i: (i, 0)),
            pl.BlockSpec((kdim, d), lambda i: (0, 0)),
        ],
        out_specs=pl.BlockSpec((tm, d), lambda i: (i, 0)),
        out_shape=jax.ShapeDtypeStruct((m, d), F32),
        compiler_params=_params(("parallel",)),
        name="matmul_residual",
    )(a2, x2, w)


def _t5_bucket_np(dist):
    exact = N_BUCKETS // 2
    d = np.maximum(dist, 1).astype(np.float32)
    large = exact + (np.log(d / np.float32(exact)) / np.float32(math.log(MAX_DISTANCE / exact))
                     * np.float32(N_BUCKETS - exact)).astype(np.int32)
    return np.where(dist < exact, dist, np.minimum(large, N_BUCKETS - 1)).astype(np.int32)


def _bias_by_step(rel_bias_g, jmax, dil):
    return rel_bias_g.astype(F32)[_t5_bucket_np(np.arange(jmax + 1) * dil)]


def _prompt_bias(rel_bias_g, win, dil):
    jmax = win // dil
    assert jmax == ATTN_BLOCK
    qi = np.arange(ATTN_BLOCK)[:, None]
    kj = np.arange(2 * ATTN_BLOCK)[None, :]
    rel = qi + ATTN_BLOCK - kj
    band = (rel >= 0) & (rel <= jmax)
    bvec = _bias_by_step(rel_bias_g, jmax, dil)
    period = 2 * ATTN_BLOCK + 1
    base = jnp.concatenate([bvec[::-1], jnp.zeros((period - jmax - 1, GROUP_HEADS), F32)], axis=0).T
    bias = jnp.tile(base, (1, ATTN_BLOCK))[:, :2 * ATTN_BLOCK * ATTN_BLOCK]
    bias = bias.reshape(GROUP_HEADS, ATTN_BLOCK, 2 * ATTN_BLOCK)
    rest = jnp.where(band[None], bias, -jnp.inf)
    first = jnp.where((band & (kj >= ATTN_BLOCK))[None], bias, -jnp.inf)
    return jnp.swapaxes(jnp.stack([first, rest], axis=0), 2, 3)


def _sample_bias(rel_bias_g, win, dil, buf_len, s_new):
    jmax = win // dil
    assert buf_len == jmax * dil
    width = buf_len + LANES
    bvec = _bias_by_step(rel_bias_g, jmax, dil)
    gaps = jnp.full((jmax + 1, dil - 1, GROUP_HEADS), -jnp.inf, F32)
    by_dist = jnp.concatenate([bvec[:, None, :], gaps], axis=1).reshape((jmax + 1) * dil, GROUP_HEADS)
    padded = jnp.pad(by_dist[::-1], ((s_new, width), (0, 0)), constant_values=-jnp.inf)
    rows = [padded[dil - 1 - s + s_new:dil - 1 - s + s_new + width] for s in range(s_new)]
    table = jnp.transpose(jnp.stack(rows, axis=0), (2, 0, 1))
    return table.reshape(GROUP_HEADS * s_new, width)


def _layer_a(x3, state, weights, *, chunk, pad_to):
    norm_a, wt, wgt, bg, hgain, wout = weights
    b, t, d = x3.shape
    H = MLSTM_HEADS
    di = wout.shape[0]
    dh = di // H
    k_scale = dh ** -0.5
    k_idx = 1
    if pad_to == t:
        p3, gc3, gr3, kt3 = _inproj(x3, norm_a, wt, None, wgt, bg, tm=min(1024, t), tn=di,
                                    n_heads=H, k_tile=k_idx, k_scale=k_scale)
        col_idx = (0, 1, 2, 3)
        xin = x3
    else:
        m = b * t
        extra = pad_to - t
        colscale = jnp.concatenate([jnp.ones((di,), F32), jnp.full((di,), k_scale, F32),
                                    jnp.ones((3 * di,), F32)])[None, :]
        p, gc, gr = _inproj(x3.reshape(1, m, d), norm_a, wt, colscale, wgt, bg,
                            tm=m, tn=di, n_heads=H)
        col_idx = (0, 2, 3, 4)
        p3 = p.reshape(b, t, -1)
        k3 = p3[:, :, k_idx * di:(k_idx + 1) * di]
        kt3 = jnp.pad(jnp.swapaxes(k3, 1, 2), ((0, 0), (0, 0), (0, extra)))
        p3 = jnp.pad(p3, ((0, 0), (0, extra), (0, 0)))
        xin = jnp.pad(x3, ((0, 0), (0, extra), (0, 0)))
        lane = np.arange(LANES)
        pad_col = np.where(lane < H, -np.inf, 0.0).astype(np.float32)
        gc3 = jnp.concatenate([gc.reshape(b, t, LANES),
                               jnp.broadcast_to(pad_col, (b, extra, LANES))], axis=1)
        pad_row = np.where(np.arange(8) < H, -np.inf, 0.0).astype(np.float32)[:, None, None]
        gr3 = jnp.swapaxes(jnp.concatenate([gr.reshape(8, b, t),
                                            jnp.broadcast_to(pad_row, (8, b, extra))], axis=2), 0, 1)
    xo, c_out, n_out, m_out = _scan(p3, col_idx, kt3, gc3, gr3, xin, wout, hgain, state,
                                    chunk=chunk, n_heads=H)
    return xo[:, :t], c_out, n_out, m_out[:, :H, 0]


def kernel(x_prompt, x_sample, state_mlstm_C, state_mlstm_n, state_mlstm_m, cache_kv_w128, cache_kv_w512, cache_kv_w2048, norm_a, w_in_a, b_gates_a, hnorm_a, w_out_a, norm_kv, w_kv, k_norm, norm_b, w_in_b, q_norm, rel_bias, w_out_b):
    H = MLSTM_HEADS
    bp, tp, d = x_prompt.shape
    bs, ts, _ = x_sample.shape
    di = w_out_a.shape[1]
    dh = di // H
    gw = GROUP_HEADS * ATTN_HEAD_DIM
    qw = len(GROUPS) * gw
    caches = (cache_kv_w128, cache_kv_w512, cache_kv_w2048)
    assert norm_a.shape[0] == 1 and norm_b.shape[0] == 1, "one mLSTM layer, one attention layer"
    for cb, (win, _) in zip(caches, GROUPS):
        assert cb.shape[1] == win, "window buffers must hold a full window"

    w_at = w_in_a[0].T
    wt_a = w_at[:5 * di].astype(BF16)
    wgt = jnp.pad(w_at[5 * di:].astype(F32), ((0, LANES - 2 * H), (0, 0)))
    bg = jnp.pad(b_gates_a[0].astype(F32), (0, LANES - 2 * H))[None, :]
    na = norm_a[0].astype(F32)[None, :]
    hgain = hnorm_a[0].astype(F32)[None, :]
    wout_a = w_out_a[0].astype(BF16)
    weights_a = (na, wt_a, wgt, bg, hgain, wout_a)

    xp1, c_p, n_p, m_p = _layer_a(x_prompt, None, weights_a, chunk=256, pad_to=tp)
    m0 = jnp.pad(jnp.broadcast_to(state_mlstm_m[0].astype(F32)[:, :, None], (bs, H, LANES)),
                 ((0, 0), (0, 8 - H), (0, 0)))
    state_s = (state_mlstm_C[0].astype(F32), state_mlstm_n[0].astype(F32), m0)
    xs1, c_s, n_s, m_s = _layer_a(x_sample, state_s, weights_a, chunk=16, pad_to=16)

    nkv = norm_kv.astype(F32)[None, :]
    wkv = w_kv.astype(BF16)
    kgain = jnp.tile(k_norm.astype(F32), qw // ATTN_HEAD_DIM)[None, :]
    rows_p = [min(win, tp) for win, _ in GROUPS]
    dils = tuple(dil for _, dil in GROUPS)
    nb_ = norm_b[0].astype(F32)[None, :]
    wb = w_in_b[0].astype(BF16)
    qgain = jnp.tile(q_norm[0].astype(F32), qw // ATTN_HEAD_DIM)[None, :]
    qscale = ATTN_HEAD_DIM ** -0.5
    kp, vp, qp, zp, (kv128_p, kv512_p, kv2048_p) = _kvq_proj(
        xp1, nkv, wkv, kgain, nb_, wb, qgain, qscale=qscale, tm=512, tail_rows=rows_p, dils=dils)
    xs1_flat = xs1.reshape(bs * ts, d)
    by_seq = lambda a: a.reshape(bs, ts, a.shape[-1])
    ks, vs, kv128_s, kv512_s, kv2048_s = map(by_seq, _proj_headnorm(
        xs1_flat, nkv, wkv, kgain, n_norm=qw, scale=1.0, group_rows=True))

    wout_b = w_out_b[0].astype(BF16)
    qs, zs = map(by_seq, _proj_headnorm(xs1_flat, nb_, wb, qgain, n_norm=qw, scale=qscale))

    outs, lses = [], []
    for g, (win, dil) in enumerate(GROUPS):
        bias = _prompt_bias(rel_bias[:, g * GROUP_HEADS:(g + 1) * GROUP_HEADS], win, dil)
        o, lse = _attn_prompt(qp[g], kp[g], vp[g], bias, g)
        outs.append(o)
        lses.append(lse)
    y_p = _merge_out(outs, lses, zp, xp1, wout_b, tm=1024)

    sbias = [_sample_bias(rel_bias[:, g * GROUP_HEADS:(g + 1) * GROUP_HEADS], win, dil,
                          caches[g].shape[1], ts) for g, (win, dil) in enumerate(GROUPS)]
    a_s = _attn_sample(qs, ks, vs, zs, caches, sbias)
    y_s = _matmul_residual(a_s.reshape(bs * ts, gw), xs1.reshape(bs * ts, d), wout_b,
                           tm=bs * ts).reshape(bs, ts, d)

    kv5 = lambda a: a.reshape(a.shape[0], a.shape[1], 2, GROUP_HEADS, ATTN_HEAD_DIM)
    return (y_p, y_s, c_p[None], n_p[None], m_p[None], c_s[None], n_s[None], m_s[None],
            kv5(kv128_p), kv5(kv512_p), kv5(kv2048_p), kv5(kv128_s), kv5(kv512_s), kv5(kv2048_s))
```

```python
import functools
import math

import numpy as np
import jax
import jax.numpy as jnp
from jax import lax
from jax.experimental import pallas as pl
from jax.experimental.pallas import tpu as pltpu

F32 = jnp.float32
BF16 = jnp.bfloat16

EPS = 1e-6
MLSTM_HEADS = 4
GROUPS = ((128, 1), (512, 4), (2048, 16))
GROUP_HEADS = 8
ATTN_HEAD_DIM = 64
ATTN_BLOCK = 128
N_BUCKETS = 32
MAX_DISTANCE = 2048

LANES = 128
MXU_DIM = 256
VMEM_LIMIT_BYTES = 56 * 1024 * 1024

NT_DIMS = (((1,), (1,)), ((), ()))


def _params(sem):
    return pltpu.CompilerParams(dimension_semantics=sem, vmem_limit_bytes=VMEM_LIMIT_BYTES)


def _rms_scale(xf):
    return lax.rsqrt(jnp.mean(xf * xf, axis=-1, keepdims=True) + EPS)


def _sigmoid(x):
    return 0.5 * jnp.tanh(0.5 * x) + 0.5


def _split_bf16(a):
    hi = a.astype(BF16)
    lo = (a - hi.astype(F32)).astype(BF16)
    return hi, lo


def _inproj_kernel(*refs, n_heads, k_tile, k_scale, use_colscale, o_tile):
    x_ref, g_ref, wt_ref, cs_ref, wgt_ref, bg_ref, p_ref, gc_ref, gr_ref = refs[:9]
    kt_ref = refs[9] if k_tile is not None else None
    xn_ref, so_ref = refs[-2:]
    z_tile = o_tile + 1
    j = pl.program_id(2)

    @pl.when(j == 0)
    def _():
        xf = x_ref[...]
        xn = xf * _rms_scale(xf) * g_ref[...]
        xh, xl = _split_bf16(xn)
        wh, wl = _split_bf16(wgt_ref[...])
        xn_ref[...] = xh
        gates = (lax.dot_general(xh, wh, NT_DIMS, preferred_element_type=F32)
                 + lax.dot_general(xl, wh, NT_DIMS, preferred_element_type=F32)
                 + lax.dot_general(xh, wl, NT_DIMS, preferred_element_type=F32)) + bg_ref[...]
        lane = lax.broadcasted_iota(jnp.int32, gates.shape, 1)
        logsig = jnp.minimum(gates, 0.0) - jnp.log(1.0 + jnp.exp(-jnp.abs(gates)))
        gcol = jnp.where(lane < n_heads, gates, jnp.where(lane < 2 * n_heads, logsig, 0.0))
        gc_ref[...] = gcol
        gr_ref[...] = gcol.T[:8, :]

    def token_major():
        return lax.dot_general(xn_ref[...], wt_ref[...], NT_DIMS, preferred_element_type=F32)

    plain = (j != o_tile) & (j != z_tile)
    if k_tile is not None:
        plain = plain & (j != k_tile)

        @pl.when(j == k_tile)
        def _():
            acc = lax.dot_general(wt_ref[...], xn_ref[...], NT_DIMS, preferred_element_type=F32)
            kt_ref[...] = (acc * k_scale).astype(BF16)

    @pl.when(plain)
    def _():
        acc = token_major()
        if use_colscale:
            acc = acc * cs_ref[...]
        p_ref[...] = acc.astype(BF16)

    @pl.when(j == o_tile)
    def _():
        so_ref[...] = _sigmoid(token_major()).astype(BF16)

    @pl.when(j == z_tile)
    def _():
        z = token_major()
        p_ref[...] = (so_ref[...].astype(F32) * (z * _sigmoid(z))).astype(BF16)


def _inproj(x3, g, wt, colscale, wgt, bg, *, tm, tn, n_heads, k_tile=None, k_scale=1.0):
    b, t, d = x3.shape
    n = wt.shape[0]
    assert t % tm == 0 and n % tn == 0 and 2 * n_heads <= 8
    nj = n // tn
    o_tile = nj - 2
    assert o_tile >= 1 and (k_tile is None or 1 <= k_tile < o_tile)
    held = (o_tile,) if k_tile is None else (k_tile, o_tile)
    p_tile = lambda j: j - sum((j >= s).astype(jnp.int32) for s in held)
    n_out = n - len(held) * tn
    out_specs = [
        pl.BlockSpec((None, tm, tn), lambda bi, i, j: (bi, i, p_tile(j))),
        pl.BlockSpec((None, tm, LANES), lambda bi, i, j: (bi, i, 0)),
        pl.BlockSpec((None, 8, tm), lambda bi, i, j: (bi, 0, i)),
    ]
    out_shape = [
        jax.ShapeDtypeStruct((b, t, n_out), BF16),
        jax.ShapeDtypeStruct((b, t, LANES), F32),
        jax.ShapeDtypeStruct((b, 8, t), F32),
    ]
    if k_tile is not None:
        out_specs.append(pl.BlockSpec((None, tn, tm), lambda bi, i, j: (bi, 0, i)))
        out_shape.append(jax.ShapeDtypeStruct((b, tn, t), BF16))
    use_colscale = colscale is not None
    if colscale is None:
        colscale = jnp.ones((1, n), F32)
    return pl.pallas_call(
        functools.partial(_inproj_kernel, n_heads=n_heads, k_tile=k_tile, k_scale=k_scale,
                          use_colscale=use_colscale, o_tile=o_tile),
        grid=(b, t // tm, nj),
        in_specs=[
            pl.BlockSpec((None, tm, d), lambda bi, i, j: (bi, i, 0)),
            pl.BlockSpec((1, d), lambda bi, i, j: (0, 0)),
            pl.BlockSpec((tn, d), lambda bi, i, j: (j, 0)),
            pl.BlockSpec((1, tn), lambda bi, i, j: (0, j)),
            pl.BlockSpec((LANES, d), lambda bi, i, j: (0, 0)),
            pl.BlockSpec((1, LANES), lambda bi, i, j: (0, 0)),
        ],
        out_specs=out_specs,
        out_shape=out_shape,
        scratch_shapes=[pltpu.VMEM((tm, d), BF16), pltpu.VMEM((tm, tn), BF16)],
        compiler_params=_params(("parallel", "parallel", "arbitrary")),
        name="inproj",
    )(x3, g, wt, colscale, wgt, bg)


def _scan_kernel(*refs, chunk, n_heads, dh, n_chunks, has_state):
    L, H = chunk, n_heads
    dext = dh + LANES
    (q_ref, kt_ref, v_ref, gate_ref, gc_ref, gr_ref, x_ref, wout_ref, hg_ref) = refs[:9]
    pos = 9
    if has_state:
        c0_ref, n0_ref, m0_ref = refs[pos:pos + 3]
        pos += 3
    xo_ref, cout_ref, nout_ref, mout_ref = refs[pos:pos + 4]
    cext_ref, cb_ref, m_ref = refs[pos + 4:pos + 7]
    c = pl.program_id(1)

    @pl.when(c == 0)
    def _():
        if has_state:
            lane0 = lax.broadcasted_iota(jnp.int32, (dh, LANES), 1) == 0
            for h in range(H):
                ncol = jnp.broadcast_to(n0_ref[h:h + 1, :], (LANES, dh)).T
                cext_ref[h, :, :dh] = c0_ref[h]
                cext_ref[h, :, dh:] = jnp.where(lane0, ncol, 0.0)
            m_ref[...] = m0_ref[...]
        else:
            cext_ref[...] = jnp.zeros(cext_ref.shape, F32)
            m_ref[...] = jnp.zeros(m_ref.shape, F32)
        cb_ref[...] = cext_ref[...].astype(BF16)

    gc = gc_ref[...]
    gr = gr_ref[...]
    row = lax.broadcasted_iota(jnp.int32, (L, L), 0)
    col = lax.broadcasted_iota(jnp.int32, (L, L), 1)
    causal = row >= col
    lane = lax.broadcasted_iota(jnp.int32, gc.shape, 1)
    subl = lax.broadcasted_iota(jnp.int32, gr.shape, 0)
    lf_c_hi, lf_c_lo = _split_bf16(jnp.where((lane >= H) & (lane < 2 * H), gc, 0.0))
    lf_r_hi, lf_r_lo = _split_bf16(jnp.where((subl >= H) & (subl < 2 * H), gr, 0.0))
    lower = causal.astype(BF16)
    upper = (row <= col).astype(BF16)
    bc_all = (jnp.dot(lower, lf_c_hi, preferred_element_type=F32)
              + jnp.dot(lower, lf_c_lo, preferred_element_type=F32))
    br_all = (jnp.dot(lf_r_hi, upper, preferred_element_type=F32)
              + jnp.dot(lf_r_lo, upper, preferred_element_type=F32))
    ones_col = (lax.broadcasted_iota(jnp.int32, (L, LANES), 1) == 0).astype(BF16)
    y = jnp.zeros((L, wout_ref.shape[1]), F32)

    for h in range(H):
        sl = slice(h * dh, (h + 1) * dh)
        q = q_ref[:, sl]
        kt = kt_ref[sl, :]
        vext = jnp.concatenate([v_ref[:, sl], ones_col], axis=1)
        ig_r = gr[h:h + 1, :]
        b_c = bc_all[:, H + h:H + h + 1]
        b_r = br_all[H + h:H + h + 1, :]
        m_prev = m_ref[h:h + 1, 0:1]

        log_d = jnp.where(causal, b_c - b_r + ig_r, -jnp.inf)
        log_inter = b_c + m_prev
        m_t = jnp.maximum(log_inter, jnp.max(log_d, axis=1, keepdims=True))
        dmat = jnp.exp(log_d - m_t)
        inter = jnp.exp(log_inter - m_t)
        s = jnp.dot(q, kt, preferred_element_type=F32) * dmat
        numden = (jnp.dot(s.astype(BF16), vext, preferred_element_type=F32)
                  + inter * jnp.dot(q, cb_ref[h], preferred_element_type=F32))
        num = numden[:, :dh]
        den = numden[:, dh:dh + 1]
        hh = num / jnp.maximum(jnp.abs(den), jnp.exp(-m_t))

        b_last = b_r[:, L - 1:L]
        a_r = b_last - b_r + ig_r
        m_new = jnp.maximum(b_last + m_prev, jnp.max(a_r, axis=1, keepdims=True))
        w_r = jnp.exp(a_r - m_new)
        decay = jnp.exp(b_last + m_prev - m_new)
        ktw = (kt.astype(F32) * w_r).astype(BF16)
        c_new = decay * cext_ref[h] + jnp.dot(ktw, vext, preferred_element_type=F32)
        cext_ref[h] = c_new
        cb_ref[h] = c_new.astype(BF16)
        m_ref[h:h + 1, :] = jnp.broadcast_to(m_new, (1, LANES))

        hn = hh * lax.rsqrt(jnp.mean(hh * hh, axis=1, keepdims=True) + EPS)
        hg = (hn * hg_ref[:, sl] * gate_ref[:, sl].astype(F32)).astype(BF16)
        y = y + jnp.dot(hg, wout_ref[sl, :], preferred_element_type=F32)

    xo_ref[...] = x_ref[...] + y

    @pl.when(c == n_chunks - 1)
    def _():
        for h in range(H):
            cout_ref[h] = cext_ref[h, :, :dh]
            nout_ref[h:h + 1, :] = cext_ref[h, :, dh:].T[0:1, :]
        mout_ref[...] = m_ref[...]


def _scan(p3, col_idx, kt3, gc, gr, x3, wout, hgain, state, *, chunk, n_heads):
    b, t, _ = p3.shape
    dh = kt3.shape[1] // n_heads
    di = n_heads * dh
    d = x3.shape[2]
    nc = t // chunk
    assert t % chunk == 0
    has_state = state is not None
    qi, vi, gi = col_idx

    def pspec(ci):
        return pl.BlockSpec((None, chunk, di), lambda bi, c, ci=ci: (bi, c, ci))

    in_specs = [
        pspec(qi),
        pl.BlockSpec((None, di, chunk), lambda bi, c: (bi, 0, c)),
        pspec(vi), pspec(gi),
        pl.BlockSpec((None, chunk, LANES), lambda bi, c: (bi, c, 0)),
        pl.BlockSpec((None, 8, chunk), lambda bi, c: (bi, 0, c)),
        pl.BlockSpec((None, chunk, d), lambda bi, c: (bi, c, 0)),
        pl.BlockSpec((di, d), lambda bi, c: (0, 0)),
        pl.BlockSpec((1, di), lambda bi, c: (0, 0)),
    ]
    args = [p3, kt3, p3, p3, gc, gr, x3, wout, hgain]
    if has_state:
        c0, n0, m0 = state
        in_specs += [
            pl.BlockSpec((None, n_heads, dh, dh), lambda bi, c: (bi, 0, 0, 0)),
            pl.BlockSpec((None, n_heads, dh), lambda bi, c: (bi, 0, 0)),
            pl.BlockSpec((None, 8, LANES), lambda bi, c: (bi, 0, 0)),
        ]
        args += [c0, n0, m0]
    return pl.pallas_call(
        functools.partial(_scan_kernel, chunk=chunk, n_heads=n_heads, dh=dh,
                          n_chunks=nc, has_state=has_state),
        grid=(b, nc),
        in_specs=in_specs,
        out_specs=[
            pl.BlockSpec((None, chunk, d), lambda bi, c: (bi, c, 0)),
            pl.BlockSpec((None, n_heads, dh, dh), lambda bi, c: (bi, 0, 0, 0)),
            pl.BlockSpec((None, n_heads, dh), lambda bi, c: (bi, 0, 0)),
            pl.BlockSpec((None, 8, LANES), lambda bi, c: (bi, 0, 0)),
        ],
        out_shape=[
            jax.ShapeDtypeStruct((b, t, d), F32),
            jax.ShapeDtypeStruct((b, n_heads, dh, dh), F32),
            jax.ShapeDtypeStruct((b, n_heads, dh), F32),
            jax.ShapeDtypeStruct((b, 8, LANES), F32),
        ],
        scratch_shapes=[
            pltpu.VMEM((n_heads, dh, dh + LANES), F32),
            pltpu.VMEM((n_heads, dh, dh + LANES), BF16),
            pltpu.VMEM((8, LANES), F32),
        ],
        compiler_params=_params(("parallel", "arbitrary")),
        name="mlstm_scan",
    )(*args)


def _headnorm(a):
    n = a.shape[1]
    r = lax.broadcasted_iota(jnp.int32, (MXU_DIM, MXU_DIM), 0)
    c = lax.broadcasted_iota(jnp.int32, (MXU_DIM, MXU_DIM), 1)
    same_head = (lax.shift_right_logical(r, 6) == lax.shift_right_logical(c, 6)).astype(BF16)
    parts = []
    for c0 in range(0, n, MXU_DIM):
        blk = a[:, c0:c0 + MXU_DIM]
        ss = jnp.dot((blk * blk).astype(BF16), same_head, preferred_element_type=F32)
        parts.append(blk * lax.rsqrt(ss * (1.0 / ATTN_HEAD_DIM) + EPS))
    return jnp.concatenate(parts, axis=1)


def _store_by_residue(ref, val, scr, dil):
    rows, width = val.shape
    if dil == 1:
        ref[0] = val.astype(BF16)
        return
    for c in range(width // LANES):
        scr[c] = val[:, c * LANES:(c + 1) * LANES]
    for r in range(dil):
        parts = [scr[c, pl.ds(r, rows // dil, stride=dil), :] for c in range(width // LANES)]
        ref[r] = jnp.concatenate(parts, axis=1).astype(BF16)


def _proj_headnorm_kernel(x_ref, g_ref, w_ref, hg_ref, a_ref, r_ref, *row_refs, n_norm, scale):
    gw = GROUP_HEADS * ATTN_HEAD_DIM
    xf = x_ref[...]
    xn = (xf * _rms_scale(xf) * g_ref[...]).astype(BF16)
    p = jnp.dot(xn, w_ref[...], preferred_element_type=F32)
    a = _headnorm(p[:, :n_norm]) * hg_ref[...]
    if scale != 1.0:
        a = a * scale
    r = p[:, n_norm:]
    a_ref[...] = a.astype(BF16)
    r_ref[...] = r.astype(BF16)
    for g, row_ref in enumerate(row_refs):
        cs = slice(g * gw, (g + 1) * gw)
        row_ref[:, :gw] = a[:, cs]
        row_ref[:, gw:] = r[:, cs]


def _proj_headnorm(x2, g, w, hgain, *, n_norm, scale, group_rows=False):
    m, d = x2.shape
    n = w.shape[1]
    gw = GROUP_HEADS * ATTN_HEAD_DIM
    assert n_norm % MXU_DIM == 0 and m % 16 == 0
    whole = lambda shape: pl.BlockSpec(shape, lambda i: (0, 0))
    out_specs = [whole((m, n_norm)), whole((m, n - n_norm))]
    out_shape = [jax.ShapeDtypeStruct((m, n_norm), BF16), jax.ShapeDtypeStruct((m, n - n_norm), BF16)]
    if group_rows:
        assert n - n_norm == n_norm
        for _ in range(n_norm // gw):
            out_specs.append(whole((m, 2 * gw)))
            out_shape.append(jax.ShapeDtypeStruct((m, 2 * gw), F32))
    return pl.pallas_call(
        functools.partial(_proj_headnorm_kernel, n_norm=n_norm, scale=scale),
        grid=(1,),
        in_specs=[whole((m, d)), whole((1, d)), whole((d, n)), whole((1, n_norm))],
        out_specs=out_specs,
        out_shape=out_shape,
        compiler_params=_params(("arbitrary",)),
        name="proj_headnorm",
    )(x2, g, w, hgain)


def _kvq_proj_kernel(x_ref, gkv_ref, wkv_ref, kg_ref, gb_ref, wb_ref, qg_ref, *refs,
                     qscale, tails, tm, dils):
    ng = len(dils)
    gw = GROUP_HEADS * ATTN_HEAD_DIM
    n_norm = ng * gw
    k_refs, v_refs, q_refs = refs[:ng], refs[ng:2 * ng], refs[2 * ng:3 * ng]
    z_ref = refs[3 * ng]
    tail_refs = refs[3 * ng + 1:3 * ng + 1 + len(tails)]
    scr = refs[-1]
    xf = x_ref[...]
    xhat = xf * _rms_scale(xf)
    xkv = (xhat * gkv_ref[...]).astype(BF16)
    xb = (xhat * gb_ref[...]).astype(BF16)
    for g, dil in enumerate(dils):
        cs = slice(g * gw, (g + 1) * gw)
        k = _headnorm(jnp.dot(xkv, wkv_ref[:, cs], preferred_element_type=F32)) * kg_ref[:, cs]
        v = jnp.dot(xkv, wkv_ref[:, n_norm + g * gw:n_norm + (g + 1) * gw],
                    preferred_element_type=F32)
        q = (_headnorm(jnp.dot(xb, wb_ref[:, cs], preferred_element_type=F32))
             * qg_ref[:, cs] * qscale)
        _store_by_residue(k_refs[g], k, scr, dil)
        _store_by_residue(v_refs[g], v, scr, dil)
        _store_by_residue(q_refs[g], q, scr, dil)
        rows, t_ref = tails[g][0], tail_refs[g]
        t_ref[:, :gw] = k[max(tm - rows, 0):, :]
        t_ref[:, gw:] = v[max(tm - rows, 0):, :]
    z_ref[...] = jnp.dot(xb, wb_ref[:, n_norm:], preferred_element_type=F32).astype(BF16)


def _kvq_proj(x3, gkv, wkv, kgain, gb, wb, qgain, *, qscale, tm, tail_rows, dils):
    b, t, d = x3.shape
    gw = GROUP_HEADS * ATTN_HEAD_DIM
    ng = len(dils)
    assert t % tm == 0 and wkv.shape[1] == 2 * ng * gw and wb.shape[1] > ng * gw
    n_tiles = t // tm
    out_specs, out_shape = [], []
    for _ in range(3):
        for dil in dils:
            assert tm % (dil * 16) == 0
            out_specs.append(pl.BlockSpec((None, dil, tm // dil, gw), lambda bi, i: (bi, 0, i, 0)))
            out_shape.append(jax.ShapeDtypeStruct((b, dil, t // dil, gw), BF16))
    zw = wb.shape[1] - ng * gw
    out_specs.append(pl.BlockSpec((None, tm, zw), lambda bi, i: (bi, i, 0)))
    out_shape.append(jax.ShapeDtypeStruct((b, t, zw), BF16))
    tails = []
    for rows in tail_rows:
        if rows >= tm:
            assert rows % tm == 0
            first, blk = n_tiles - rows // tm, tm
        else:
            first, blk = n_tiles - 1, rows
        tails.append((rows, first))
        out_specs.append(pl.BlockSpec(
            (None, blk, 2 * gw), lambda bi, i, first=first: (bi, jnp.maximum(i - first, 0), 0)))
        out_shape.append(jax.ShapeDtypeStruct((b, rows, 2 * gw), F32))
    const = lambda shape: pl.BlockSpec(shape, lambda bi, i: (0, 0))
    outs = pl.pallas_call(
        functools.partial(_kvq_proj_kernel, qscale=qscale, tails=tuple(tails), tm=tm, dils=dils),
        grid=(b, n_tiles),
        in_specs=[
            pl.BlockSpec((None, tm, d), lambda bi, i: (bi, i, 0)),
            const((1, d)), const(wkv.shape), const((1, ng * gw)),
            const((1, d)), const(wb.shape), const((1, ng * gw)),
        ],
        out_specs=out_specs,
        out_shape=out_shape,
        scratch_shapes=[pltpu.VMEM((gw // LANES, tm, LANES), F32)],
        compiler_params=_params(("parallel", "arbitrary")),
        name="kvq_proj",
    )(x3, gkv, wkv, kgain, gb, wb, qgain)
    return outs[:ng], outs[ng:2 * ng], outs[2 * ng:3 * ng], outs[3 * ng], outs[3 * ng + 1:]


def _attn_prompt_kernel(q_ref, kp_ref, kc_ref, vp_ref, vc_ref, bias_ref, o_ref, lse_ref, *, nq):
    blk = ATTN_BLOCK
    first = jnp.where(pl.program_id(2) == 0, 0, 1)
    low_half = lax.broadcasted_iota(jnp.int32, (blk, LANES), 1) < ATTN_HEAD_DIM
    for rr, jb in [(rr, jb) for rr in range(q_ref.shape[0]) for jb in range(nq)]:
        q = q_ref[rr, jb * blk:(jb + 1) * blk, :]
        if jb == 0:
            kcat = jnp.concatenate([kp_ref[rr], kc_ref[rr, 0:blk, :]], axis=0)
            vcat = jnp.concatenate([vp_ref[rr], vc_ref[rr, 0:blk, :]], axis=0)
        else:
            kcat = kc_ref[rr, (jb - 1) * blk:(jb + 1) * blk, :]
            vcat = vc_ref[rr, (jb - 1) * blk:(jb + 1) * blk, :]
        lses, outs = [], []
        for j in range(GROUP_HEADS // 2):
            ps = slice(j * LANES, (j + 1) * LANES)
            qf = q[:, ps].astype(F32)
            kpair, vpair = kcat[:, ps], vcat[:, ps]
            pair = []
            for half in range(2):
                h = 2 * j + half
                qh = jnp.where(low_half if half == 0 else ~low_half, qf, 0.0).astype(BF16)
                bias = bias_ref[first, h] if jb == 0 else bias_ref[1, h]
                st = lax.dot_general(kpair, qh, NT_DIMS, preferred_element_type=F32) + bias
                m = jnp.max(st, axis=0, keepdims=True)
                p = jnp.exp(st - m)
                l = jnp.sum(p, axis=0, keepdims=True)
                pn = (p * (1.0 / l)).T.astype(BF16)
                pair.append(jnp.dot(pn, vpair, preferred_element_type=F32))
                lses.append(m + jnp.log(l))
            outs.append(jnp.where(low_half, pair[0], pair[1]))
        o_ref[rr, jb * blk:(jb + 1) * blk, :] = jnp.concatenate(outs, axis=1).astype(o_ref.dtype)
        lse_ref[rr, jb * blk:(jb + 1) * blk, :] = jnp.concatenate(
            lses + [jnp.zeros((LANES - GROUP_HEADS, blk), F32)], axis=0).T


def _attn_prompt(q, k, v, bias, g):
    b, dil, s, gw = q.shape
    nb = s // ATTN_BLOCK
    assert s % ATTN_BLOCK == 0
    blocks_per_step = 8
    nq = min(blocks_per_step, nb)
    rps = min(blocks_per_step // nq, dil)
    assert nb % nq == 0 and dil % rps == 0
    rows = nq * ATTN_BLOCK
    cur = pl.BlockSpec((None, rps, rows, gw), lambda bi, r, j: (bi, r, j, 0))
    prev = pl.BlockSpec((None, rps, ATTN_BLOCK, gw),
                        lambda bi, r, j: (bi, r, jnp.maximum(j * nq - 1, 0), 0))
    return pl.pallas_call(
        functools.partial(_attn_prompt_kernel, nq=nq),
        grid=(b, dil // rps, nb // nq),
        in_specs=[
            cur, prev, cur, prev, cur,
            pl.BlockSpec(bias.shape, lambda bi, r, j: (0, 0, 0, 0)),
        ],
        out_specs=[
            pl.BlockSpec((None, rps, rows, gw), lambda bi, r, j: (bi, r, j, 0)),
            pl.BlockSpec((None, rps, rows, LANES), lambda bi, r, j: (bi, r, j, 0)),
        ],
        out_shape=[
            jax.ShapeDtypeStruct((b, dil, s, gw), BF16),
            jax.ShapeDtypeStruct((b, dil, s, LANES), F32),
        ],
        compiler_params=_params(("parallel", "parallel", "arbitrary")),
        name="attn_prompt_g%d" % g,
    )(q, k, k, v, v, bias)


def _head_expand_matrix():
    r = lax.broadcasted_iota(jnp.int32, (LANES, GROUP_HEADS * ATTN_HEAD_DIM), 0)
    c = lax.broadcasted_iota(jnp.int32, (LANES, GROUP_HEADS * ATTN_HEAD_DIM), 1)
    return (r == lax.shift_right_logical(c, 6)).astype(BF16)


def _load_token_order(ref, scr):
    dil, per, width = ref.shape
    if dil == 1:
        return ref[0].astype(F32)
    n_tiles = width // LANES
    for r in range(dil):
        val = ref[r].astype(F32)
        for c in range(n_tiles):
            scr[c, pl.ds(r, per, stride=dil), :] = val[:, c * LANES:(c + 1) * LANES]
    return jnp.concatenate([scr[c] for c in range(n_tiles)], axis=1)


def _merge_out_kernel(o0_ref, o1_ref, o2_ref, l0_ref, l1_ref, l2_ref, z_ref, x_ref, w_ref, y_ref,
                      *scratch):
    o_refs, l_refs = (o0_ref, o1_ref, o2_ref), (l0_ref, l1_ref, l2_ref)
    os_, ls, k = [], [], 0
    for o_ref, l_ref in zip(o_refs, l_refs):
        if o_ref.shape[0] == 1:
            os_.append(o_ref[0].astype(F32))
            ls.append(l_ref[0])
        else:
            os_.append(_load_token_order(o_ref, scratch[k]))
            ls.append(_load_token_order(l_ref, scratch[k + 1]))
            k += 2
    lmax = jnp.maximum(jnp.maximum(ls[0], ls[1]), ls[2])
    es = [jnp.exp(l - lmax) for l in ls]
    tot = es[0] + es[1] + es[2]
    expand = _head_expand_matrix()
    o = jnp.zeros(os_[0].shape, F32)
    for e, og in zip(es, os_):
        hi, lo = _split_bf16(e / tot)
        wexp = (jnp.dot(hi, expand, preferred_element_type=F32)
                + jnp.dot(lo, expand, preferred_element_type=F32))
        o = o + wexp * og
    zf = z_ref[...].astype(F32)
    a = (o * (zf * _sigmoid(zf))).astype(BF16)
    y_ref[...] = x_ref[...] + jnp.dot(a, w_ref[...], preferred_element_type=F32)


def _merge_out(outs, lses, z3, x3, w, *, tm):
    b, t, d = x3.shape
    gw = w.shape[0]
    assert t % tm == 0
    row = lambda width: pl.BlockSpec((None, tm, width), lambda bi, i: (bi, i, 0))
    by_residue = lambda a: pl.BlockSpec((None, a.shape[1], tm // a.shape[1], a.shape[3]),
                                        lambda bi, i: (bi, 0, i, 0))
    scratch = []
    for o in outs:
        if o.shape[1] > 1:
            scratch += [pltpu.VMEM((gw // LANES, tm, LANES), F32), pltpu.VMEM((1, tm, LANES), F32)]
    return pl.pallas_call(
        _merge_out_kernel,
        grid=(b, t // tm),
        in_specs=[by_residue(o) for o in outs] + [by_residue(l) for l in lses]
        + [row(gw), row(d), pl.BlockSpec((gw, d), lambda bi, i: (0, 0))],
        out_specs=row(d),
        out_shape=jax.ShapeDtypeStruct((b, t, d), F32),
        scratch_shapes=scratch,
        compiler_params=_params(("parallel", "parallel")),
        name="merge_out",
    )(*outs, *lses, z3, x3, w)


def _attn_sample_kernel(q_ref, kn_ref, vn_ref, z_ref, c0_ref, c1_ref, c2_ref,
                        b0_ref, b1_ref, b2_ref, a_ref, ks0, vs0, ks1, vs1, ks2, vs2, *, s_new):
    gw = GROUP_HEADS * ATTN_HEAD_DIM
    rows = GROUP_HEADS * s_new
    r = lax.broadcasted_iota(jnp.int32, (rows, gw), 0)
    c = lax.broadcasted_iota(jnp.int32, (rows, gw), 1)
    head_mask = (lax.shift_right_logical(r, int(math.log2(s_new)))
                 == lax.shift_right_logical(c, int(math.log2(ATTN_HEAD_DIM))))
    caches = ((c0_ref, b0_ref, ks0, vs0), (c1_ref, b1_ref, ks1, vs1), (c2_ref, b2_ref, ks2, vs2))
    pad_rows = jnp.zeros((LANES - s_new, gw), F32)
    outs, lses = [], []
    for g, (c_ref, b_ref, ks, vs) in enumerate(caches):
        buf_len = c_ref.shape[2]
        cs = slice(g * gw, (g + 1) * gw)
        ks[:, :buf_len] = c_ref[0].astype(BF16)
        vs[:, :buf_len] = c_ref[1].astype(BF16)
        ks[:, buf_len:] = jnp.concatenate([kn_ref[:, cs].astype(F32), pad_rows], axis=0).T.astype(BF16)
        vs[:, buf_len:] = jnp.concatenate([vn_ref[:, cs].astype(F32), pad_rows], axis=0).T.astype(BF16)
        qg = q_ref[:, cs].astype(F32)
        qbd = jnp.where(head_mask, jnp.concatenate([qg] * GROUP_HEADS, axis=0), 0.0).astype(BF16)
        s = jnp.dot(qbd, ks[...], preferred_element_type=F32) + b_ref[...]
        m = jnp.max(s, axis=1, keepdims=True)
        p = jnp.exp(s - m)
        l = jnp.sum(p, axis=1, keepdims=True)
        outs.append(lax.dot_general(p.astype(BF16), vs[...], NT_DIMS, preferred_element_type=F32) / l)
        lses.append(m + jnp.log(l))
    lmax = jnp.maximum(jnp.maximum(lses[0], lses[1]), lses[2])
    es = [jnp.exp(l - lmax) for l in lses]
    tot = es[0] + es[1] + es[2]
    o = jnp.zeros((rows, gw), F32)
    for e, og in zip(es, outs):
        o = o + (e / tot) * og
    o = jnp.where(head_mask, o, 0.0)
    folded = o[0:s_new, :]
    for h in range(1, GROUP_HEADS):
        folded = folded + o[h * s_new:(h + 1) * s_new, :]
    zf = z_ref[...].astype(F32)
    a_ref[...] = (folded * (zf * _sigmoid(zf))).astype(BF16)


def _attn_sample(q, kn, vn, z, caches, biases):
    b, s_new, qw = q.shape
    gw = GROUP_HEADS * ATTN_HEAD_DIM
    assert s_new % 8 == 0
    cache2 = [jnp.transpose(cb, (0, 2, 3, 4, 1)).reshape(b, 2, gw, cb.shape[1]) for cb in caches]
    in_specs = [
        pl.BlockSpec((None, s_new, qw), lambda bi: (bi, 0, 0)),
        pl.BlockSpec((None, s_new, qw), lambda bi: (bi, 0, 0)),
        pl.BlockSpec((None, s_new, qw), lambda bi: (bi, 0, 0)),
        pl.BlockSpec((None, s_new, gw), lambda bi: (bi, 0, 0)),
    ]
    for cb in cache2:
        in_specs.append(pl.BlockSpec((None, 2, gw, cb.shape[3]), lambda bi: (bi, 0, 0, 0)))
    for bt in biases:
        in_specs.append(pl.BlockSpec(bt.shape, lambda bi: (0, 0)))
    scratch = []
    for cb in cache2:
        scratch += [pltpu.VMEM((gw, cb.shape[3] + LANES), BF16)] * 2
    return pl.pallas_call(
        functools.partial(_attn_sample_kernel, s_new=s_new),
        grid=(b,),
        in_specs=in_specs,
        out_specs=pl.BlockSpec((None, s_new, gw), lambda bi: (bi, 0, 0)),
        out_shape=jax.ShapeDtypeStruct((b, s_new, gw), BF16),
        scratch_shapes=scratch,
        compiler_params=_params(("arbitrary",)),
        name="attn_sample",
    )(q, kn, vn, z, *cache2, *biases)


def _matmul_residual_kernel(a_ref, x_ref, w_ref, y_ref):
    y_ref[...] = x_ref[...] + jnp.dot(a_ref[...], w_ref[...], preferred_element_type=F32)


def _matmul_residual(a2, x2, w, *, tm):
    m, d = x2.shape
    kdim = a2.shape[1]
    assert m % tm == 0
    return pl.pallas_call(
        _matmul_residual_kernel,
        grid=(m // tm,),
        in_specs=[
            pl.BlockSpec((tm, kdim), lambda i: (i, 0)),
            pl.BlockSpec((tm, d), lambda i: (i, 0)),
            pl.BlockSpec((kdim, d), lambda i: (0, 0)),
        ],
        out_specs=pl.BlockSpec((tm, d), lambda i: (i, 0)),
        out_shape=jax.ShapeDtypeStruct((m, d), F32),
        compiler_params=_params(("parallel",)),
        name="matmul_residual",
    )(a2, x2, w)


def _t5_bucket_np(dist):
    exact = N_BUCKETS // 2
    d = np.maximum(dist, 1).astype(np.float32)
    large = exact + (np.log(d / np.float32(exact)) / np.float32(math.log(MAX_DISTANCE / exact))
                     * np.float32(N_BUCKETS - exact)).astype(np.int32)
    return np.where(dist < exact, dist, np.minimum(large, N_BUCKETS - 1)).astype(np.int32)


def _bias_by_step(rel_bias_g, jmax, dil):
    return rel_bias_g.astype(F32)[_t5_bucket_np(np.arange(jmax + 1) * dil)]


def _prompt_bias(rel_bias_g, win, dil):
    jmax = win // dil
    assert jmax == ATTN_BLOCK
    qi = np.arange(ATTN_BLOCK)[:, None]
    kj = np.arange(2 * ATTN_BLOCK)[None, :]
    rel = qi + ATTN_BLOCK - kj
    band = (rel >= 0) & (rel <= jmax)
    bvec = _bias_by_step(rel_bias_g, jmax, dil)
    period = 2 * ATTN_BLOCK + 1
    base = jnp.concatenate([bvec[::-1], jnp.zeros((period - jmax - 1, GROUP_HEADS), F32)], axis=0).T
    bias = jnp.tile(base, (1, ATTN_BLOCK))[:, :2 * ATTN_BLOCK * ATTN_BLOCK]
    bias = bias.reshape(GROUP_HEADS, ATTN_BLOCK, 2 * ATTN_BLOCK)
    rest = jnp.where(band[None], bias, -jnp.inf)
    first = jnp.where((band & (kj >= ATTN_BLOCK))[None], bias, -jnp.inf)
    return jnp.swapaxes(jnp.stack([first, rest], axis=0), 2, 3)


def _sample_bias(rel_bias_g, win, dil, buf_len, s_new):
    jmax = win // dil
    assert buf_len == jmax * dil
    width = buf_len + LANES
    bvec = _bias_by_step(rel_bias_g, jmax, dil)
    gaps = jnp.full((jmax + 1, dil - 1, GROUP_HEADS), -jnp.inf, F32)
    by_dist = jnp.concatenate([bvec[:, None, :], gaps], axis=1).reshape((jmax + 1) * dil, GROUP_HEADS)
    padded = jnp.pad(by_dist[::-1], ((s_new, width), (0, 0)), constant_values=-jnp.inf)
    rows = [padded[dil - 1 - s + s_new:dil - 1 - s + s_new + width] for s in range(s_new)]
    table = jnp.transpose(jnp.stack(rows, axis=0), (2, 0, 1))
    return table.reshape(GROUP_HEADS * s_new, width)


def _layer_a(x3, state, weights, *, chunk, pad_to):
    norm_a, wt, wgt, bg, hgain, wout = weights
    b, t, d = x3.shape
    H = MLSTM_HEADS
    di = wout.shape[0]
    dh = di // H
    k_scale = dh ** -0.5
    k_idx = 1
    if pad_to == t:
        p3, gc3, gr3, kt3 = _inproj(x3, norm_a, wt, None, wgt, bg, tm=min(1024, t), tn=di,
                                    n_heads=H, k_tile=k_idx, k_scale=k_scale)
        col_idx = (0, 1, 2)
        xin = x3
    else:
        m = b * t
        extra = pad_to - t
        colscale = jnp.concatenate([jnp.ones((di,), F32), jnp.full((di,), k_scale, F32),
                                    jnp.ones((3 * di,), F32)])[None, :]
        p, gc, gr = _inproj(x3.reshape(1, m, d), norm_a, wt, colscale, wgt, bg,
                            tm=m, tn=di, n_heads=H)
        col_idx = (0, 2, 3)
        p3 = p.reshape(b, t, -1)
        k3 = p3[:, :, k_idx * di:(k_idx + 1) * di]
        kt3 = jnp.pad(jnp.swapaxes(k3, 1, 2), ((0, 0), (0, 0), (0, extra)))
        p3 = jnp.pad(p3, ((0, 0), (0, extra), (0, 0)))
        xin = jnp.pad(x3, ((0, 0), (0, extra), (0, 0)))
        lane = np.arange(LANES)
        pad_col = np.where(lane < H, -np.inf, 0.0).astype(np.float32)
        gc3 = jnp.concatenate([gc.reshape(b, t, LANES),
                               jnp.broadcast_to(pad_col, (b, extra, LANES))], axis=1)
        pad_row = np.where(np.arange(8) < H, -np.inf, 0.0).astype(np.float32)[:, None, None]
        gr3 = jnp.swapaxes(jnp.concatenate([gr.reshape(8, b, t),
                                            jnp.broadcast_to(pad_row, (8, b, extra))], axis=2), 0, 1)
    xo, c_out, n_out, m_out = _scan(p3, col_idx, kt3, gc3, gr3, xin, wout, hgain, state,
                                    chunk=chunk, n_heads=H)
    return xo[:, :t], c_out, n_out, m_out[:, :H, 0]


def kernel(x_prompt, x_sample, state_mlstm_C, state_mlstm_n, state_mlstm_m, cache_kv_w128, cache_kv_w512, cache_kv_w2048, norm_a, w_in_a, b_gates_a, hnorm_a, w_out_a, norm_kv, w_kv, k_norm, norm_b, w_in_b, q_norm, rel_bias, w_out_b):
    H = MLSTM_HEADS
    bp, tp, d = x_prompt.shape
    bs, ts, _ = x_sample.shape
    di = w_out_a.shape[1]
    dh = di // H
    gw = GROUP_HEADS * ATTN_HEAD_DIM
    qw = len(GROUPS) * gw
    caches = (cache_kv_w128, cache_kv_w512, cache_kv_w2048)
    assert norm_a.shape[0] == 1 and norm_b.shape[0] == 1, "one mLSTM layer, one attention layer"
    for cb, (win, _) in zip(caches, GROUPS):
        assert cb.shape[1] == win, "window buffers must hold a full window"

    w_at = w_in_a[0].T
    wt_a = w_at[:5 * di].astype(BF16)
    wgt = jnp.pad(w_at[5 * di:].astype(F32), ((0, LANES - 2 * H), (0, 0)))
    bg = jnp.pad(b_gates_a[0].astype(F32), (0, LANES - 2 * H))[None, :]
    na = norm_a[0].astype(F32)[None, :]
    hgain = hnorm_a[0].astype(F32)[None, :]
    wout_a = w_out_a[0].astype(BF16)
    weights_a = (na, wt_a, wgt, bg, hgain, wout_a)

    xp1, c_p, n_p, m_p = _layer_a(x_prompt, None, weights_a, chunk=256, pad_to=tp)
    m0 = jnp.pad(jnp.broadcast_to(state_mlstm_m[0].astype(F32)[:, :, None], (bs, H, LANES)),
                 ((0, 0), (0, 8 - H), (0, 0)))
    state_s = (state_mlstm_C[0].astype(F32), state_mlstm_n[0].astype(F32), m0)
    xs1, c_s, n_s, m_s = _layer_a(x_sample, state_s, weights_a, chunk=16, pad_to=16)

    nkv = norm_kv.astype(F32)[None, :]
    wkv = w_kv.astype(BF16)
    kgain = jnp.tile(k_norm.astype(F32), qw // ATTN_HEAD_DIM)[None, :]
    rows_p = [min(win, tp) for win, _ in GROUPS]
    dils = tuple(dil for _, dil in GROUPS)
    nb_ = norm_b[0].astype(F32)[None, :]
    wb = w_in_b[0].astype(BF16)
    qgain = jnp.tile(q_norm[0].astype(F32), qw // ATTN_HEAD_DIM)[None, :]
    qscale = ATTN_HEAD_DIM ** -0.5
    kp, vp, qp, zp, (kv128_p, kv512_p, kv2048_p) = _kvq_proj(
        xp1, nkv, wkv, kgain, nb_, wb, qgain, qscale=qscale, tm=512, tail_rows=rows_p, dils=dils)
    xs1_flat = xs1.reshape(bs * ts, d)
    by_seq = lambda a: a.reshape(bs, ts, a.shape[-1])
    ks, vs, kv128_s, kv512_s, kv2048_s = map(by_seq, _proj_headnorm(
        xs1_flat, nkv, wkv, kgain, n_norm=qw, scale=1.0, group_rows=True))

    wout_b = w_out_b[0].astype(BF16)
    qs, zs = map(by_seq, _proj_headnorm(xs1_flat, nb_, wb, qgain, n_norm=qw, scale=qscale))

    outs, lses = [], []
    for g, (win, dil) in enumerate(GROUPS):
        bias = _prompt_bias(rel_bias[:, g * GROUP_HEADS:(g + 1) * GROUP_HEADS], win, dil)
        o, lse = _attn_prompt(qp[g], kp[g], vp[g], bias, g)
        outs.append(o)
        lses.append(lse)
    y_p = _merge_out(outs, lses, zp, xp1, wout_b, tm=1024)

    sbias = [_sample_bias(rel_bias[:, g * GROUP_HEADS:(g + 1) * GROUP_HEADS], win, dil,
                          caches[g].shape[1], ts) for g, (win, dil) in enumerate(GROUPS)]
    a_s = _attn_sample(qs, ks, vs, zs, caches, sbias)
    y_s = _matmul_residual(a_s.reshape(bs * ts, gw), xs1.reshape(bs * ts, d), wout_b,
                           tm=bs * ts).reshape(bs, ts, d)

    kv5 = lambda a: a.reshape(a.shape[0], a.shape[1], 2, GROUP_HEADS, ATTN_HEAD_DIM)
    return (y_p, y_s, c_p[None], n_p[None], m_p[None], c_s[None], n_s[None], m_s[None],
            kv5(kv128_p), kv5(kv512_p), kv5(kv2048_p), kv5(kv128_s), kv5(kv512_s), kv5(kv2048_s))
```

```python
import functools
import math

import numpy as np
import jax
import jax.numpy as jnp
from jax import lax
from jax.experimental import pallas as pl
from jax.experimental.pallas import tpu as pltpu

F32 = jnp.float32
BF16 = jnp.bfloat16

EPS = 1e-6
MLSTM_HEADS = 4
GROUPS = ((128, 1), (512, 4), (2048, 16))
GROUP_HEADS = 8
ATTN_HEAD_DIM = 64
ATTN_BLOCK = 128
N_BUCKETS = 32
MAX_DISTANCE = 2048

HEAD_SHIFT = ATTN_HEAD_DIM.bit_length() - 1

LANES = 128
MXU_DIM = 256
VMEM_LIMIT_BYTES = 56 * 1024 * 1024

INPROJ_ROWS = 1024
SCAN_CHUNK = 256
SHORT_SCAN_CHUNK = 16
KVQ_ROWS = 512
MERGE_ROWS = 1024
ATTN_BLOCKS_PER_STEP = 8

NT_DIMS = (((1,), (1,)), ((), ()))


def _params(sem):
    return pltpu.CompilerParams(dimension_semantics=sem, vmem_limit_bytes=VMEM_LIMIT_BYTES)


def _rms_scale(xf):
    return lax.rsqrt(jnp.mean(xf * xf, axis=-1, keepdims=True) + EPS)


def _sigmoid(x):
    return 0.5 * jnp.tanh(0.5 * x) + 0.5


def _split_bf16(a):
    hi = a.astype(BF16)
    lo = (a - hi.astype(F32)).astype(BF16)
    return hi, lo


def _inproj_kernel(*refs, n_heads, k_tile, k_scale, use_colscale, o_tile):
    x_ref, g_ref, wt_ref, cs_ref, wgt_ref, bg_ref, p_ref, gc_ref, gr_ref = refs[:9]
    kt_ref = refs[9] if k_tile is not None else None
    xn_ref, so_ref = refs[-2:]
    z_tile = o_tile + 1
    j = pl.program_id(2)

    @pl.when(j == 0)
    def _():
        xf = x_ref[...]
        xn = xf * _rms_scale(xf) * g_ref[...]
        xh, xl = _split_bf16(xn)
        wh, wl = _split_bf16(wgt_ref[...])
        xn_ref[...] = xh
        gates = (lax.dot_general(xh, wh, NT_DIMS, preferred_element_type=F32)
                 + lax.dot_general(xl, wh, NT_DIMS, preferred_element_type=F32)
                 + lax.dot_general(xh, wl, NT_DIMS, preferred_element_type=F32)) + bg_ref[...]
        lane = lax.broadcasted_iota(jnp.int32, gates.shape, 1)
        logsig = jnp.minimum(gates, 0.0) - jnp.log(1.0 + jnp.exp(-jnp.abs(gates)))
        gcol = jnp.where(lane < n_heads, gates, jnp.where(lane < 2 * n_heads, logsig, 0.0))
        gc_ref[...] = gcol
        gr_ref[...] = gcol.T[:8, :]

    def token_major():
        return lax.dot_general(xn_ref[...], wt_ref[...], NT_DIMS, preferred_element_type=F32)

    plain = (j != o_tile) & (j != z_tile)
    if k_tile is not None:
        plain = plain & (j != k_tile)

        @pl.when(j == k_tile)
        def _():
            acc = lax.dot_general(wt_ref[...], xn_ref[...], NT_DIMS, preferred_element_type=F32)
            kt_ref[...] = (acc * k_scale).astype(BF16)

    @pl.when(plain)
    def _():
        acc = token_major()
        if use_colscale:
            acc = acc * cs_ref[...]
        p_ref[...] = acc.astype(BF16)

    @pl.when(j == o_tile)
    def _():
        so_ref[...] = _sigmoid(token_major()).astype(BF16)

    @pl.when(j == z_tile)
    def _():
        z = token_major()
        p_ref[...] = (so_ref[...].astype(F32) * (z * _sigmoid(z))).astype(BF16)


def _inproj(x3, g, wt, colscale, wgt, bg, *, tm, tn, n_heads, k_tile=None, k_scale=1.0):
    b, t, d = x3.shape
    n = wt.shape[0]
    assert t % tm == 0 and n % tn == 0 and 2 * n_heads <= 8
    nj = n // tn
    o_tile = nj - 2
    assert o_tile >= 1 and (k_tile is None or 1 <= k_tile < o_tile)
    held = (o_tile,) if k_tile is None else (k_tile, o_tile)
    p_tile = lambda j: j - sum((j >= s).astype(jnp.int32) for s in held)
    n_out = n - len(held) * tn
    out_specs = [
        pl.BlockSpec((None, tm, tn), lambda bi, i, j: (bi, i, p_tile(j))),
        pl.BlockSpec((None, tm, LANES), lambda bi, i, j: (bi, i, 0)),
        pl.BlockSpec((None, 8, tm), lambda bi, i, j: (bi, 0, i)),
    ]
    out_shape = [
        jax.ShapeDtypeStruct((b, t, n_out), BF16),
        jax.ShapeDtypeStruct((b, t, LANES), F32),
        jax.ShapeDtypeStruct((b, 8, t), F32),
    ]
    if k_tile is not None:
        out_specs.append(pl.BlockSpec((None, tn, tm), lambda bi, i, j: (bi, 0, i)))
        out_shape.append(jax.ShapeDtypeStruct((b, tn, t), BF16))
    use_colscale = colscale is not None
    if colscale is None:
        colscale = jnp.ones((1, n), F32)
    return pl.pallas_call(
        functools.partial(_inproj_kernel, n_heads=n_heads, k_tile=k_tile, k_scale=k_scale,
                          use_colscale=use_colscale, o_tile=o_tile),
        grid=(b, t // tm, nj),
        in_specs=[
            pl.BlockSpec((None, tm, d), lambda bi, i, j: (bi, i, 0)),
            pl.BlockSpec((1, d), lambda bi, i, j: (0, 0)),
            pl.BlockSpec((tn, d), lambda bi, i, j: (j, 0)),
            pl.BlockSpec((1, tn), lambda bi, i, j: (0, j)),
            pl.BlockSpec((LANES, d), lambda bi, i, j: (0, 0)),
            pl.BlockSpec((1, LANES), lambda bi, i, j: (0, 0)),
        ],
        out_specs=out_specs,
        out_shape=out_shape,
        scratch_shapes=[pltpu.VMEM((tm, d), BF16), pltpu.VMEM((tm, tn), BF16)],
        compiler_params=_params(("parallel", "parallel", "arbitrary")),
        name="inproj",
    )(x3, g, wt, colscale, wgt, bg)


def _scan_kernel(*refs, chunk, n_heads, dh, n_chunks, has_state):
    L, H = chunk, n_heads
    dext = dh + LANES
    (q_ref, kt_ref, v_ref, gate_ref, gc_ref, gr_ref, x_ref, wout_ref, hg_ref) = refs[:9]
    pos = 9
    if has_state:
        c0_ref, n0_ref, m0_ref = refs[pos:pos + 3]
        pos += 3
    xo_ref, cout_ref, nout_ref, mout_ref = refs[pos:pos + 4]
    cext_ref, cb_ref, m_ref = refs[pos + 4:pos + 7]
    c = pl.program_id(1)

    @pl.when(c == 0)
    def _():
        if has_state:
            lane0 = lax.broadcasted_iota(jnp.int32, (dh, LANES), 1) == 0
            for h in range(H):
                ncol = jnp.broadcast_to(n0_ref[h:h + 1, :], (LANES, dh)).T
                cext_ref[h, :, :dh] = c0_ref[h]
                cext_ref[h, :, dh:] = jnp.where(lane0, ncol, 0.0)
            m_ref[...] = m0_ref[...]
        else:
            cext_ref[...] = jnp.zeros(cext_ref.shape, F32)
            m_ref[...] = jnp.zeros(m_ref.shape, F32)
        cb_ref[...] = cext_ref[...].astype(BF16)

    gc = gc_ref[...]
    gr = gr_ref[...]
    row = lax.broadcasted_iota(jnp.int32, (L, L), 0)
    col = lax.broadcasted_iota(jnp.int32, (L, L), 1)
    causal = row >= col
    lane = lax.broadcasted_iota(jnp.int32, gc.shape, 1)
    subl = lax.broadcasted_iota(jnp.int32, gr.shape, 0)
    lf_c_hi, lf_c_lo = _split_bf16(jnp.where((lane >= H) & (lane < 2 * H), gc, 0.0))
    lf_r_hi, lf_r_lo = _split_bf16(jnp.where((subl >= H) & (subl < 2 * H), gr, 0.0))
    lower = causal.astype(BF16)
    upper = (row <= col).astype(BF16)
    bc_all = (jnp.dot(lower, lf_c_hi, preferred_element_type=F32)
              + jnp.dot(lower, lf_c_lo, preferred_element_type=F32))
    br_all = (jnp.dot(lf_r_hi, upper, preferred_element_type=F32)
              + jnp.dot(lf_r_lo, upper, preferred_element_type=F32))
    ones_col = (lax.broadcasted_iota(jnp.int32, (L, LANES), 1) == 0).astype(BF16)
    y = jnp.zeros((L, wout_ref.shape[1]), F32)

    for h in range(H):
        sl = slice(h * dh, (h + 1) * dh)
        q = q_ref[:, sl]
        kt = kt_ref[sl, :]
        vext = jnp.concatenate([v_ref[:, sl], ones_col], axis=1)
        ig_r = gr[h:h + 1, :]
        b_c = bc_all[:, H + h:H + h + 1]
        b_r = br_all[H + h:H + h + 1, :]
        m_prev = m_ref[h:h + 1, 0:1]

        log_d = jnp.where(causal, b_c - b_r + ig_r, -jnp.inf)
        log_inter = b_c + m_prev
        m_t = jnp.maximum(log_inter, jnp.max(log_d, axis=1, keepdims=True))
        dmat = jnp.exp(log_d - m_t)
        inter = jnp.exp(log_inter - m_t)
        s = jnp.dot(q, kt, preferred_element_type=F32) * dmat
        numden = (jnp.dot(s.astype(BF16), vext, preferred_element_type=F32)
                  + inter * jnp.dot(q, cb_ref[h], preferred_element_type=F32))
        num = numden[:, :dh]
        den = numden[:, dh:dh + 1]
        hh = num / jnp.maximum(jnp.abs(den), jnp.exp(-m_t))

        b_last = b_r[:, L - 1:L]
        a_r = b_last - b_r + ig_r
        m_new = jnp.maximum(b_last + m_prev, jnp.max(a_r, axis=1, keepdims=True))
        w_r = jnp.exp(a_r - m_new)
        decay = jnp.exp(b_last + m_prev - m_new)
        ktw = (kt.astype(F32) * w_r).astype(BF16)
        c_new = decay * cext_ref[h] + jnp.dot(ktw, vext, preferred_element_type=F32)
        cext_ref[h] = c_new
        cb_ref[h] = c_new.astype(BF16)
        m_ref[h:h + 1, :] = jnp.broadcast_to(m_new, (1, LANES))

        hn = hh * lax.rsqrt(jnp.mean(hh * hh, axis=1, keepdims=True) + EPS)
        hg = (hn * hg_ref[:, sl] * gate_ref[:, sl].astype(F32)).astype(BF16)
        y = y + jnp.dot(hg, wout_ref[sl, :], preferred_element_type=F32)

    xo_ref[...] = x_ref[...] + y

    @pl.when(c == n_chunks - 1)
    def _():
        for h in range(H):
            cout_ref[h] = cext_ref[h, :, :dh]
            nout_ref[h:h + 1, :] = cext_ref[h, :, dh:].T[0:1, :]
        mout_ref[...] = m_ref[...]


def _scan(p3, col_idx, kt3, gc, gr, x3, wout, hgain, state, *, chunk, n_heads):
    b, t, _ = p3.shape
    dh = kt3.shape[1] // n_heads
    di = n_heads * dh
    d = x3.shape[2]
    nc = t // chunk
    assert t % chunk == 0
    has_state = state is not None
    qi, vi, gi = col_idx

    def pspec(ci):
        return pl.BlockSpec((None, chunk, di), lambda bi, c, ci=ci: (bi, c, ci))

    in_specs = [
        pspec(qi),
        pl.BlockSpec((None, di, chunk), lambda bi, c: (bi, 0, c)),
        pspec(vi), pspec(gi),
        pl.BlockSpec((None, chunk, LANES), lambda bi, c: (bi, c, 0)),
        pl.BlockSpec((None, 8, chunk), lambda bi, c: (bi, 0, c)),
        pl.BlockSpec((None, chunk, d), lambda bi, c: (bi, c, 0)),
        pl.BlockSpec((di, d), lambda bi, c: (0, 0)),
        pl.BlockSpec((1, di), lambda bi, c: (0, 0)),
    ]
    args = [p3, kt3, p3, p3, gc, gr, x3, wout, hgain]
    if has_state:
        c0, n0, m0 = state
        in_specs += [
            pl.BlockSpec((None, n_heads, dh, dh), lambda bi, c: (bi, 0, 0, 0)),
            pl.BlockSpec((None, n_heads, dh), lambda bi, c: (bi, 0, 0)),
            pl.BlockSpec((None, 8, LANES), lambda bi, c: (bi, 0, 0)),
        ]
        args += [c0, n0, m0]
    return pl.pallas_call(
        functools.partial(_scan_kernel, chunk=chunk, n_heads=n_heads, dh=dh,
                          n_chunks=nc, has_state=has_state),
        grid=(b, nc),
        in_specs=in_specs,
        out_specs=[
            pl.BlockSpec((None, chunk, d), lambda bi, c: (bi, c, 0)),
            pl.BlockSpec((None, n_heads, dh, dh), lambda bi, c: (bi, 0, 0, 0)),
            pl.BlockSpec((None, n_heads, dh), lambda bi, c: (bi, 0, 0)),
            pl.BlockSpec((None, 8, LANES), lambda bi, c: (bi, 0, 0)),
        ],
        out_shape=[
            jax.ShapeDtypeStruct((b, t, d), F32),
            jax.ShapeDtypeStruct((b, n_heads, dh, dh), F32),
            jax.ShapeDtypeStruct((b, n_heads, dh), F32),
            jax.ShapeDtypeStruct((b, 8, LANES), F32),
        ],
        scratch_shapes=[
            pltpu.VMEM((n_heads, dh, dh + LANES), F32),
            pltpu.VMEM((n_heads, dh, dh + LANES), BF16),
            pltpu.VMEM((8, LANES), F32),
        ],
        compiler_params=_params(("parallel", "arbitrary")),
        name="mlstm_scan",
    )(*args)


def _headnorm(a):
    n = a.shape[1]
    r = lax.broadcasted_iota(jnp.int32, (MXU_DIM, MXU_DIM), 0)
    c = lax.broadcasted_iota(jnp.int32, (MXU_DIM, MXU_DIM), 1)
    same_head = (lax.shift_right_logical(r, HEAD_SHIFT)
                 == lax.shift_right_logical(c, HEAD_SHIFT)).astype(BF16)
    parts = []
    for c0 in range(0, n, MXU_DIM):
        blk = a[:, c0:c0 + MXU_DIM]
        ss = jnp.dot((blk * blk).astype(BF16), same_head, preferred_element_type=F32)
        parts.append(blk * lax.rsqrt(ss * (1.0 / ATTN_HEAD_DIM) + EPS))
    return jnp.concatenate(parts, axis=1)


def _store_by_residue(ref, val, scr, dil):
    rows, width = val.shape
    if dil == 1:
        ref[0] = val.astype(BF16)
        return
    for c in range(width // LANES):
        scr[c] = val[:, c * LANES:(c + 1) * LANES]
    for r in range(dil):
        parts = [scr[c, pl.ds(r, rows // dil, stride=dil), :] for c in range(width // LANES)]
        ref[r] = jnp.concatenate(parts, axis=1).astype(BF16)


def _proj_headnorm_kernel(x_ref, g_ref, w_ref, hg_ref, a_ref, r_ref, *row_refs, n_norm, scale):
    gw = GROUP_HEADS * ATTN_HEAD_DIM
    xf = x_ref[...]
    xn = (xf * _rms_scale(xf) * g_ref[...]).astype(BF16)
    p = jnp.dot(xn, w_ref[...], preferred_element_type=F32)
    a = _headnorm(p[:, :n_norm]) * hg_ref[...]
    if scale != 1.0:
        a = a * scale
    r = p[:, n_norm:]
    a_ref[...] = a.astype(BF16)
    r_ref[...] = r.astype(BF16)
    for g, row_ref in enumerate(row_refs):
        cs = slice(g * gw, (g + 1) * gw)
        row_ref[:, :gw] = a[:, cs]
        row_ref[:, gw:] = r[:, cs]


def _proj_headnorm(x2, g, w, hgain, *, n_norm, scale, group_rows=False):
    m, d = x2.shape
    n = w.shape[1]
    gw = GROUP_HEADS * ATTN_HEAD_DIM
    assert n_norm % MXU_DIM == 0 and m % 16 == 0
    whole = lambda shape: pl.BlockSpec(shape, lambda i: (0, 0))
    out_specs = [whole((m, n_norm)), whole((m, n - n_norm))]
    out_shape = [jax.ShapeDtypeStruct((m, n_norm), BF16), jax.ShapeDtypeStruct((m, n - n_norm), BF16)]
    if group_rows:
        assert n - n_norm == n_norm
        for _ in range(n_norm // gw):
            out_specs.append(whole((m, 2 * gw)))
            out_shape.append(jax.ShapeDtypeStruct((m, 2 * gw), F32))
    return pl.pallas_call(
        functools.partial(_proj_headnorm_kernel, n_norm=n_norm, scale=scale),
        grid=(1,),
        in_specs=[whole((m, d)), whole((1, d)), whole((d, n)), whole((1, n_norm))],
        out_specs=out_specs,
        out_shape=out_shape,
        compiler_params=_params(("arbitrary",)),
        name="proj_headnorm",
    )(x2, g, w, hgain)


def _kvq_proj_kernel(x_ref, gkv_ref, wkv_ref, kg_ref, gb_ref, wb_ref, qg_ref, *refs,
                     qscale, tails, tm, dils):
    ng = len(dils)
    gw = GROUP_HEADS * ATTN_HEAD_DIM
    n_norm = ng * gw
    k_refs, v_refs, q_refs = refs[:ng], refs[ng:2 * ng], refs[2 * ng:3 * ng]
    z_ref = refs[3 * ng]
    tail_refs = refs[3 * ng + 1:3 * ng + 1 + len(tails)]
    scr = refs[-1]
    xf = x_ref[...]
    xhat = xf * _rms_scale(xf)
    xkv = (xhat * gkv_ref[...]).astype(BF16)
    xb = (xhat * gb_ref[...]).astype(BF16)
    for g, dil in enumerate(dils):
        cs = slice(g * gw, (g + 1) * gw)
        k = _headnorm(jnp.dot(xkv, wkv_ref[:, cs], preferred_element_type=F32)) * kg_ref[:, cs]
        v = jnp.dot(xkv, wkv_ref[:, n_norm + g * gw:n_norm + (g + 1) * gw],
                    preferred_element_type=F32)
        q = (_headnorm(jnp.dot(xb, wb_ref[:, cs], preferred_element_type=F32))
             * qg_ref[:, cs] * qscale)
        _store_by_residue(k_refs[g], k, scr, dil)
        _store_by_residue(v_refs[g], v, scr, dil)
        _store_by_residue(q_refs[g], q, scr, dil)
        rows, t_ref = tails[g][0], tail_refs[g]
        t_ref[:, :gw] = k[max(tm - rows, 0):, :]
        t_ref[:, gw:] = v[max(tm - rows, 0):, :]
    z_ref[...] = jnp.dot(xb, wb_ref[:, n_norm:], preferred_element_type=F32).astype(BF16)


def _kvq_proj(x3, gkv, wkv, kgain, gb, wb, qgain, *, qscale, tm, tail_rows, dils):
    b, t, d = x3.shape
    gw = GROUP_HEADS * ATTN_HEAD_DIM
    ng = len(dils)
    assert t % tm == 0 and wkv.shape[1] == 2 * ng * gw and wb.shape[1] > ng * gw
    n_tiles = t // tm
    out_specs, out_shape = [], []
    for _ in range(3):
        for dil in dils:
            assert tm % (dil * 16) == 0
            out_specs.append(pl.BlockSpec((None, dil, tm // dil, gw), lambda bi, i: (bi, 0, i, 0)))
            out_shape.append(jax.ShapeDtypeStruct((b, dil, t // dil, gw), BF16))
    zw = wb.shape[1] - ng * gw
    out_specs.append(pl.BlockSpec((None, tm, zw), lambda bi, i: (bi, i, 0)))
    out_shape.append(jax.ShapeDtypeStruct((b, t, zw), BF16))
    tails = []
    for rows in tail_rows:
        if rows >= tm:
            assert rows % tm == 0
            first, blk = n_tiles - rows // tm, tm
        else:
            first, blk = n_tiles - 1, rows
        tails.append((rows, first))
        out_specs.append(pl.BlockSpec(
            (None, blk, 2 * gw), lambda bi, i, first=first: (bi, jnp.maximum(i - first, 0), 0)))
        out_shape.append(jax.ShapeDtypeStruct((b, rows, 2 * gw), F32))
    const = lambda shape: pl.BlockSpec(shape, lambda bi, i: (0, 0))
    outs = pl.pallas_call(
        functools.partial(_kvq_proj_kernel, qscale=qscale, tails=tuple(tails), tm=tm, dils=dils),
        grid=(b, n_tiles),
        in_specs=[
            pl.BlockSpec((None, tm, d), lambda bi, i: (bi, i, 0)),
            const((1, d)), const(wkv.shape), const((1, ng * gw)),
            const((1, d)), const(wb.shape), const((1, ng * gw)),
        ],
        out_specs=out_specs,
        out_shape=out_shape,
        scratch_shapes=[pltpu.VMEM((gw // LANES, tm, LANES), F32)],
        compiler_params=_params(("parallel", "arbitrary")),
        name="kvq_proj",
    )(x3, gkv, wkv, kgain, gb, wb, qgain)
    return outs[:ng], outs[ng:2 * ng], outs[2 * ng:3 * ng], outs[3 * ng], outs[3 * ng + 1:]


def _attn_prompt_kernel(q_ref, kp_ref, kc_ref, vp_ref, vc_ref, bias_ref, o_ref, lse_ref, *, nq):
    blk = ATTN_BLOCK
    first = jnp.where(pl.program_id(2) == 0, 0, 1)
    lane = lax.broadcasted_iota(jnp.int32, (blk, LANES), 1)
    low_half = lane < ATTN_HEAD_DIM
    ones_tile = jnp.ones((2 * blk, LANES), BF16)
    for rr, jb in [(rr, jb) for rr in range(q_ref.shape[0]) for jb in range(nq)]:
        q = q_ref[rr, jb * blk:(jb + 1) * blk, :]
        if jb == 0:
            kcat = jnp.concatenate([kp_ref[rr], kc_ref[rr, 0:blk, :]], axis=0)
            vcat = jnp.concatenate([vp_ref[rr], vc_ref[rr, 0:blk, :]], axis=0)
        else:
            kcat = kc_ref[rr, (jb - 1) * blk:(jb + 1) * blk, :]
            vcat = vc_ref[rr, (jb - 1) * blk:(jb + 1) * blk, :]
        ms, outs = [], []
        sums = jnp.ones((blk, LANES), F32)
        for j in range(GROUP_HEADS // 2):
            ps = slice(j * LANES, (j + 1) * LANES)
            qf = q[:, ps].astype(F32)
            kpair = kcat[:, ps]
            vext = jnp.concatenate([vcat[:, ps], ones_tile], axis=1)
            pair = []
            for half in range(2):
                h = 2 * j + half
                qh = jnp.where(low_half if half == 0 else ~low_half, qf, 0.0).astype(BF16)
                bias = bias_ref[first, h] if jb == 0 else bias_ref[1, h]
                st = lax.dot_general(kpair, qh, NT_DIMS, preferred_element_type=F32) + bias
                m = jnp.max(st, axis=0, keepdims=True)
                p = jnp.exp(st - m).T.astype(BF16)
                pv = jnp.dot(p, vext, preferred_element_type=F32)
                l = pv[:, LANES:]
                pair.append(pv[:, :LANES] * (1.0 / l))
                sums = jnp.where(lane == h, l, sums)
                ms.append(m)
            outs.append(jnp.where(low_half, pair[0], pair[1]))
        o_ref[rr, jb * blk:(jb + 1) * blk, :] = jnp.concatenate(outs, axis=1).astype(o_ref.dtype)
        m_t = jnp.concatenate(ms + [jnp.zeros((LANES - GROUP_HEADS, blk), F32)], axis=0).T
        lse_ref[rr, jb * blk:(jb + 1) * blk, :] = m_t + jnp.log(sums)


def _attn_prompt(q, k, v, bias, g):
    b, dil, s, gw = q.shape
    nb = s // ATTN_BLOCK
    assert s % ATTN_BLOCK == 0
    nq = min(ATTN_BLOCKS_PER_STEP, nb)
    rps = min(ATTN_BLOCKS_PER_STEP // nq, dil)
    assert nb % nq == 0 and dil % rps == 0
    rows = nq * ATTN_BLOCK
    cur = pl.BlockSpec((None, rps, rows, gw), lambda bi, r, j: (bi, r, j, 0))
    prev = pl.BlockSpec((None, rps, ATTN_BLOCK, gw),
                        lambda bi, r, j: (bi, r, jnp.maximum(j * nq - 1, 0), 0))
    return pl.pallas_call(
        functools.partial(_attn_prompt_kernel, nq=nq),
        grid=(b, dil // rps, nb // nq),
        in_specs=[
            cur, prev, cur, prev, cur,
            pl.BlockSpec(bias.shape, lambda bi, r, j: (0, 0, 0, 0)),
        ],
        out_specs=[
            pl.BlockSpec((None, rps, rows, gw), lambda bi, r, j: (bi, r, j, 0)),
            pl.BlockSpec((None, rps, rows, LANES), lambda bi, r, j: (bi, r, j, 0)),
        ],
        out_shape=[
            jax.ShapeDtypeStruct((b, dil, s, gw), BF16),
            jax.ShapeDtypeStruct((b, dil, s, LANES), F32),
        ],
        compiler_params=_params(("parallel", "parallel", "arbitrary")),
        name="attn_prompt_g%d" % g,
    )(q, k, k, v, v, bias)


def _head_expand_matrix():
    r = lax.broadcasted_iota(jnp.int32, (LANES, GROUP_HEADS * ATTN_HEAD_DIM), 0)
    c = lax.broadcasted_iota(jnp.int32, (LANES, GROUP_HEADS * ATTN_HEAD_DIM), 1)
    return (r == lax.shift_right_logical(c, HEAD_SHIFT)).astype(BF16)


def _load_token_order(ref, scr):
    dil, per, width = ref.shape
    if dil == 1:
        return ref[0].astype(F32)
    n_tiles = width // LANES
    for r in range(dil):
        val = ref[r].astype(F32)
        for c in range(n_tiles):
            scr[c, pl.ds(r, per, stride=dil), :] = val[:, c * LANES:(c + 1) * LANES]
    return jnp.concatenate([scr[c] for c in range(n_tiles)], axis=1)


def _merge_out_kernel(o0_ref, o1_ref, o2_ref, l0_ref, l1_ref, l2_ref, z_ref, x_ref, w_ref, y_ref,
                      *scratch):
    o_refs, l_refs = (o0_ref, o1_ref, o2_ref), (l0_ref, l1_ref, l2_ref)
    os_, ls, k = [], [], 0
    for o_ref, l_ref in zip(o_refs, l_refs):
        if o_ref.shape[0] == 1:
            os_.append(o_ref[0].astype(F32))
            ls.append(l_ref[0])
        else:
            os_.append(_load_token_order(o_ref, scratch[k]))
            ls.append(_load_token_order(l_ref, scratch[k + 1]))
            k += 2
    lmax = jnp.maximum(jnp.maximum(ls[0], ls[1]), ls[2])
    es = [jnp.exp(l - lmax) for l in ls]
    tot = es[0] + es[1] + es[2]
    expand = _head_expand_matrix()
    o = jnp.zeros(os_[0].shape, F32)
    for e, og in zip(es, os_):
        hi, lo = _split_bf16(e / tot)
        wexp = (jnp.dot(hi, expand, preferred_element_type=F32)
                + jnp.dot(lo, expand, preferred_element_type=F32))
        o = o + wexp * og
    zf = z_ref[...].astype(F32)
    a = (o * (zf * _sigmoid(zf))).astype(BF16)
    y_ref[...] = x_ref[...] + jnp.dot(a, w_ref[...], preferred_element_type=F32)


def _merge_out(outs, lses, z3, x3, w, *, tm):
    b, t, d = x3.shape
    gw = w.shape[0]
    assert t % tm == 0
    row = lambda width: pl.BlockSpec((None, tm, width), lambda bi, i: (bi, i, 0))
    by_residue = lambda a: pl.BlockSpec((None, a.shape[1], tm // a.shape[1], a.shape[3]),
                                        lambda bi, i: (bi, 0, i, 0))
    scratch = []
    for o in outs:
        if o.shape[1] > 1:
            scratch += [pltpu.VMEM((gw // LANES, tm, LANES), F32), pltpu.VMEM((1, tm, LANES), F32)]
    return pl.pallas_call(
        _merge_out_kernel,
        grid=(b, t // tm),
        in_specs=[by_residue(o) for o in outs] + [by_residue(l) for l in lses]
        + [row(gw), row(d), pl.BlockSpec((gw, d), lambda bi, i: (0, 0))],
        out_specs=row(d),
        out_shape=jax.ShapeDtypeStruct((b, t, d), F32),
        scratch_shapes=scratch,
        compiler_params=_params(("parallel", "parallel")),
        name="merge_out",
    )(*outs, *lses, z3, x3, w)


def _attn_sample_kernel(q_ref, kn_ref, vn_ref, z_ref, c0_ref, c1_ref, c2_ref,
                        b0_ref, b1_ref, b2_ref, a_ref, ks0, vs0, ks1, vs1, ks2, vs2, *, s_new):
    gw = GROUP_HEADS * ATTN_HEAD_DIM
    rows = GROUP_HEADS * s_new
    r = lax.broadcasted_iota(jnp.int32, (rows, gw), 0)
    c = lax.broadcasted_iota(jnp.int32, (rows, gw), 1)
    head_mask = (lax.shift_right_logical(r, int(math.log2(s_new)))
                 == lax.shift_right_logical(c, HEAD_SHIFT))
    caches = ((c0_ref, b0_ref, ks0, vs0), (c1_ref, b1_ref, ks1, vs1), (c2_ref, b2_ref, ks2, vs2))
    pad_rows = jnp.zeros((LANES - s_new, gw), F32)
    outs, lses = [], []
    for g, (c_ref, b_ref, ks, vs) in enumerate(caches):
        buf_len = c_ref.shape[2]
        cs = slice(g * gw, (g + 1) * gw)
        ks[:, :buf_len] = c_ref[0].astype(BF16)
        vs[:, :buf_len] = c_ref[1].astype(BF16)
        ks[:, buf_len:] = jnp.concatenate([kn_ref[:, cs].astype(F32), pad_rows], axis=0).T.astype(BF16)
        vs[:, buf_len:] = jnp.concatenate([vn_ref[:, cs].astype(F32), pad_rows], axis=0).T.astype(BF16)
        qg = q_ref[:, cs].astype(F32)
        qbd = jnp.where(head_mask, jnp.concatenate([qg] * GROUP_HEADS, axis=0), 0.0).astype(BF16)
        s = jnp.dot(qbd, ks[...], preferred_element_type=F32) + b_ref[...]
        m = jnp.max(s, axis=1, keepdims=True)
        p = jnp.exp(s - m)
        l = jnp.sum(p, axis=1, keepdims=True)
        outs.append(lax.dot_general(p.astype(BF16), vs[...], NT_DIMS, preferred_element_type=F32) / l)
        lses.append(m + jnp.log(l))
    lmax = jnp.maximum(jnp.maximum(lses[0], lses[1]), lses[2])
    es = [jnp.exp(l - lmax) for l in lses]
    tot = es[0] + es[1] + es[2]
    o = jnp.zeros((rows, gw), F32)
    for e, og in zip(es, outs):
        o = o + (e / tot) * og
    o = jnp.where(head_mask, o, 0.0)
    folded = o[0:s_new, :]
    for h in range(1, GROUP_HEADS):
        folded = folded + o[h * s_new:(h + 1) * s_new, :]
    zf = z_ref[...].astype(F32)
    a_ref[...] = (folded * (zf * _sigmoid(zf))).astype(BF16)


def _attn_sample(q, kn, vn, z, caches, biases):
    b, s_new, qw = q.shape
    gw = GROUP_HEADS * ATTN_HEAD_DIM
    assert s_new % 8 == 0
    cache2 = [jnp.transpose(cb, (0, 2, 3, 4, 1)).reshape(b, 2, gw, cb.shape[1]) for cb in caches]
    in_specs = [
        pl.BlockSpec((None, s_new, qw), lambda bi: (bi, 0, 0)),
        pl.BlockSpec((None, s_new, qw), lambda bi: (bi, 0, 0)),
        pl.BlockSpec((None, s_new, qw), lambda bi: (bi, 0, 0)),
        pl.BlockSpec((None, s_new, gw), lambda bi: (bi, 0, 0)),
    ]
    for cb in cache2:
        in_specs.append(pl.BlockSpec((None, 2, gw, cb.shape[3]), lambda bi: (bi, 0, 0, 0)))
    for bt in biases:
        in_specs.append(pl.BlockSpec(bt.shape, lambda bi: (0, 0)))
    scratch = []
    for cb in cache2:
        scratch += [pltpu.VMEM((gw, cb.shape[3] + LANES), BF16)] * 2
    return pl.pallas_call(
        functools.partial(_attn_sample_kernel, s_new=s_new),
        grid=(b,),
        in_specs=in_specs,
        out_specs=pl.BlockSpec((None, s_new, gw), lambda bi: (bi, 0, 0)),
        out_shape=jax.ShapeDtypeStruct((b, s_new, gw), BF16),
        scratch_shapes=scratch,
        compiler_params=_params(("arbitrary",)),
        name="attn_sample",
    )(q, kn, vn, z, *cache2, *biases)


def _matmul_residual_kernel(a_ref, x_ref, w_ref, y_ref):
    y_ref[...] = x_ref[...] + jnp.dot(a_ref[...], w_ref[...], preferred_element_type=F32)


def _matmul_residual(a2, x2, w, *, tm):
    m, d = x2.shape
    kdim = a2.shape[1]
    assert m % tm == 0
    return pl.pallas_call(
        _matmul_residual_kernel,
        grid=(m // tm,),
        in_specs=[
            pl.BlockSpec((tm, kdim), lambda i: (i, 0)),
            pl.BlockSpec((tm, d), lambda i: (i, 0)),
            pl.BlockSpec((kdim, d), lambda i: (0, 0)),
        ],
        out_specs=pl.BlockSpec((tm, d), lambda i: (i, 0)),
        out_shape=jax.ShapeDtypeStruct((m, d), F32),
        compiler_params=_params(("parallel",)),
        name="matmul_residual",
    )(a2, x2, w)


def _t5_bucket_np(dist):
    exact = N_BUCKETS // 2
    d = np.maximum(dist, 1).astype(np.float32)
    large = exact + (np.log(d / np.float32(exact)) / np.float32(math.log(MAX_DISTANCE / exact))
                     * np.float32(N_BUCKETS - exact)).astype(np.int32)
    return np.where(dist < exact, dist, np.minimum(large, N_BUCKETS - 1)).astype(np.int32)


def _bias_by_step(rel_bias_g, jmax, dil):
    return rel_bias_g.astype(F32)[_t5_bucket_np(np.arange(jmax + 1) * dil)]


def _prompt_bias(rel_bias_g, win, dil):
    jmax = win // dil
    assert jmax == ATTN_BLOCK
    qi = np.arange(ATTN_BLOCK)[:, None]
    kj = np.arange(2 * ATTN_BLOCK)[None, :]
    rel = qi + ATTN_BLOCK - kj
    band = (rel >= 0) & (rel <= jmax)
    bvec = _bias_by_step(rel_bias_g, jmax, dil)
    period = 2 * ATTN_BLOCK + 1
    base = jnp.concatenate([bvec[::-1], jnp.zeros((period - jmax - 1, GROUP_HEADS), F32)], axis=0).T
    bias = jnp.tile(base, (1, ATTN_BLOCK))[:, :2 * ATTN_BLOCK * ATTN_BLOCK]
    bias = bias.reshape(GROUP_HEADS, ATTN_BLOCK, 2 * ATTN_BLOCK)
    rest = jnp.where(band[None], bias, -jnp.inf)
    first = jnp.where((band & (kj >= ATTN_BLOCK))[None], bias, -jnp.inf)
    return jnp.swapaxes(jnp.stack([first, rest], axis=0), 2, 3)


def _sample_bias(rel_bias_g, win, dil, buf_len, s_new):
    jmax = win // dil
    assert buf_len == jmax * dil
    width = buf_len + LANES
    bvec = _bias_by_step(rel_bias_g, jmax, dil)
    gaps = jnp.full((jmax + 1, dil - 1, GROUP_HEADS), -jnp.inf, F32)
    by_dist = jnp.concatenate([bvec[:, None, :], gaps], axis=1).reshape((jmax + 1) * dil, GROUP_HEADS)
    padded = jnp.pad(by_dist[::-1], ((s_new, width), (0, 0)), constant_values=-jnp.inf)
    rows = [padded[dil - 1 - s + s_new:dil - 1 - s + s_new + width] for s in range(s_new)]
    table = jnp.transpose(jnp.stack(rows, axis=0), (2, 0, 1))
    return table.reshape(GROUP_HEADS * s_new, width)


def _layer_a(x3, state, weights, *, chunk, pad_to):
    norm_a, wt, wgt, bg, hgain, wout = weights
    b, t, d = x3.shape
    H = MLSTM_HEADS
    di = wout.shape[0]
    dh = di // H
    k_scale = dh ** -0.5
    k_idx = 1
    if pad_to == t:
        p3, gc3, gr3, kt3 = _inproj(x3, norm_a, wt, None, wgt, bg, tm=min(INPROJ_ROWS, t), tn=di,
                                    n_heads=H, k_tile=k_idx, k_scale=k_scale)
        col_idx = (0, 1, 2)
        xin = x3
    else:
        m = b * t
        extra = pad_to - t
        colscale = jnp.concatenate([jnp.ones((di,), F32), jnp.full((di,), k_scale, F32),
                                    jnp.ones((3 * di,), F32)])[None, :]
        p, gc, gr = _inproj(x3.reshape(1, m, d), norm_a, wt, colscale, wgt, bg,
                            tm=m, tn=di, n_heads=H)
        col_idx = (0, 2, 3)
        p3 = p.reshape(b, t, -1)
        k3 = p3[:, :, k_idx * di:(k_idx + 1) * di]
        kt3 = jnp.pad(jnp.swapaxes(k3, 1, 2), ((0, 0), (0, 0), (0, extra)))
        p3 = jnp.pad(p3, ((0, 0), (0, extra), (0, 0)))
        xin = jnp.pad(x3, ((0, 0), (0, extra), (0, 0)))
        lane = np.arange(LANES)
        pad_col = np.where(lane < H, -np.inf, 0.0).astype(np.float32)
        gc3 = jnp.concatenate([gc.reshape(b, t, LANES),
                               jnp.broadcast_to(pad_col, (b, extra, LANES))], axis=1)
        pad_row = np.where(np.arange(8) < H, -np.inf, 0.0).astype(np.float32)[:, None, None]
        gr3 = jnp.swapaxes(jnp.concatenate([gr.reshape(8, b, t),
                                            jnp.broadcast_to(pad_row, (8, b, extra))], axis=2), 0, 1)
    xo, c_out, n_out, m_out = _scan(p3, col_idx, kt3, gc3, gr3, xin, wout, hgain, state,
                                    chunk=chunk, n_heads=H)
    return xo[:, :t], c_out, n_out, m_out[:, :H, 0]


def kernel(x_prompt, x_sample, state_mlstm_C, state_mlstm_n, state_mlstm_m, cache_kv_w128, cache_kv_w512, cache_kv_w2048, norm_a, w_in_a, b_gates_a, hnorm_a, w_out_a, norm_kv, w_kv, k_norm, norm_b, w_in_b, q_norm, rel_bias, w_out_b):
    H = MLSTM_HEADS
    bp, tp, d = x_prompt.shape
    bs, ts, _ = x_sample.shape
    di = w_out_a.shape[1]
    dh = di // H
    gw = GROUP_HEADS * ATTN_HEAD_DIM
    qw = len(GROUPS) * gw
    caches = (cache_kv_w128, cache_kv_w512, cache_kv_w2048)
    assert norm_a.shape[0] == 1 and norm_b.shape[0] == 1, "one mLSTM layer, one attention layer"
    for cb, (win, _) in zip(caches, GROUPS):
        assert cb.shape[1] == win, "window buffers must hold a full window"

    w_at = w_in_a[0].T
    wt_a = w_at[:5 * di].astype(BF16)
    wgt = jnp.pad(w_at[5 * di:].astype(F32), ((0, LANES - 2 * H), (0, 0)))
    bg = jnp.pad(b_gates_a[0].astype(F32), (0, LANES - 2 * H))[None, :]
    na = norm_a[0].astype(F32)[None, :]
    hgain = hnorm_a[0].astype(F32)[None, :]
    wout_a = w_out_a[0].astype(BF16)
    weights_a = (na, wt_a, wgt, bg, hgain, wout_a)

    xp1, c_p, n_p, m_p = _layer_a(x_prompt, None, weights_a, chunk=SCAN_CHUNK, pad_to=tp)
    m0 = jnp.pad(jnp.broadcast_to(state_mlstm_m[0].astype(F32)[:, :, None], (bs, H, LANES)),
                 ((0, 0), (0, 8 - H), (0, 0)))
    state_s = (state_mlstm_C[0].astype(F32), state_mlstm_n[0].astype(F32), m0)
    assert ts <= SHORT_SCAN_CHUNK
    xs1, c_s, n_s, m_s = _layer_a(x_sample, state_s, weights_a, chunk=SHORT_SCAN_CHUNK,
                                  pad_to=SHORT_SCAN_CHUNK)

    nkv = norm_kv.astype(F32)[None, :]
    wkv = w_kv.astype(BF16)
    kgain = jnp.tile(k_norm.astype(F32), qw // ATTN_HEAD_DIM)[None, :]
    rows_p = [min(win, tp) for win, _ in GROUPS]
    dils = tuple(dil for _, dil in GROUPS)
    nb_ = norm_b[0].astype(F32)[None, :]
    wb = w_in_b[0].astype(BF16)
    qgain = jnp.tile(q_norm[0].astype(F32), qw // ATTN_HEAD_DIM)[None, :]
    qscale = ATTN_HEAD_DIM ** -0.5
    kp, vp, qp, zp, (kv128_p, kv512_p, kv2048_p) = _kvq_proj(
        xp1, nkv, wkv, kgain, nb_, wb, qgain, qscale=qscale, tm=KVQ_ROWS, tail_rows=rows_p,
        dils=dils)
    xs1_flat = xs1.reshape(bs * ts, d)
    by_seq = lambda a: a.reshape(bs, ts, a.shape[-1])
    ks, vs, kv128_s, kv512_s, kv2048_s = map(by_seq, _proj_headnorm(
        xs1_flat, nkv, wkv, kgain, n_norm=qw, scale=1.0, group_rows=True))

    wout_b = w_out_b[0].astype(BF16)
    qs, zs = map(by_seq, _proj_headnorm(xs1_flat, nb_, wb, qgain, n_norm=qw, scale=qscale))

    outs, lses = [], []
    for g, (win, dil) in enumerate(GROUPS):
        bias = _prompt_bias(rel_bias[:, g * GROUP_HEADS:(g + 1) * GROUP_HEADS], win, dil)
        o, lse = _attn_prompt(qp[g], kp[g], vp[g], bias, g)
        outs.append(o)
        lses.append(lse)
    y_p = _merge_out(outs, lses, zp, xp1, wout_b, tm=MERGE_ROWS)

    sbias = [_sample_bias(rel_bias[:, g * GROUP_HEADS:(g + 1) * GROUP_HEADS], win, dil,
                          caches[g].shape[1], ts) for g, (win, dil) in enumerate(GROUPS)]
    a_s = _attn_sample(qs, ks, vs, zs, caches, sbias)
    y_s = _matmul_residual(a_s.reshape(bs * ts, gw), xs1.reshape(bs * ts, d), wout_b,
                           tm=bs * ts).reshape(bs, ts, d)

    kv5 = lambda a: a.reshape(a.shape[0], a.shape[1], 2, GROUP_HEADS, ATTN_HEAD_DIM)
    return (y_p, y_s, c_p[None], n_p[None], m_p[None], c_s[None], n_s[None], m_s[None],
            kv5(kv128_p), kv5(kv512_p), kv5(kv2048_p), kv5(kv128_s), kv5(kv512_s), kv5(kv2048_s))
```

```python
import functools
import math

import numpy as np
import jax
import jax.numpy as jnp
from jax import lax
from jax.experimental import pallas as pl
from jax.experimental.pallas import tpu as pltpu

F32 = jnp.float32
BF16 = jnp.bfloat16

EPS = 1e-6
MLSTM_HEADS = 4
GROUPS = ((128, 1), (512, 4), (2048, 16))
GROUP_HEADS = 8
ATTN_HEAD_DIM = 64
ATTN_BLOCK = 128
N_BUCKETS = 32
MAX_DISTANCE = 2048

HEAD_SHIFT = ATTN_HEAD_DIM.bit_length() - 1

LANES = 128
MXU_DIM = 256
VMEM_LIMIT_BYTES = 56 * 1024 * 1024

INPROJ_ROWS = 512
SCAN_CHUNK = 256
SHORT_SCAN_CHUNK = 16
KVQ_ROWS = 512
MERGE_ROWS = 1024
ATTN_BLOCKS_PER_STEP = 8

NT_DIMS = (((1,), (1,)), ((), ()))


def _params(sem):
    return pltpu.CompilerParams(dimension_semantics=sem, vmem_limit_bytes=VMEM_LIMIT_BYTES)


def _rms_scale(xf):
    return lax.rsqrt(jnp.mean(xf * xf, axis=-1, keepdims=True) + EPS)


def _sigmoid(x):
    return 0.5 * jnp.tanh(0.5 * x) + 0.5


def _split_bf16(a):
    hi = a.astype(BF16)
    lo = (a - hi.astype(F32)).astype(BF16)
    return hi, lo


def _inproj_kernel(x_ref, g_ref, wt_ref, wgt_ref, bg_ref, p_ref, gc_ref, gr_ref, *kt_refs,
                   n_heads, tn, k_scale):
    xf = x_ref[...]
    xn = xf * _rms_scale(xf) * g_ref[...]
    xh, xl = _split_bf16(xn)
    wh, wl = _split_bf16(wgt_ref[...])
    gates = (lax.dot_general(xh, wh, NT_DIMS, preferred_element_type=F32)
             + lax.dot_general(xl, wh, NT_DIMS, preferred_element_type=F32)
             + lax.dot_general(xh, wl, NT_DIMS, preferred_element_type=F32)) + bg_ref[...]
    lane = lax.broadcasted_iota(jnp.int32, gates.shape, 1)
    logsig = jnp.minimum(gates, 0.0) - jnp.log(1.0 + jnp.exp(-jnp.abs(gates)))
    gcol = jnp.where(lane < n_heads, gates, jnp.where(lane < 2 * n_heads, logsig, 0.0))
    gc_ref[...] = gcol
    gr_ref[...] = gcol.T[:8, :]

    def tile(j):
        return lax.dot_general(xh, wt_ref[j * tn:(j + 1) * tn, :], NT_DIMS,
                               preferred_element_type=F32)

    q_tile, k_tile, v_tile, o_tile, z_tile = range(5)
    p_ref[:, 0:tn] = tile(q_tile).astype(BF16)
    if kt_refs:
        acc = lax.dot_general(wt_ref[k_tile * tn:(k_tile + 1) * tn, :], xh, NT_DIMS,
                              preferred_element_type=F32)
        kt_refs[0][...] = (acc * k_scale).astype(BF16)
        col = tn
    else:
        p_ref[:, tn:2 * tn] = (tile(k_tile) * k_scale).astype(BF16)
        col = 2 * tn
    p_ref[:, col:col + tn] = tile(v_tile).astype(BF16)
    z = tile(z_tile)
    p_ref[:, col + tn:col + 2 * tn] = (_sigmoid(tile(o_tile)) * (z * _sigmoid(z))).astype(BF16)


def _inproj(x3, g, wt, wgt, bg, *, tm, tn, n_heads, k_transposed, k_scale):
    b, t, d = x3.shape
    assert t % tm == 0 and wt.shape[0] == 5 * tn and 2 * n_heads <= 8
    n_out = (3 if k_transposed else 4) * tn
    out_specs = [
        pl.BlockSpec((None, tm, n_out), lambda bi, i: (bi, i, 0)),
        pl.BlockSpec((None, tm, LANES), lambda bi, i: (bi, i, 0)),
        pl.BlockSpec((None, 8, tm), lambda bi, i: (bi, 0, i)),
    ]
    out_shape = [
        jax.ShapeDtypeStruct((b, t, n_out), BF16),
        jax.ShapeDtypeStruct((b, t, LANES), F32),
        jax.ShapeDtypeStruct((b, 8, t), F32),
    ]
    if k_transposed:
        out_specs.append(pl.BlockSpec((None, tn, tm), lambda bi, i: (bi, 0, i)))
        out_shape.append(jax.ShapeDtypeStruct((b, tn, t), BF16))
    const = lambda shape, **kw: pl.BlockSpec(shape, lambda bi, i: (0, 0), **kw)
    return pl.pallas_call(
        functools.partial(_inproj_kernel, n_heads=n_heads, tn=tn, k_scale=k_scale),
        grid=(b, t // tm),
        in_specs=[
            pl.BlockSpec((None, tm, d), lambda bi, i: (bi, i, 0)),
            const((1, d)),
            const(wt.shape, pipeline_mode=pl.Buffered(1)),
            const((LANES, d)),
            const((1, LANES)),
        ],
        out_specs=out_specs,
        out_shape=out_shape,
        compiler_params=_params(("parallel", "parallel")),
        name="inproj",
    )(x3, g, wt, wgt, bg)


def _scan_kernel(*refs, chunk, n_heads, dh, n_chunks, has_state):
    L, H = chunk, n_heads
    dext = dh + LANES
    (q_ref, kt_ref, v_ref, gate_ref, gc_ref, gr_ref, x_ref, wout_ref, hg_ref) = refs[:9]
    pos = 9
    if has_state:
        c0_ref, n0_ref, m0_ref = refs[pos:pos + 3]
        pos += 3
    xo_ref, cout_ref, nout_ref, mout_ref = refs[pos:pos + 4]
    cext_ref, cb_ref, m_ref = refs[pos + 4:pos + 7]
    c = pl.program_id(1)

    @pl.when(c == 0)
    def _():
        if has_state:
            lane0 = lax.broadcasted_iota(jnp.int32, (dh, LANES), 1) == 0
            for h in range(H):
                ncol = jnp.broadcast_to(n0_ref[h:h + 1, :], (LANES, dh)).T
                cext_ref[h, :, :dh] = c0_ref[h]
                cext_ref[h, :, dh:] = jnp.where(lane0, ncol, 0.0)
            m_ref[...] = m0_ref[...]
        else:
            cext_ref[...] = jnp.zeros(cext_ref.shape, F32)
            m_ref[...] = jnp.zeros(m_ref.shape, F32)
        cb_ref[...] = cext_ref[...].astype(BF16)

    gc = gc_ref[...]
    gr = gr_ref[...]
    row = lax.broadcasted_iota(jnp.int32, (L, L), 0)
    col = lax.broadcasted_iota(jnp.int32, (L, L), 1)
    causal = row >= col
    lane = lax.broadcasted_iota(jnp.int32, gc.shape, 1)
    subl = lax.broadcasted_iota(jnp.int32, gr.shape, 0)
    lf_c_hi, lf_c_lo = _split_bf16(jnp.where((lane >= H) & (lane < 2 * H), gc, 0.0))
    lf_r_hi, lf_r_lo = _split_bf16(jnp.where((subl >= H) & (subl < 2 * H), gr, 0.0))
    lower = causal.astype(BF16)
    upper = (row <= col).astype(BF16)
    bc_all = (jnp.dot(lower, lf_c_hi, preferred_element_type=F32)
              + jnp.dot(lower, lf_c_lo, preferred_element_type=F32))
    br_all = (jnp.dot(lf_r_hi, upper, preferred_element_type=F32)
              + jnp.dot(lf_r_lo, upper, preferred_element_type=F32))
    ones_col = (lax.broadcasted_iota(jnp.int32, (L, LANES), 1) == 0).astype(BF16)
    y = jnp.zeros((L, wout_ref.shape[1]), F32)

    for h in range(H):
        sl = slice(h * dh, (h + 1) * dh)
        q = q_ref[:, sl]
        kt = kt_ref[sl, :]
        vext = jnp.concatenate([v_ref[:, sl], ones_col], axis=1)
        ig_r = gr[h:h + 1, :]
        b_c = bc_all[:, H + h:H + h + 1]
        b_r = br_all[H + h:H + h + 1, :]
        m_prev = m_ref[h:h + 1, 0:1]

        log_d = jnp.where(causal, b_c - b_r + ig_r, -jnp.inf)
        log_inter = b_c + m_prev
        m_t = jnp.maximum(log_inter, jnp.max(log_d, axis=1, keepdims=True))
        dmat = jnp.exp(log_d - m_t)
        inter = jnp.exp(log_inter - m_t)
        s = jnp.dot(q, kt, preferred_element_type=F32) * dmat
        numden = (jnp.dot(s.astype(BF16), vext, preferred_element_type=F32)
                  + inter * jnp.dot(q, cb_ref[h], preferred_element_type=F32))
        num = numden[:, :dh]
        den = numden[:, dh:dh + 1]
        hh = num / jnp.maximum(jnp.abs(den), jnp.exp(-m_t))

        b_last = b_r[:, L - 1:L]
        a_r = b_last - b_r + ig_r
        m_new = jnp.maximum(b_last + m_prev, jnp.max(a_r, axis=1, keepdims=True))
        w_r = jnp.exp(a_r - m_new)
        decay = jnp.exp(b_last + m_prev - m_new)
        ktw = (kt.astype(F32) * w_r).astype(BF16)
        c_new = decay * cext_ref[h] + jnp.dot(ktw, vext, preferred_element_type=F32)
        cext_ref[h] = c_new
        cb_ref[h] = c_new.astype(BF16)
        m_ref[h:h + 1, :] = jnp.broadcast_to(m_new, (1, LANES))

        hn = hh * lax.rsqrt(jnp.mean(hh * hh, axis=1, keepdims=True) + EPS)
        hg = (hn * hg_ref[:, sl] * gate_ref[:, sl].astype(F32)).astype(BF16)
        y = y + jnp.dot(hg, wout_ref[sl, :], preferred_element_type=F32)

    xo_ref[...] = x_ref[...] + y

    @pl.when(c == n_chunks - 1)
    def _():
        for h in range(H):
            cout_ref[h] = cext_ref[h, :, :dh]
            nout_ref[h:h + 1, :] = cext_ref[h, :, dh:].T[0:1, :]
        mout_ref[...] = m_ref[...]


def _scan(p3, col_idx, kt3, gc, gr, x3, wout, hgain, state, *, chunk, n_heads):
    b, t, _ = p3.shape
    dh = kt3.shape[1] // n_heads
    di = n_heads * dh
    d = x3.shape[2]
    nc = t // chunk
    assert t % chunk == 0
    has_state = state is not None
    qi, vi, gi = col_idx

    def pspec(ci):
        return pl.BlockSpec((None, chunk, di), lambda bi, c, ci=ci: (bi, c, ci))

    in_specs = [
        pspec(qi),
        pl.BlockSpec((None, di, chunk), lambda bi, c: (bi, 0, c)),
        pspec(vi), pspec(gi),
        pl.BlockSpec((None, chunk, LANES), lambda bi, c: (bi, c, 0)),
        pl.BlockSpec((None, 8, chunk), lambda bi, c: (bi, 0, c)),
        pl.BlockSpec((None, chunk, d), lambda bi, c: (bi, c, 0)),
        pl.BlockSpec((di, d), lambda bi, c: (0, 0)),
        pl.BlockSpec((1, di), lambda bi, c: (0, 0)),
    ]
    args = [p3, kt3, p3, p3, gc, gr, x3, wout, hgain]
    if has_state:
        c0, n0, m0 = state
        in_specs += [
            pl.BlockSpec((None, n_heads, dh, dh), lambda bi, c: (bi, 0, 0, 0)),
            pl.BlockSpec((None, n_heads, dh), lambda bi, c: (bi, 0, 0)),
            pl.BlockSpec((None, 8, LANES), lambda bi, c: (bi, 0, 0)),
        ]
        args += [c0, n0, m0]
    return pl.pallas_call(
        functools.partial(_scan_kernel, chunk=chunk, n_heads=n_heads, dh=dh,
                          n_chunks=nc, has_state=has_state),
        grid=(b, nc),
        in_specs=in_specs,
        out_specs=[
            pl.BlockSpec((None, chunk, d), lambda bi, c: (bi, c, 0)),
            pl.BlockSpec((None, n_heads, dh, dh), lambda bi, c: (bi, 0, 0, 0)),
            pl.BlockSpec((None, n_heads, dh), lambda bi, c: (bi, 0, 0)),
            pl.BlockSpec((None, 8, LANES), lambda bi, c: (bi, 0, 0)),
        ],
        out_shape=[
            jax.ShapeDtypeStruct((b, t, d), F32),
            jax.ShapeDtypeStruct((b, n_heads, dh, dh), F32),
            jax.ShapeDtypeStruct((b, n_heads, dh), F32),
            jax.ShapeDtypeStruct((b, 8, LANES), F32),
        ],
        scratch_shapes=[
            pltpu.VMEM((n_heads, dh, dh + LANES), F32),
            pltpu.VMEM((n_heads, dh, dh + LANES), BF16),
            pltpu.VMEM((8, LANES), F32),
        ],
        compiler_params=_params(("parallel", "arbitrary")),
        name="mlstm_scan",
    )(*args)


def _headnorm(a):
    n = a.shape[1]
    r = lax.broadcasted_iota(jnp.int32, (MXU_DIM, MXU_DIM), 0)
    c = lax.broadcasted_iota(jnp.int32, (MXU_DIM, MXU_DIM), 1)
    same_head = (lax.shift_right_logical(r, HEAD_SHIFT)
                 == lax.shift_right_logical(c, HEAD_SHIFT)).astype(BF16)
    parts = []
    for c0 in range(0, n, MXU_DIM):
        blk = a[:, c0:c0 + MXU_DIM]
        ss = jnp.dot((blk * blk).astype(BF16), same_head, preferred_element_type=F32)
        parts.append(blk * lax.rsqrt(ss * (1.0 / ATTN_HEAD_DIM) + EPS))
    return jnp.concatenate(parts, axis=1)


def _store_by_residue(ref, val, scr, dil):
    rows, width = val.shape
    if dil == 1:
        ref[0] = val.astype(BF16)
        return
    for c in range(width // LANES):
        scr[c] = val[:, c * LANES:(c + 1) * LANES]
    for r in range(dil):
        parts = [scr[c, pl.ds(r, rows // dil, stride=dil), :] for c in range(width // LANES)]
        ref[r] = jnp.concatenate(parts, axis=1).astype(BF16)


def _proj_headnorm_kernel(x_ref, g_ref, w_ref, hg_ref, a_ref, r_ref, *row_refs, n_norm, scale):
    gw = GROUP_HEADS * ATTN_HEAD_DIM
    xf = x_ref[...]
    xn = (xf * _rms_scale(xf) * g_ref[...]).astype(BF16)
    p = jnp.dot(xn, w_ref[...], preferred_element_type=F32)
    a = _headnorm(p[:, :n_norm]) * hg_ref[...]
    if scale != 1.0:
        a = a * scale
    r = p[:, n_norm:]
    a_ref[...] = a.astype(BF16)
    r_ref[...] = r.astype(BF16)
    for g, row_ref in enumerate(row_refs):
        cs = slice(g * gw, (g + 1) * gw)
        row_ref[:, :gw] = a[:, cs]
        row_ref[:, gw:] = r[:, cs]


def _proj_headnorm(x2, g, w, hgain, *, n_norm, scale, group_rows=False):
    m, d = x2.shape
    n = w.shape[1]
    gw = GROUP_HEADS * ATTN_HEAD_DIM
    assert n_norm % MXU_DIM == 0 and m % 16 == 0
    whole = lambda shape: pl.BlockSpec(shape, lambda i: (0, 0))
    out_specs = [whole((m, n_norm)), whole((m, n - n_norm))]
    out_shape = [jax.ShapeDtypeStruct((m, n_norm), BF16), jax.ShapeDtypeStruct((m, n - n_norm), BF16)]
    if group_rows:
        assert n - n_norm == n_norm
        for _ in range(n_norm // gw):
            out_specs.append(whole((m, 2 * gw)))
            out_shape.append(jax.ShapeDtypeStruct((m, 2 * gw), F32))
    return pl.pallas_call(
        functools.partial(_proj_headnorm_kernel, n_norm=n_norm, scale=scale),
        grid=(1,),
        in_specs=[whole((m, d)), whole((1, d)), whole((d, n)), whole((1, n_norm))],
        out_specs=out_specs,
        out_shape=out_shape,
        compiler_params=_params(("arbitrary",)),
        name="proj_headnorm",
    )(x2, g, w, hgain)


def _kvq_proj_kernel(x_ref, gkv_ref, wkv_ref, kg_ref, gb_ref, wb_ref, qg_ref, *refs,
                     qscale, tails, tm, dils):
    ng = len(dils)
    gw = GROUP_HEADS * ATTN_HEAD_DIM
    n_norm = ng * gw
    k_refs, v_refs, q_refs = refs[:ng], refs[ng:2 * ng], refs[2 * ng:3 * ng]
    z_ref = refs[3 * ng]
    tail_refs = refs[3 * ng + 1:3 * ng + 1 + len(tails)]
    scr = refs[-1]
    xf = x_ref[...]
    xhat = xf * _rms_scale(xf)
    xkv = (xhat * gkv_ref[...]).astype(BF16)
    xb = (xhat * gb_ref[...]).astype(BF16)
    for g, dil in enumerate(dils):
        cs = slice(g * gw, (g + 1) * gw)
        k = _headnorm(jnp.dot(xkv, wkv_ref[:, cs], preferred_element_type=F32)) * kg_ref[:, cs]
        v = jnp.dot(xkv, wkv_ref[:, n_norm + g * gw:n_norm + (g + 1) * gw],
                    preferred_element_type=F32)
        q = (_headnorm(jnp.dot(xb, wb_ref[:, cs], preferred_element_type=F32))
             * qg_ref[:, cs] * qscale)
        _store_by_residue(k_refs[g], k, scr, dil)
        _store_by_residue(v_refs[g], v, scr, dil)
        _store_by_residue(q_refs[g], q, scr, dil)
        rows, t_ref = tails[g][0], tail_refs[g]
        t_ref[:, :gw] = k[max(tm - rows, 0):, :]
        t_ref[:, gw:] = v[max(tm - rows, 0):, :]
    z_ref[...] = jnp.dot(xb, wb_ref[:, n_norm:], preferred_element_type=F32).astype(BF16)


def _kvq_proj(x3, gkv, wkv, kgain, gb, wb, qgain, *, qscale, tm, tail_rows, dils):
    b, t, d = x3.shape
    gw = GROUP_HEADS * ATTN_HEAD_DIM
    ng = len(dils)
    assert t % tm == 0 and wkv.shape[1] == 2 * ng * gw and wb.shape[1] > ng * gw
    n_tiles = t // tm
    out_specs, out_shape = [], []
    for _ in range(3):
        for dil in dils:
            assert tm % (dil * 16) == 0
            out_specs.append(pl.BlockSpec((None, dil, tm // dil, gw), lambda bi, i: (bi, 0, i, 0)))
            out_shape.append(jax.ShapeDtypeStruct((b, dil, t // dil, gw), BF16))
    zw = wb.shape[1] - ng * gw
    out_specs.append(pl.BlockSpec((None, tm, zw), lambda bi, i: (bi, i, 0)))
    out_shape.append(jax.ShapeDtypeStruct((b, t, zw), BF16))
    tails = []
    for rows in tail_rows:
        if rows >= tm:
            assert rows % tm == 0
            first, blk = n_tiles - rows // tm, tm
        else:
            first, blk = n_tiles - 1, rows
        tails.append((rows, first))
        out_specs.append(pl.BlockSpec(
            (None, blk, 2 * gw), lambda bi, i, first=first: (bi, jnp.maximum(i - first, 0), 0)))
        out_shape.append(jax.ShapeDtypeStruct((b, rows, 2 * gw), F32))
    const = lambda shape: pl.BlockSpec(shape, lambda bi, i: (0, 0))
    outs = pl.pallas_call(
        functools.partial(_kvq_proj_kernel, qscale=qscale, tails=tuple(tails), tm=tm, dils=dils),
        grid=(b, n_tiles),
        in_specs=[
            pl.BlockSpec((None, tm, d), lambda bi, i: (bi, i, 0)),
            const((1, d)), const(wkv.shape), const((1, ng * gw)),
            const((1, d)), const(wb.shape), const((1, ng * gw)),
        ],
        out_specs=out_specs,
        out_shape=out_shape,
        scratch_shapes=[pltpu.VMEM((gw // LANES, tm, LANES), F32)],
        compiler_params=_params(("parallel", "arbitrary")),
        name="kvq_proj",
    )(x3, gkv, wkv, kgain, gb, wb, qgain)
    return outs[:ng], outs[ng:2 * ng], outs[2 * ng:3 * ng], outs[3 * ng], outs[3 * ng + 1:]


def _attn_prompt_kernel(q_ref, kp_ref, kc_ref, vp_ref, vc_ref, bias_ref, o_ref, lse_ref, *, nq):
    blk = ATTN_BLOCK
    first = jnp.where(pl.program_id(2) == 0, 0, 1)
    lane = lax.broadcasted_iota(jnp.int32, (blk, LANES), 1)
    low_half = lane < ATTN_HEAD_DIM
    ones_tile = jnp.ones((2 * blk, LANES), BF16)
    for rr, jb in [(rr, jb) for rr in range(q_ref.shape[0]) for jb in range(nq)]:
        q = q_ref[rr, jb * blk:(jb + 1) * blk, :]
        if jb == 0:
            kcat = jnp.concatenate([kp_ref[rr], kc_ref[rr, 0:blk, :]], axis=0)
            vcat = jnp.concatenate([vp_ref[rr], vc_ref[rr, 0:blk, :]], axis=0)
        else:
            kcat = kc_ref[rr, (jb - 1) * blk:(jb + 1) * blk, :]
            vcat = vc_ref[rr, (jb - 1) * blk:(jb + 1) * blk, :]
        ms, outs = [], []
        sums = jnp.ones((blk, LANES), F32)
        for j in range(GROUP_HEADS // 2):
            ps = slice(j * LANES, (j + 1) * LANES)
            qf = q[:, ps].astype(F32)
            kpair = kcat[:, ps]
            vext = jnp.concatenate([vcat[:, ps], ones_tile], axis=1)
            pair = []
            for half in range(2):
                h = 2 * j + half
                qh = jnp.where(low_half if half == 0 else ~low_half, qf, 0.0).astype(BF16)
                bias = bias_ref[first, h] if jb == 0 else bias_ref[1, h]
                st = lax.dot_general(kpair, qh, NT_DIMS, preferred_element_type=F32) + bias
                m = jnp.max(st, axis=0, keepdims=True)
                p = jnp.exp(st - m).T.astype(BF16)
                pv = jnp.dot(p, vext, preferred_element_type=F32)
                l = pv[:, LANES:]
                pair.append(pv[:, :LANES] * (1.0 / l))
                sums = jnp.where(lane == h, l, sums)
                ms.append(m)
            outs.append(jnp.where(low_half, pair[0], pair[1]))
        o_ref[rr, jb * blk:(jb + 1) * blk, :] = jnp.concatenate(outs, axis=1).astype(o_ref.dtype)
        m_t = jnp.concatenate(ms + [jnp.zeros((LANES - GROUP_HEADS, blk), F32)], axis=0).T
        lse_ref[rr, jb * blk:(jb + 1) * blk, :] = m_t + jnp.log(sums)


def _attn_prompt(q, k, v, bias, g):
    b, dil, s, gw = q.shape
    nb = s // ATTN_BLOCK
    assert s % ATTN_BLOCK == 0
    nq = min(ATTN_BLOCKS_PER_STEP, nb)
    rps = min(ATTN_BLOCKS_PER_STEP // nq, dil)
    assert nb % nq == 0 and dil % rps == 0
    rows = nq * ATTN_BLOCK
    cur = pl.BlockSpec((None, rps, rows, gw), lambda bi, r, j: (bi, r, j, 0))
    prev = pl.BlockSpec((None, rps, ATTN_BLOCK, gw),
                        lambda bi, r, j: (bi, r, jnp.maximum(j * nq - 1, 0), 0))
    return pl.pallas_call(
        functools.partial(_attn_prompt_kernel, nq=nq),
        grid=(b, dil // rps, nb // nq),
        in_specs=[
            cur, prev, cur, prev, cur,
            pl.BlockSpec(bias.shape, lambda bi, r, j: (0, 0, 0, 0)),
        ],
        out_specs=[
            pl.BlockSpec((None, rps, rows, gw), lambda bi, r, j: (bi, r, j, 0)),
            pl.BlockSpec((None, rps, rows, LANES), lambda bi, r, j: (bi, r, j, 0)),
        ],
        out_shape=[
            jax.ShapeDtypeStruct((b, dil, s, gw), BF16),
            jax.ShapeDtypeStruct((b, dil, s, LANES), F32),
        ],
        compiler_params=_params(("parallel", "parallel", "arbitrary")),
        name="attn_prompt_g%d" % g,
    )(q, k, k, v, v, bias)


def _head_expand_matrix():
    r = lax.broadcasted_iota(jnp.int32, (LANES, GROUP_HEADS * ATTN_HEAD_DIM), 0)
    c = lax.broadcasted_iota(jnp.int32, (LANES, GROUP_HEADS * ATTN_HEAD_DIM), 1)
    return (r == lax.shift_right_logical(c, HEAD_SHIFT)).astype(BF16)


def _load_token_order(ref, scr):
    dil, per, width = ref.shape
    if dil == 1:
        return ref[0].astype(F32)
    n_tiles = width // LANES
    for r in range(dil):
        val = ref[r].astype(F32)
        for c in range(n_tiles):
            scr[c, pl.ds(r, per, stride=dil), :] = val[:, c * LANES:(c + 1) * LANES]
    return jnp.concatenate([scr[c] for c in range(n_tiles)], axis=1)


def _merge_out_kernel(o0_ref, o1_ref, o2_ref, l0_ref, l1_ref, l2_ref, z_ref, x_ref, w_ref, y_ref,
                      *scratch):
    o_refs, l_refs = (o0_ref, o1_ref, o2_ref), (l0_ref, l1_ref, l2_ref)
    os_, ls, k = [], [], 0
    for o_ref, l_ref in zip(o_refs, l_refs):
        if o_ref.shape[0] == 1:
            os_.append(o_ref[0].astype(F32))
            ls.append(l_ref[0])
        else:
            os_.append(_load_token_order(o_ref, scratch[k]))
            ls.append(_load_token_order(l_ref, scratch[k + 1]))
            k += 2
    lmax = jnp.maximum(jnp.maximum(ls[0], ls[1]), ls[2])
    es = [jnp.exp(l - lmax) for l in ls]
    tot = es[0] + es[1] + es[2]
    expand = _head_expand_matrix()
    o = jnp.zeros(os_[0].shape, F32)
    for e, og in zip(es, os_):
        hi, lo = _split_bf16(e / tot)
        wexp = (jnp.dot(hi, expand, preferred_element_type=F32)
                + jnp.dot(lo, expand, preferred_element_type=F32))
        o = o + wexp * og
    zf = z_ref[...].astype(F32)
    a = (o * (zf * _sigmoid(zf))).astype(BF16)
    y_ref[...] = x_ref[...] + jnp.dot(a, w_ref[...], preferred_element_type=F32)


def _merge_out(outs, lses, z3, x3, w, *, tm):
    b, t, d = x3.shape
    gw = w.shape[0]
    assert t % tm == 0
    row = lambda width: pl.BlockSpec((None, tm, width), lambda bi, i: (bi, i, 0))
    by_residue = lambda a: pl.BlockSpec((None, a.shape[1], tm // a.shape[1], a.shape[3]),
                                        lambda bi, i: (bi, 0, i, 0))
    scratch = []
    for o in outs:
        if o.shape[1] > 1:
            scratch += [pltpu.VMEM((gw // LANES, tm, LANES), F32), pltpu.VMEM((1, tm, LANES), F32)]
    return pl.pallas_call(
        _merge_out_kernel,
        grid=(b, t // tm),
        in_specs=[by_residue(o) for o in outs] + [by_residue(l) for l in lses]
        + [row(gw), row(d), pl.BlockSpec((gw, d), lambda bi, i: (0, 0))],
        out_specs=row(d),
        out_shape=jax.ShapeDtypeStruct((b, t, d), F32),
        scratch_shapes=scratch,
        compiler_params=_params(("parallel", "parallel")),
        name="merge_out",
    )(*outs, *lses, z3, x3, w)


def _attn_sample_kernel(q_ref, kn_ref, vn_ref, z_ref, c0_ref, c1_ref, c2_ref,
                        b0_ref, b1_ref, b2_ref, a_ref, ks0, vs0, ks1, vs1, ks2, vs2, *, s_new):
    gw = GROUP_HEADS * ATTN_HEAD_DIM
    rows = GROUP_HEADS * s_new
    r = lax.broadcasted_iota(jnp.int32, (rows, gw), 0)
    c = lax.broadcasted_iota(jnp.int32, (rows, gw), 1)
    head_mask = (lax.shift_right_logical(r, int(math.log2(s_new)))
                 == lax.shift_right_logical(c, HEAD_SHIFT))
    caches = ((c0_ref, b0_ref, ks0, vs0), (c1_ref, b1_ref, ks1, vs1), (c2_ref, b2_ref, ks2, vs2))
    pad_rows = jnp.zeros((LANES - s_new, gw), F32)
    outs, lses = [], []
    for g, (c_ref, b_ref, ks, vs) in enumerate(caches):
        buf_len = c_ref.shape[2]
        cs = slice(g * gw, (g + 1) * gw)
        ks[:, :buf_len] = c_ref[0].astype(BF16)
        vs[:, :buf_len] = c_ref[1].astype(BF16)
        ks[:, buf_len:] = jnp.concatenate([kn_ref[:, cs].astype(F32), pad_rows], axis=0).T.astype(BF16)
        vs[:, buf_len:] = jnp.concatenate([vn_ref[:, cs].astype(F32), pad_rows], axis=0).T.astype(BF16)
        qg = q_ref[:, cs].astype(F32)
        qbd = jnp.where(head_mask, jnp.concatenate([qg] * GROUP_HEADS, axis=0), 0.0).astype(BF16)
        s = jnp.dot(qbd, ks[...], preferred_element_type=F32) + b_ref[...]
        m = jnp.max(s, axis=1, keepdims=True)
        p = jnp.exp(s - m)
        l = jnp.sum(p, axis=1, keepdims=True)
        outs.append(lax.dot_general(p.astype(BF16), vs[...], NT_DIMS, preferred_element_type=F32) / l)
        lses.append(m + jnp.log(l))
    lmax = jnp.maximum(jnp.maximum(lses[0], lses[1]), lses[2])
    es = [jnp.exp(l - lmax) for l in lses]
    tot = es[0] + es[1] + es[2]
    o = jnp.zeros((rows, gw), F32)
    for e, og in zip(es, outs):
        o = o + (e / tot) * og
    o = jnp.where(head_mask, o, 0.0)
    folded = o[0:s_new, :]
    for h in range(1, GROUP_HEADS):
        folded = folded + o[h * s_new:(h + 1) * s_new, :]
    zf = z_ref[...].astype(F32)
    a_ref[...] = (folded * (zf * _sigmoid(zf))).astype(BF16)


def _attn_sample(q, kn, vn, z, caches, biases):
    b, s_new, qw = q.shape
    gw = GROUP_HEADS * ATTN_HEAD_DIM
    assert s_new % 8 == 0
    cache2 = [jnp.transpose(cb, (0, 2, 3, 4, 1)).reshape(b, 2, gw, cb.shape[1]) for cb in caches]
    in_specs = [
        pl.BlockSpec((None, s_new, qw), lambda bi: (bi, 0, 0)),
        pl.BlockSpec((None, s_new, qw), lambda bi: (bi, 0, 0)),
        pl.BlockSpec((None, s_new, qw), lambda bi: (bi, 0, 0)),
        pl.BlockSpec((None, s_new, gw), lambda bi: (bi, 0, 0)),
    ]
    for cb in cache2:
        in_specs.append(pl.BlockSpec((None, 2, gw, cb.shape[3]), lambda bi: (bi, 0, 0, 0)))
    for bt in biases:
        in_specs.append(pl.BlockSpec(bt.shape, lambda bi: (0, 0)))
    scratch = []
    for cb in cache2:
        scratch += [pltpu.VMEM((gw, cb.shape[3] + LANES), BF16)] * 2
    return pl.pallas_call(
        functools.partial(_attn_sample_kernel, s_new=s_new),
        grid=(b,),
        in_specs=in_specs,
        out_specs=pl.BlockSpec((None, s_new, gw), lambda bi: (bi, 0, 0)),
        out_shape=jax.ShapeDtypeStruct((b, s_new, gw), BF16),
        scratch_shapes=scratch,
        compiler_params=_params(("arbitrary",)),
        name="attn_sample",
    )(q, kn, vn, z, *cache2, *biases)


def _matmul_residual_kernel(a_ref, x_ref, w_ref, y_ref):
    y_ref[...] = x_ref[...] + jnp.dot(a_ref[...], w_ref[...], preferred_element_type=F32)


def _matmul_residual(a2, x2, w, *, tm):
    m, d = x2.shape
    kdim = a2.shape[1]
    assert m % tm == 0
    return pl.pallas_call(
        _matmul_residual_kernel,
        grid=(m // tm,),
        in_specs=[
            pl.BlockSpec((tm, kdim), lambda i: (i, 0)),
            pl.BlockSpec((tm, d), lambda i: (i, 0)),
            pl.BlockSpec((kdim, d), lambda i: (0, 0)),
        ],
        out_specs=pl.BlockSpec((tm, d), lambda i: (i, 0)),
        out_shape=jax.ShapeDtypeStruct((m, d), F32),
        compiler_params=_params(("parallel",)),
        name="matmul_residual",
    )(a2, x2, w)


def _t5_bucket_np(dist):
    exact = N_BUCKETS // 2
    d = np.maximum(dist, 1).astype(np.float32)
    large = exact + (np.log(d / np.float32(exact)) / np.float32(math.log(MAX_DISTANCE / exact))
                     * np.float32(N_BUCKETS - exact)).astype(np.int32)
    return np.where(dist < exact, dist, np.minimum(large, N_BUCKETS - 1)).astype(np.int32)


def _bias_by_step(rel_bias_g, jmax, dil):
    return rel_bias_g.astype(F32)[_t5_bucket_np(np.arange(jmax + 1) * dil)]


def _prompt_bias(rel_bias_g, win, dil):
    jmax = win // dil
    assert jmax == ATTN_BLOCK
    qi = np.arange(ATTN_BLOCK)[:, None]
    kj = np.arange(2 * ATTN_BLOCK)[None, :]
    rel = qi + ATTN_BLOCK - kj
    band = (rel >= 0) & (rel <= jmax)
    bvec = _bias_by_step(rel_bias_g, jmax, dil)
    period = 2 * ATTN_BLOCK + 1
    base = jnp.concatenate([bvec[::-1], jnp.zeros((period - jmax - 1, GROUP_HEADS), F32)], axis=0).T
    bias = jnp.tile(base, (1, ATTN_BLOCK))[:, :2 * ATTN_BLOCK * ATTN_BLOCK]
    bias = bias.reshape(GROUP_HEADS, ATTN_BLOCK, 2 * ATTN_BLOCK)
    rest = jnp.where(band[None], bias, -jnp.inf)
    first = jnp.where((band & (kj >= ATTN_BLOCK))[None], bias, -jnp.inf)
    return jnp.swapaxes(jnp.stack([first, rest], axis=0), 2, 3)


def _sample_bias(rel_bias_g, win, dil, buf_len, s_new):
    jmax = win // dil
    assert buf_len == jmax * dil
    width = buf_len + LANES
    bvec = _bias_by_step(rel_bias_g, jmax, dil)
    gaps = jnp.full((jmax + 1, dil - 1, GROUP_HEADS), -jnp.inf, F32)
    by_dist = jnp.concatenate([bvec[:, None, :], gaps], axis=1).reshape((jmax + 1) * dil, GROUP_HEADS)
    padded = jnp.pad(by_dist[::-1], ((s_new, width), (0, 0)), constant_values=-jnp.inf)
    rows = [padded[dil - 1 - s + s_new:dil - 1 - s + s_new + width] for s in range(s_new)]
    table = jnp.transpose(jnp.stack(rows, axis=0), (2, 0, 1))
    return table.reshape(GROUP_HEADS * s_new, width)


def _layer_a(x3, state, weights, *, chunk, pad_to):
    norm_a, wt, wgt, bg, hgain, wout = weights
    b, t, d = x3.shape
    H = MLSTM_HEADS
    di = wout.shape[0]
    dh = di // H
    k_scale = dh ** -0.5
    k_idx = 1
    if pad_to == t:
        p3, gc3, gr3, kt3 = _inproj(x3, norm_a, wt, wgt, bg, tm=min(INPROJ_ROWS, t), tn=di,
                                    n_heads=H, k_transposed=True, k_scale=k_scale)
        col_idx = (0, 1, 2)
        xin = x3
    else:
        m = b * t
        extra = pad_to - t
        p, gc, gr = _inproj(x3.reshape(1, m, d), norm_a, wt, wgt, bg, tm=m, tn=di, n_heads=H,
                            k_transposed=False, k_scale=k_scale)
        col_idx = (0, 2, 3)
        p3 = p.reshape(b, t, -1)
        k3 = p3[:, :, k_idx * di:(k_idx + 1) * di]
        kt3 = jnp.pad(jnp.swapaxes(k3, 1, 2), ((0, 0), (0, 0), (0, extra)))
        p3 = jnp.pad(p3, ((0, 0), (0, extra), (0, 0)))
        xin = jnp.pad(x3, ((0, 0), (0, extra), (0, 0)))
        lane = np.arange(LANES)
        pad_col = np.where(lane < H, -np.inf, 0.0).astype(np.float32)
        gc3 = jnp.concatenate([gc.reshape(b, t, LANES),
                               jnp.broadcast_to(pad_col, (b, extra, LANES))], axis=1)
        pad_row = np.where(np.arange(8) < H, -np.inf, 0.0).astype(np.float32)[:, None, None]
        gr3 = jnp.swapaxes(jnp.concatenate([gr.reshape(8, b, t),
                                            jnp.broadcast_to(pad_row, (8, b, extra))], axis=2), 0, 1)
    xo, c_out, n_out, m_out = _scan(p3, col_idx, kt3, gc3, gr3, xin, wout, hgain, state,
                                    chunk=chunk, n_heads=H)
    return xo[:, :t], c_out, n_out, m_out[:, :H, 0]


def kernel(x_prompt, x_sample, state_mlstm_C, state_mlstm_n, state_mlstm_m, cache_kv_w128, cache_kv_w512, cache_kv_w2048, norm_a, w_in_a, b_gates_a, hnorm_a, w_out_a, norm_kv, w_kv, k_norm, norm_b, w_in_b, q_norm, rel_bias, w_out_b):
    H = MLSTM_HEADS
    bp, tp, d = x_prompt.shape
    bs, ts, _ = x_sample.shape
    di = w_out_a.shape[1]
    dh = di // H
    gw = GROUP_HEADS * ATTN_HEAD_DIM
    qw = len(GROUPS) * gw
    caches = (cache_kv_w128, cache_kv_w512, cache_kv_w2048)
    assert norm_a.shape[0] == 1 and norm_b.shape[0] == 1, "one mLSTM layer, one attention layer"
    for cb, (win, _) in zip(caches, GROUPS):
        assert cb.shape[1] == win, "window buffers must hold a full window"

    w_at = w_in_a[0].T
    wt_a = w_at[:5 * di].astype(BF16)
    wgt = jnp.pad(w_at[5 * di:].astype(F32), ((0, LANES - 2 * H), (0, 0)))
    bg = jnp.pad(b_gates_a[0].astype(F32), (0, LANES - 2 * H))[None, :]
    na = norm_a[0].astype(F32)[None, :]
    hgain = hnorm_a[0].astype(F32)[None, :]
    wout_a = w_out_a[0].astype(BF16)
    weights_a = (na, wt_a, wgt, bg, hgain, wout_a)

    xp1, c_p, n_p, m_p = _layer_a(x_prompt, None, weights_a, chunk=SCAN_CHUNK, pad_to=tp)
    m0 = jnp.pad(jnp.broadcast_to(state_mlstm_m[0].astype(F32)[:, :, None], (bs, H, LANES)),
                 ((0, 0), (0, 8 - H), (0, 0)))
    state_s = (state_mlstm_C[0].astype(F32), state_mlstm_n[0].astype(F32), m0)
    assert ts <= SHORT_SCAN_CHUNK
    xs1, c_s, n_s, m_s = _layer_a(x_sample, state_s, weights_a, chunk=SHORT_SCAN_CHUNK,
                                  pad_to=SHORT_SCAN_CHUNK)

    nkv = norm_kv.astype(F32)[None, :]
    wkv = w_kv.astype(BF16)
    kgain = jnp.tile(k_norm.astype(F32), qw // ATTN_HEAD_DIM)[None, :]
    rows_p = [min(win, tp) for win, _ in GROUPS]
    dils = tuple(dil for _, dil in GROUPS)
    nb_ = norm_b[0].astype(F32)[None, :]
    wb = w_in_b[0].astype(BF16)
    qgain = jnp.tile(q_norm[0].astype(F32), qw // ATTN_HEAD_DIM)[None, :]
    qscale = ATTN_HEAD_DIM ** -0.5
    kp, vp, qp, zp, (kv128_p, kv512_p, kv2048_p) = _kvq_proj(
        xp1, nkv, wkv, kgain, nb_, wb, qgain, qscale=qscale, tm=KVQ_ROWS, tail_rows=rows_p,
        dils=dils)
    xs1_flat = xs1.reshape(bs * ts, d)
    by_seq = lambda a: a.reshape(bs, ts, a.shape[-1])
    ks, vs, kv128_s, kv512_s, kv2048_s = map(by_seq, _proj_headnorm(
        xs1_flat, nkv, wkv, kgain, n_norm=qw, scale=1.0, group_rows=True))

    wout_b = w_out_b[0].astype(BF16)
    qs, zs = map(by_seq, _proj_headnorm(xs1_flat, nb_, wb, qgain, n_norm=qw, scale=qscale))

    outs, lses = [], []
    for g, (win, dil) in enumerate(GROUPS):
        bias = _prompt_bias(rel_bias[:, g * GROUP_HEADS:(g + 1) * GROUP_HEADS], win, dil)
        o, lse = _attn_prompt(qp[g], kp[g], vp[g], bias, g)
        outs.append(o)
        lses.append(lse)
    y_p = _merge_out(outs, lses, zp, xp1, wout_b, tm=MERGE_ROWS)

    sbias = [_sample_bias(rel_bias[:, g * GROUP_HEADS:(g + 1) * GROUP_HEADS], win, dil,
                          caches[g].shape[1], ts) for g, (win, dil) in enumerate(GROUPS)]
    a_s = _attn_sample(qs, ks, vs, zs, caches, sbias)
    y_s = _matmul_residual(a_s.reshape(bs * ts, gw), xs1.reshape(bs * ts, d), wout_b,
                           tm=bs * ts).reshape(bs, ts, d)

    kv5 = lambda a: a.reshape(a.shape[0], a.shape[1], 2, GROUP_HEADS, ATTN_HEAD_DIM)
    return (y_p, y_s, c_p[None], n_p[None], m_p[None], c_s[None], n_s[None], m_s[None],
            kv5(kv128_p), kv5(kv512_p), kv5(kv2048_p), kv5(kv128_s), kv5(kv512_s), kv5(kv2048_s))
```

```python
import functools
import math

import numpy as np
import jax
import jax.numpy as jnp
from jax import lax
from jax.experimental import pallas as pl
from jax.experimental.pallas import tpu as pltpu

F32 = jnp.float32
BF16 = jnp.bfloat16

EPS = 1e-6
MLSTM_HEADS = 4
GROUPS = ((128, 1), (512, 4), (2048, 16))
GROUP_HEADS = 8
ATTN_HEAD_DIM = 64
ATTN_BLOCK = 128
N_BUCKETS = 32
MAX_DISTANCE = 2048

HEAD_SHIFT = ATTN_HEAD_DIM.bit_length() - 1

LANES = 128
MXU_DIM = 256
VMEM_LIMIT_BYTES = 56 * 1024 * 1024

INPROJ_ROWS = 512
SCAN_CHUNK = 256
SHORT_SCAN_CHUNK = 8
KVQ_ROWS = 512
MERGE_ROWS = 1024
ATTN_BLOCKS_PER_STEP = 8

NT_DIMS = (((1,), (1,)), ((), ()))


def _params(sem):
    return pltpu.CompilerParams(dimension_semantics=sem, vmem_limit_bytes=VMEM_LIMIT_BYTES)


def _rms_scale(xf):
    return lax.rsqrt(jnp.mean(xf * xf, axis=-1, keepdims=True) + EPS)


def _sigmoid(x):
    return 0.5 * jnp.tanh(0.5 * x) + 0.5


def _split_bf16(a):
    hi = a.astype(BF16)
    lo = (a - hi.astype(F32)).astype(BF16)
    return hi, lo


def _inproj_kernel(x_ref, g_ref, wt_ref, wgt_ref, bg_ref, p_ref, gc_ref, gr_ref, *kt_refs,
                   n_heads, tn, k_scale):
    xf = x_ref[...]
    xn = xf * _rms_scale(xf) * g_ref[...]
    xh, xl = _split_bf16(xn)
    wh, wl = _split_bf16(wgt_ref[...])
    gates = (lax.dot_general(xh, wh, NT_DIMS, preferred_element_type=F32)
             + lax.dot_general(xl, wh, NT_DIMS, preferred_element_type=F32)
             + lax.dot_general(xh, wl, NT_DIMS, preferred_element_type=F32)) + bg_ref[...]
    lane = lax.broadcasted_iota(jnp.int32, gates.shape, 1)
    logsig = jnp.minimum(gates, 0.0) - jnp.log(1.0 + jnp.exp(-jnp.abs(gates)))
    gcol = jnp.where(lane < n_heads, gates, jnp.where(lane < 2 * n_heads, logsig, 0.0))
    gc_ref[...] = gcol
    gr_ref[...] = gcol.T[:8, :]

    def tile(j):
        return lax.dot_general(xh, wt_ref[j * tn:(j + 1) * tn, :], NT_DIMS,
                               preferred_element_type=F32)

    q_tile, k_tile, v_tile, o_tile, z_tile = range(5)
    p_ref[:, 0:tn] = tile(q_tile).astype(BF16)
    if kt_refs:
        acc = lax.dot_general(wt_ref[k_tile * tn:(k_tile + 1) * tn, :], xh, NT_DIMS,
                              preferred_element_type=F32)
        kt_refs[0][...] = (acc * k_scale).astype(BF16)
        col = tn
    else:
        p_ref[:, tn:2 * tn] = (tile(k_tile) * k_scale).astype(BF16)
        col = 2 * tn
    p_ref[:, col:col + tn] = tile(v_tile).astype(BF16)
    z = tile(z_tile)
    p_ref[:, col + tn:col + 2 * tn] = (_sigmoid(tile(o_tile)) * (z * _sigmoid(z))).astype(BF16)


def _inproj(x3, g, wt, wgt, bg, *, tm, tn, n_heads, k_transposed, k_scale):
    b, t, d = x3.shape
    assert t % tm == 0 and wt.shape[0] == 5 * tn and 2 * n_heads <= 8
    n_out = (3 if k_transposed else 4) * tn
    out_specs = [
        pl.BlockSpec((None, tm, n_out), lambda bi, i: (bi, i, 0)),
        pl.BlockSpec((None, tm, LANES), lambda bi, i: (bi, i, 0)),
        pl.BlockSpec((None, 8, tm), lambda bi, i: (bi, 0, i)),
    ]
    out_shape = [
        jax.ShapeDtypeStruct((b, t, n_out), BF16),
        jax.ShapeDtypeStruct((b, t, LANES), F32),
        jax.ShapeDtypeStruct((b, 8, t), F32),
    ]
    if k_transposed:
        out_specs.append(pl.BlockSpec((None, tn, tm), lambda bi, i: (bi, 0, i)))
        out_shape.append(jax.ShapeDtypeStruct((b, tn, t), BF16))
    const = lambda shape, **kw: pl.BlockSpec(shape, lambda bi, i: (0, 0), **kw)
    return pl.pallas_call(
        functools.partial(_inproj_kernel, n_heads=n_heads, tn=tn, k_scale=k_scale),
        grid=(b, t // tm),
        in_specs=[
            pl.BlockSpec((None, tm, d), lambda bi, i: (bi, i, 0)),
            const((1, d)),
            const(wt.shape, pipeline_mode=pl.Buffered(1)),
            const((LANES, d)),
            const((1, LANES)),
        ],
        out_specs=out_specs,
        out_shape=out_shape,
        compiler_params=_params(("parallel", "parallel")),
        name="inproj",
    )(x3, g, wt, wgt, bg)


def _scan_kernel(*refs, chunk, n_heads, dh, n_chunks, has_state):
    L, H = chunk, n_heads
    dext = dh + LANES
    (q_ref, kt_ref, v_ref, gate_ref, gc_ref, gr_ref, x_ref, wout_ref, hg_ref) = refs[:9]
    pos = 9
    if has_state:
        c0_ref, n0_ref, m0_ref = refs[pos:pos + 3]
        pos += 3
    xo_ref, cout_ref, nout_ref, mout_ref = refs[pos:pos + 4]
    cext_ref, cb_ref, m_ref = refs[pos + 4:pos + 7]
    c = pl.program_id(1)

    @pl.when(c == 0)
    def _():
        if has_state:
            lane0 = lax.broadcasted_iota(jnp.int32, (dh, LANES), 1) == 0
            for h in range(H):
                ncol = jnp.broadcast_to(n0_ref[h:h + 1, :], (LANES, dh)).T
                cext_ref[h, :, :dh] = c0_ref[h]
                cext_ref[h, :, dh:] = jnp.where(lane0, ncol, 0.0)
            m_ref[...] = m0_ref[...]
        else:
            cext_ref[...] = jnp.zeros(cext_ref.shape, F32)
            m_ref[...] = jnp.zeros(m_ref.shape, F32)
        cb_ref[...] = cext_ref[...].astype(BF16)

    gc = gc_ref[...]
    gr = gr_ref[...]
    row = lax.broadcasted_iota(jnp.int32, (L, L), 0)
    col = lax.broadcasted_iota(jnp.int32, (L, L), 1)
    causal = row >= col
    lane = lax.broadcasted_iota(jnp.int32, gc.shape, 1)
    subl = lax.broadcasted_iota(jnp.int32, gr.shape, 0)
    lf_c_hi, lf_c_lo = _split_bf16(jnp.where((lane >= H) & (lane < 2 * H), gc, 0.0))
    lf_r_hi, lf_r_lo = _split_bf16(jnp.where((subl >= H) & (subl < 2 * H), gr, 0.0))
    lower = causal.astype(BF16)
    upper = (row <= col).astype(BF16)
    bc_all = (jnp.dot(lower, lf_c_hi, preferred_element_type=F32)
              + jnp.dot(lower, lf_c_lo, preferred_element_type=F32))
    br_all = (jnp.dot(lf_r_hi, upper, preferred_element_type=F32)
              + jnp.dot(lf_r_lo, upper, preferred_element_type=F32))
    ones_col = (lax.broadcasted_iota(jnp.int32, (L, LANES), 1) == 0).astype(BF16)
    y = jnp.zeros((L, wout_ref.shape[1]), F32)

    for h in range(H):
        sl = slice(h * dh, (h + 1) * dh)
        q = q_ref[:, sl]
        kt = kt_ref[sl, :]
        vext = jnp.concatenate([v_ref[:, sl], ones_col], axis=1)
        ig_r = gr[h:h + 1, :]
        b_c = bc_all[:, H + h:H + h + 1]
        b_r = br_all[H + h:H + h + 1, :]
        m_prev = m_ref[h:h + 1, 0:1]

        log_d = jnp.where(causal, b_c - b_r + ig_r, -jnp.inf)
        log_inter = b_c + m_prev
        m_t = jnp.maximum(log_inter, jnp.max(log_d, axis=1, keepdims=True))
        dmat = jnp.exp(log_d - m_t)
        inter = jnp.exp(log_inter - m_t)
        s = jnp.dot(q, kt, preferred_element_type=F32) * dmat
        numden = (jnp.dot(s.astype(BF16), vext, preferred_element_type=F32)
                  + inter * jnp.dot(q, cb_ref[h], preferred_element_type=F32))
        num = numden[:, :dh]
        den = numden[:, dh:dh + 1]
        hh = num / jnp.maximum(jnp.abs(den), jnp.exp(-m_t))

        b_last = b_r[:, L - 1:L]
        a_r = b_last - b_r + ig_r
        m_new = jnp.maximum(b_last + m_prev, jnp.max(a_r, axis=1, keepdims=True))
        w_r = jnp.exp(a_r - m_new)
        decay = jnp.exp(b_last + m_prev - m_new)
        ktw = (kt.astype(F32) * w_r).astype(BF16)
        c_new = decay * cext_ref[h] + jnp.dot(ktw, vext, preferred_element_type=F32)
        cext_ref[h] = c_new
        cb_ref[h] = c_new.astype(BF16)
        m_ref[h:h + 1, :] = jnp.broadcast_to(m_new, (1, LANES))

        hn = hh * lax.rsqrt(jnp.mean(hh * hh, axis=1, keepdims=True) + EPS)
        hg = (hn * hg_ref[:, sl] * gate_ref[:, sl].astype(F32)).astype(BF16)
        y = y + jnp.dot(hg, wout_ref[sl, :], preferred_element_type=F32)

    xo_ref[...] = x_ref[...] + y

    @pl.when(c == n_chunks - 1)
    def _():
        for h in range(H):
            cout_ref[h] = cext_ref[h, :, :dh]
            nout_ref[h:h + 1, :] = cext_ref[h, :, dh:].T[0:1, :]
        mout_ref[...] = m_ref[...]


def _scan(p3, col_idx, kt3, gc, gr, x3, wout, hgain, state, *, chunk, n_heads):
    b, t, _ = p3.shape
    dh = kt3.shape[1] // n_heads
    di = n_heads * dh
    d = x3.shape[2]
    nc = t // chunk
    assert t % chunk == 0
    has_state = state is not None
    qi, vi, gi = col_idx

    def pspec(ci):
        return pl.BlockSpec((None, chunk, di), lambda bi, c, ci=ci: (bi, c, ci))

    in_specs = [
        pspec(qi),
        pl.BlockSpec((None, di, chunk), lambda bi, c: (bi, 0, c)),
        pspec(vi), pspec(gi),
        pl.BlockSpec((None, chunk, LANES), lambda bi, c: (bi, c, 0)),
        pl.BlockSpec((None, 8, chunk), lambda bi, c: (bi, 0, c)),
        pl.BlockSpec((None, chunk, d), lambda bi, c: (bi, c, 0)),
        pl.BlockSpec((di, d), lambda bi, c: (0, 0)),
        pl.BlockSpec((1, di), lambda bi, c: (0, 0)),
    ]
    args = [p3, kt3, p3, p3, gc, gr, x3, wout, hgain]
    if has_state:
        c0, n0, m0 = state
        in_specs += [
            pl.BlockSpec((None, n_heads, dh, dh), lambda bi, c: (bi, 0, 0, 0)),
            pl.BlockSpec((None, n_heads, dh), lambda bi, c: (bi, 0, 0)),
            pl.BlockSpec((None, 8, LANES), lambda bi, c: (bi, 0, 0)),
        ]
        args += [c0, n0, m0]
    return pl.pallas_call(
        functools.partial(_scan_kernel, chunk=chunk, n_heads=n_heads, dh=dh,
                          n_chunks=nc, has_state=has_state),
        grid=(b, nc),
        in_specs=in_specs,
        out_specs=[
            pl.BlockSpec((None, chunk, d), lambda bi, c: (bi, c, 0)),
            pl.BlockSpec((None, n_heads, dh, dh), lambda bi, c: (bi, 0, 0, 0)),
            pl.BlockSpec((None, n_heads, dh), lambda bi, c: (bi, 0, 0)),
            pl.BlockSpec((None, 8, LANES), lambda bi, c: (bi, 0, 0)),
        ],
        out_shape=[
            jax.ShapeDtypeStruct((b, t, d), F32),
            jax.ShapeDtypeStruct((b, n_heads, dh, dh), F32),
            jax.ShapeDtypeStruct((b, n_heads, dh), F32),
            jax.ShapeDtypeStruct((b, 8, LANES), F32),
        ],
        scratch_shapes=[
            pltpu.VMEM((n_heads, dh, dh + LANES), F32),
            pltpu.VMEM((n_heads, dh, dh + LANES), BF16),
            pltpu.VMEM((8, LANES), F32),
        ],
        compiler_params=_params(("parallel", "arbitrary")),
        name="mlstm_scan",
    )(*args)


def _headnorm(a):
    n = a.shape[1]
    r = lax.broadcasted_iota(jnp.int32, (MXU_DIM, MXU_DIM), 0)
    c = lax.broadcasted_iota(jnp.int32, (MXU_DIM, MXU_DIM), 1)
    same_head = (lax.shift_right_logical(r, HEAD_SHIFT)
                 == lax.shift_right_logical(c, HEAD_SHIFT)).astype(BF16)
    parts = []
    for c0 in range(0, n, MXU_DIM):
        blk = a[:, c0:c0 + MXU_DIM]
        ss = jnp.dot((blk * blk).astype(BF16), same_head, preferred_element_type=F32)
        parts.append(blk * lax.rsqrt(ss * (1.0 / ATTN_HEAD_DIM) + EPS))
    return jnp.concatenate(parts, axis=1)


def _store_by_residue(ref, val, scr, dil):
    rows, width = val.shape
    if dil == 1:
        ref[0] = val.astype(BF16)
        return
    for c in range(width // LANES):
        scr[c] = val[:, c * LANES:(c + 1) * LANES]
    for r in range(dil):
        parts = [scr[c, pl.ds(r, rows // dil, stride=dil), :] for c in range(width // LANES)]
        ref[r] = jnp.concatenate(parts, axis=1).astype(BF16)


def _proj_headnorm_kernel(x_ref, g_ref, w_ref, hg_ref, a_ref, r_ref, *row_refs, n_norm, scale):
    gw = GROUP_HEADS * ATTN_HEAD_DIM
    xf = x_ref[...]
    xn = (xf * _rms_scale(xf) * g_ref[...]).astype(BF16)
    p = jnp.dot(xn, w_ref[...], preferred_element_type=F32)
    a = _headnorm(p[:, :n_norm]) * hg_ref[...]
    if scale != 1.0:
        a = a * scale
    r = p[:, n_norm:]
    a_ref[...] = a.astype(BF16)
    r_ref[...] = r.astype(BF16)
    for g, row_ref in enumerate(row_refs):
        cs = slice(g * gw, (g + 1) * gw)
        row_ref[:, :gw] = a[:, cs]
        row_ref[:, gw:] = r[:, cs]


def _proj_headnorm(x2, g, w, hgain, *, n_norm, scale, group_rows=False):
    m, d = x2.shape
    n = w.shape[1]
    gw = GROUP_HEADS * ATTN_HEAD_DIM
    assert n_norm % MXU_DIM == 0 and m % 16 == 0
    whole = lambda shape: pl.BlockSpec(shape, lambda i: (0, 0))
    out_specs = [whole((m, n_norm)), whole((m, n - n_norm))]
    out_shape = [jax.ShapeDtypeStruct((m, n_norm), BF16), jax.ShapeDtypeStruct((m, n - n_norm), BF16)]
    if group_rows:
        assert n - n_norm == n_norm
        for _ in range(n_norm // gw):
            out_specs.append(whole((m, 2 * gw)))
            out_shape.append(jax.ShapeDtypeStruct((m, 2 * gw), F32))
    return pl.pallas_call(
        functools.partial(_proj_headnorm_kernel, n_norm=n_norm, scale=scale),
        grid=(1,),
        in_specs=[whole((m, d)), whole((1, d)), whole((d, n)), whole((1, n_norm))],
        out_specs=out_specs,
        out_shape=out_shape,
        compiler_params=_params(("arbitrary",)),
        name="proj_headnorm",
    )(x2, g, w, hgain)


def _kvq_proj_kernel(x_ref, gkv_ref, wkv_ref, kg_ref, gb_ref, wb_ref, qg_ref, *refs,
                     qscale, tails, tm, dils):
    ng = len(dils)
    gw = GROUP_HEADS * ATTN_HEAD_DIM
    n_norm = ng * gw
    k_refs, v_refs, q_refs = refs[:ng], refs[ng:2 * ng], refs[2 * ng:3 * ng]
    z_ref = refs[3 * ng]
    tail_refs = refs[3 * ng + 1:3 * ng + 1 + len(tails)]
    scr = refs[-1]
    xf = x_ref[...]
    xhat = xf * _rms_scale(xf)
    xkv = (xhat * gkv_ref[...]).astype(BF16)
    xb = (xhat * gb_ref[...]).astype(BF16)
    for g, dil in enumerate(dils):
        cs = slice(g * gw, (g + 1) * gw)
        k = _headnorm(jnp.dot(xkv, wkv_ref[:, cs], preferred_element_type=F32)) * kg_ref[:, cs]
        v = jnp.dot(xkv, wkv_ref[:, n_norm + g * gw:n_norm + (g + 1) * gw],
                    preferred_element_type=F32)
        q = (_headnorm(jnp.dot(xb, wb_ref[:, cs], preferred_element_type=F32))
             * qg_ref[:, cs] * qscale)
        _store_by_residue(k_refs[g], k, scr, dil)
        _store_by_residue(v_refs[g], v, scr, dil)
        _store_by_residue(q_refs[g], q, scr, dil)
        rows, t_ref = tails[g][0], tail_refs[g]
        t_ref[:, :gw] = k[max(tm - rows, 0):, :]
        t_ref[:, gw:] = v[max(tm - rows, 0):, :]
    z_ref[...] = jnp.dot(xb, wb_ref[:, n_norm:], preferred_element_type=F32).astype(BF16)


def _kvq_proj(x3, gkv, wkv, kgain, gb, wb, qgain, *, qscale, tm, tail_rows, dils):
    b, t, d = x3.shape
    gw = GROUP_HEADS * ATTN_HEAD_DIM
    ng = len(dils)
    assert t % tm == 0 and wkv.shape[1] == 2 * ng * gw and wb.shape[1] > ng * gw
    n_tiles = t // tm
    out_specs, out_shape = [], []
    for _ in range(3):
        for dil in dils:
            assert tm % (dil * 16) == 0
            out_specs.append(pl.BlockSpec((None, dil, tm // dil, gw), lambda bi, i: (bi, 0, i, 0)))
            out_shape.append(jax.ShapeDtypeStruct((b, dil, t // dil, gw), BF16))
    zw = wb.shape[1] - ng * gw
    out_specs.append(pl.BlockSpec((None, tm, zw), lambda bi, i: (bi, i, 0)))
    out_shape.append(jax.ShapeDtypeStruct((b, t, zw), BF16))
    tails = []
    for rows in tail_rows:
        if rows >= tm:
            assert rows % tm == 0
            first, blk = n_tiles - rows // tm, tm
        else:
            first, blk = n_tiles - 1, rows
        tails.append((rows, first))
        out_specs.append(pl.BlockSpec(
            (None, blk, 2 * gw), lambda bi, i, first=first: (bi, jnp.maximum(i - first, 0), 0)))
        out_shape.append(jax.ShapeDtypeStruct((b, rows, 2 * gw), F32))
    const = lambda shape: pl.BlockSpec(shape, lambda bi, i: (0, 0))
    outs = pl.pallas_call(
        functools.partial(_kvq_proj_kernel, qscale=qscale, tails=tuple(tails), tm=tm, dils=dils),
        grid=(b, n_tiles),
        in_specs=[
            pl.BlockSpec((None, tm, d), lambda bi, i: (bi, i, 0)),
            const((1, d)), const(wkv.shape), const((1, ng * gw)),
            const((1, d)), const(wb.shape), const((1, ng * gw)),
        ],
        out_specs=out_specs,
        out_shape=out_shape,
        scratch_shapes=[pltpu.VMEM((gw // LANES, tm, LANES), F32)],
        compiler_params=_params(("parallel", "arbitrary")),
        name="kvq_proj",
    )(x3, gkv, wkv, kgain, gb, wb, qgain)
    return outs[:ng], outs[ng:2 * ng], outs[2 * ng:3 * ng], outs[3 * ng], outs[3 * ng + 1:]


def _attn_prompt_kernel(q_ref, kp_ref, kc_ref, vp_ref, vc_ref, bias_ref, o_ref, lse_ref, *, nq):
    blk = ATTN_BLOCK
    first = jnp.where(pl.program_id(2) == 0, 0, 1)
    lane = lax.broadcasted_iota(jnp.int32, (blk, LANES), 1)
    low_half = lane < ATTN_HEAD_DIM
    ones_tile = jnp.ones((2 * blk, LANES), BF16)
    for rr, jb in [(rr, jb) for rr in range(q_ref.shape[0]) for jb in range(nq)]:
        q = q_ref[rr, jb * blk:(jb + 1) * blk, :]
        if jb == 0:
            kcat = jnp.concatenate([kp_ref[rr], kc_ref[rr, 0:blk, :]], axis=0)
            vcat = jnp.concatenate([vp_ref[rr], vc_ref[rr, 0:blk, :]], axis=0)
        else:
            kcat = kc_ref[rr, (jb - 1) * blk:(jb + 1) * blk, :]
            vcat = vc_ref[rr, (jb - 1) * blk:(jb + 1) * blk, :]
        ms, outs = [], []
        sums = jnp.ones((blk, LANES), F32)
        for j in range(GROUP_HEADS // 2):
            ps = slice(j * LANES, (j + 1) * LANES)
            qf = q[:, ps].astype(F32)
            kpair = kcat[:, ps]
            vext = jnp.concatenate([vcat[:, ps], ones_tile], axis=1)
            pair = []
            for half in range(2):
                h = 2 * j + half
                qh = jnp.where(low_half if half == 0 else ~low_half, qf, 0.0).astype(BF16)
                bias = bias_ref[first, h] if jb == 0 else bias_ref[1, h]
                st = lax.dot_general(kpair, qh, NT_DIMS, preferred_element_type=F32) + bias
                m = jnp.max(st, axis=0, keepdims=True)
                p = jnp.exp(st - m).T.astype(BF16)
                pv = jnp.dot(p, vext, preferred_element_type=F32)
                l = pv[:, LANES:]
                pair.append(pv[:, :LANES] * (1.0 / l))
                sums = jnp.where(lane == h, l, sums)
                ms.append(m)
            outs.append(jnp.where(low_half, pair[0], pair[1]))
        o_ref[rr, jb * blk:(jb + 1) * blk, :] = jnp.concatenate(outs, axis=1).astype(o_ref.dtype)
        m_t = jnp.concatenate(ms + [jnp.zeros((LANES - GROUP_HEADS, blk), F32)], axis=0).T
        lse_ref[rr, jb * blk:(jb + 1) * blk, :] = m_t + jnp.log(sums)


def _attn_prompt(q, k, v, bias, g):
    b, dil, s, gw = q.shape
    nb = s // ATTN_BLOCK
    assert s % ATTN_BLOCK == 0
    nq = min(ATTN_BLOCKS_PER_STEP, nb)
    rps = min(ATTN_BLOCKS_PER_STEP // nq, dil)
    assert nb % nq == 0 and dil % rps == 0
    rows = nq * ATTN_BLOCK
    cur = pl.BlockSpec((None, rps, rows, gw), lambda bi, r, j: (bi, r, j, 0))
    prev = pl.BlockSpec((None, rps, ATTN_BLOCK, gw),
                        lambda bi, r, j: (bi, r, jnp.maximum(j * nq - 1, 0), 0))
    return pl.pallas_call(
        functools.partial(_attn_prompt_kernel, nq=nq),
        grid=(b, dil // rps, nb // nq),
        in_specs=[
            cur, prev, cur, prev, cur,
            pl.BlockSpec(bias.shape, lambda bi, r, j: (0, 0, 0, 0)),
        ],
        out_specs=[
            pl.BlockSpec((None, rps, rows, gw), lambda bi, r, j: (bi, r, j, 0)),
            pl.BlockSpec((None, rps, rows, LANES), lambda bi, r, j: (bi, r, j, 0)),
        ],
        out_shape=[
            jax.ShapeDtypeStruct((b, dil, s, gw), BF16),
            jax.ShapeDtypeStruct((b, dil, s, LANES), F32),
        ],
        compiler_params=_params(("parallel", "parallel", "arbitrary")),
        name="attn_prompt_g%d" % g,
    )(q, k, k, v, v, bias)


def _head_expand_matrix():
    r = lax.broadcasted_iota(jnp.int32, (LANES, GROUP_HEADS * ATTN_HEAD_DIM), 0)
    c = lax.broadcasted_iota(jnp.int32, (LANES, GROUP_HEADS * ATTN_HEAD_DIM), 1)
    return (r == lax.shift_right_logical(c, HEAD_SHIFT)).astype(BF16)


def _load_token_order(ref, scr):
    dil, per, width = ref.shape
    if dil == 1:
        return ref[0].astype(F32)
    n_tiles = width // LANES
    for r in range(dil):
        val = ref[r].astype(F32)
        for c in range(n_tiles):
            scr[c, pl.ds(r, per, stride=dil), :] = val[:, c * LANES:(c + 1) * LANES]
    return jnp.concatenate([scr[c] for c in range(n_tiles)], axis=1)


def _merge_out_kernel(o0_ref, o1_ref, o2_ref, l0_ref, l1_ref, l2_ref, z_ref, x_ref, w_ref, y_ref,
                      *scratch):
    o_refs, l_refs = (o0_ref, o1_ref, o2_ref), (l0_ref, l1_ref, l2_ref)
    os_, ls, k = [], [], 0
    for o_ref, l_ref in zip(o_refs, l_refs):
        if o_ref.shape[0] == 1:
            os_.append(o_ref[0].astype(F32))
            ls.append(l_ref[0])
        else:
            os_.append(_load_token_order(o_ref, scratch[k]))
            ls.append(_load_token_order(l_ref, scratch[k + 1]))
            k += 2
    lmax = jnp.maximum(jnp.maximum(ls[0], ls[1]), ls[2])
    es = [jnp.exp(l - lmax) for l in ls]
    tot = es[0] + es[1] + es[2]
    expand = _head_expand_matrix()
    o = jnp.zeros(os_[0].shape, F32)
    for e, og in zip(es, os_):
        wexp = jnp.dot((e / tot).astype(BF16), expand, preferred_element_type=F32)
        o = o + wexp * og
    zf = z_ref[...].astype(F32)
    a = (o * (zf * _sigmoid(zf))).astype(BF16)
    y_ref[...] = x_ref[...] + jnp.dot(a, w_ref[...], preferred_element_type=F32)


def _merge_out(outs, lses, z3, x3, w, *, tm):
    b, t, d = x3.shape
    gw = w.shape[0]
    assert t % tm == 0
    row = lambda width: pl.BlockSpec((None, tm, width), lambda bi, i: (bi, i, 0))
    by_residue = lambda a: pl.BlockSpec((None, a.shape[1], tm // a.shape[1], a.shape[3]),
                                        lambda bi, i: (bi, 0, i, 0))
    scratch = []
    for o in outs:
        if o.shape[1] > 1:
            scratch += [pltpu.VMEM((gw // LANES, tm, LANES), F32), pltpu.VMEM((1, tm, LANES), F32)]
    return pl.pallas_call(
        _merge_out_kernel,
        grid=(b, t // tm),
        in_specs=[by_residue(o) for o in outs] + [by_residue(l) for l in lses]
        + [row(gw), row(d), pl.BlockSpec((gw, d), lambda bi, i: (0, 0))],
        out_specs=row(d),
        out_shape=jax.ShapeDtypeStruct((b, t, d), F32),
        scratch_shapes=scratch,
        compiler_params=_params(("parallel", "parallel")),
        name="merge_out",
    )(*outs, *lses, z3, x3, w)


def _attn_sample_kernel(q_ref, kn_ref, vn_ref, z_ref, c0_ref, c1_ref, c2_ref,
                        b0_ref, b1_ref, b2_ref, a_ref, ks0, vs0, ks1, vs1, ks2, vs2, *, s_new):
    gw = GROUP_HEADS * ATTN_HEAD_DIM
    rows = GROUP_HEADS * s_new
    r = lax.broadcasted_iota(jnp.int32, (rows, gw), 0)
    c = lax.broadcasted_iota(jnp.int32, (rows, gw), 1)
    head_mask = (lax.shift_right_logical(r, int(math.log2(s_new)))
                 == lax.shift_right_logical(c, HEAD_SHIFT))
    caches = ((c0_ref, b0_ref, ks0, vs0), (c1_ref, b1_ref, ks1, vs1), (c2_ref, b2_ref, ks2, vs2))
    pad_rows = jnp.zeros((LANES - s_new, gw), F32)
    outs, lses = [], []
    for g, (c_ref, b_ref, ks, vs) in enumerate(caches):
        buf_len = c_ref.shape[2]
        cs = slice(g * gw, (g + 1) * gw)
        ks[:, :buf_len] = c_ref[0].astype(BF16)
        vs[:, :buf_len] = c_ref[1].astype(BF16)
        ks[:, buf_len:] = jnp.concatenate([kn_ref[:, cs].astype(F32), pad_rows], axis=0).T.astype(BF16)
        vs[:, buf_len:] = jnp.concatenate([vn_ref[:, cs].astype(F32), pad_rows], axis=0).T.astype(BF16)
        qg = q_ref[:, cs].astype(F32)
        qbd = jnp.where(head_mask, jnp.concatenate([qg] * GROUP_HEADS, axis=0), 0.0).astype(BF16)
        s = jnp.dot(qbd, ks[...], preferred_element_type=F32) + b_ref[...]
        m = jnp.max(s, axis=1, keepdims=True)
        p = jnp.exp(s - m)
        l = jnp.sum(p, axis=1, keepdims=True)
        outs.append(lax.dot_general(p.astype(BF16), vs[...], NT_DIMS, preferred_element_type=F32) / l)
        lses.append(m + jnp.log(l))
    lmax = jnp.maximum(jnp.maximum(lses[0], lses[1]), lses[2])
    es = [jnp.exp(l - lmax) for l in lses]
    tot = es[0] + es[1] + es[2]
    o = jnp.zeros((rows, gw), F32)
    for e, og in zip(es, outs):
        o = o + (e / tot) * og
    o = jnp.where(head_mask, o, 0.0)
    folded = o[0:s_new, :]
    for h in range(1, GROUP_HEADS):
        folded = folded + o[h * s_new:(h + 1) * s_new, :]
    zf = z_ref[...].astype(F32)
    a_ref[...] = (folded * (zf * _sigmoid(zf))).astype(BF16)


def _attn_sample(q, kn, vn, z, caches, biases):
    b, s_new, qw = q.shape
    gw = GROUP_HEADS * ATTN_HEAD_DIM
    assert s_new % 8 == 0
    cache2 = [jnp.transpose(cb, (0, 2, 3, 4, 1)).reshape(b, 2, gw, cb.shape[1]) for cb in caches]
    in_specs = [
        pl.BlockSpec((None, s_new, qw), lambda bi: (bi, 0, 0)),
        pl.BlockSpec((None, s_new, qw), lambda bi: (bi, 0, 0)),
        pl.BlockSpec((None, s_new, qw), lambda bi: (bi, 0, 0)),
        pl.BlockSpec((None, s_new, gw), lambda bi: (bi, 0, 0)),
    ]
    for cb in cache2:
        in_specs.append(pl.BlockSpec((None, 2, gw, cb.shape[3]), lambda bi: (bi, 0, 0, 0)))
    for bt in biases:
        in_specs.append(pl.BlockSpec(bt.shape, lambda bi: (0, 0)))
    scratch = []
    for cb in cache2:
        scratch += [pltpu.VMEM((gw, cb.shape[3] + LANES), BF16)] * 2
    return pl.pallas_call(
        functools.partial(_attn_sample_kernel, s_new=s_new),
        grid=(b,),
        in_specs=in_specs,
        out_specs=pl.BlockSpec((None, s_new, gw), lambda bi: (bi, 0, 0)),
        out_shape=jax.ShapeDtypeStruct((b, s_new, gw), BF16),
        scratch_shapes=scratch,
        compiler_params=_params(("arbitrary",)),
        name="attn_sample",
    )(q, kn, vn, z, *cache2, *biases)


def _matmul_residual_kernel(a_ref, x_ref, w_ref, y_ref):
    y_ref[...] = x_ref[...] + jnp.dot(a_ref[...], w_ref[...], preferred_element_type=F32)


def _matmul_residual(a2, x2, w, *, tm):
    m, d = x2.shape
    kdim = a2.shape[1]
    assert m % tm == 0
    return pl.pallas_call(
        _matmul_residual_kernel,
        grid=(m // tm,),
        in_specs=[
            pl.BlockSpec((tm, kdim), lambda i: (i, 0)),
            pl.BlockSpec((tm, d), lambda i: (i, 0)),
            pl.BlockSpec((kdim, d), lambda i: (0, 0)),
        ],
        out_specs=pl.BlockSpec((tm, d), lambda i: (i, 0)),
        out_shape=jax.ShapeDtypeStruct((m, d), F32),
        compiler_params=_params(("parallel",)),
        name="matmul_residual",
    )(a2, x2, w)


def _t5_bucket_np(dist):
    exact = N_BUCKETS // 2
    d = np.maximum(dist, 1).astype(np.float32)
    large = exact + (np.log(d / np.float32(exact)) / np.float32(math.log(MAX_DISTANCE / exact))
                     * np.float32(N_BUCKETS - exact)).astype(np.int32)
    return np.where(dist < exact, dist, np.minimum(large, N_BUCKETS - 1)).astype(np.int32)


def _bias_by_step(rel_bias_g, jmax, dil):
    return rel_bias_g.astype(F32)[_t5_bucket_np(np.arange(jmax + 1) * dil)]


def _prompt_bias(rel_bias_g, win, dil):
    jmax = win // dil
    assert jmax == ATTN_BLOCK
    qi = np.arange(ATTN_BLOCK)[:, None]
    kj = np.arange(2 * ATTN_BLOCK)[None, :]
    rel = qi + ATTN_BLOCK - kj
    band = (rel >= 0) & (rel <= jmax)
    bvec = _bias_by_step(rel_bias_g, jmax, dil)
    period = 2 * ATTN_BLOCK + 1
    base = jnp.concatenate([bvec[::-1], jnp.zeros((period - jmax - 1, GROUP_HEADS), F32)], axis=0).T
    bias = jnp.tile(base, (1, ATTN_BLOCK))[:, :2 * ATTN_BLOCK * ATTN_BLOCK]
    bias = bias.reshape(GROUP_HEADS, ATTN_BLOCK, 2 * ATTN_BLOCK)
    rest = jnp.where(band[None], bias, -jnp.inf)
    first = jnp.where((band & (kj >= ATTN_BLOCK))[None], bias, -jnp.inf)
    return jnp.swapaxes(jnp.stack([first, rest], axis=0), 2, 3)


def _sample_bias(rel_bias_g, win, dil, buf_len, s_new):
    jmax = win // dil
    assert buf_len == jmax * dil
    width = buf_len + LANES
    bvec = _bias_by_step(rel_bias_g, jmax, dil)
    gaps = jnp.full((jmax + 1, dil - 1, GROUP_HEADS), -jnp.inf, F32)
    by_dist = jnp.concatenate([bvec[:, None, :], gaps], axis=1).reshape((jmax + 1) * dil, GROUP_HEADS)
    padded = jnp.pad(by_dist[::-1], ((s_new, width), (0, 0)), constant_values=-jnp.inf)
    rows = [padded[dil - 1 - s + s_new:dil - 1 - s + s_new + width] for s in range(s_new)]
    table = jnp.transpose(jnp.stack(rows, axis=0), (2, 0, 1))
    return table.reshape(GROUP_HEADS * s_new, width)


def _layer_a(x3, state, weights, *, chunk, pad_to):
    norm_a, wt, wgt, bg, hgain, wout = weights
    b, t, d = x3.shape
    H = MLSTM_HEADS
    di = wout.shape[0]
    dh = di // H
    k_scale = dh ** -0.5
    k_idx = 1
    if t > SHORT_SCAN_CHUNK:
        assert pad_to == t
        p3, gc3, gr3, kt3 = _inproj(x3, norm_a, wt, wgt, bg, tm=min(INPROJ_ROWS, t), tn=di,
                                    n_heads=H, k_transposed=True, k_scale=k_scale)
        col_idx = (0, 1, 2)
        xin = x3
    else:
        m = b * t
        extra = pad_to - t
        p, gc, gr = _inproj(x3.reshape(1, m, d), norm_a, wt, wgt, bg, tm=m, tn=di, n_heads=H,
                            k_transposed=False, k_scale=k_scale)
        col_idx = (0, 2, 3)
        p3 = p.reshape(b, t, -1)
        k3 = p3[:, :, k_idx * di:(k_idx + 1) * di]
        kt3 = jnp.pad(jnp.swapaxes(k3, 1, 2), ((0, 0), (0, 0), (0, extra)))
        p3 = jnp.pad(p3, ((0, 0), (0, extra), (0, 0)))
        xin = jnp.pad(x3, ((0, 0), (0, extra), (0, 0)))
        lane = np.arange(LANES)
        pad_col = np.where(lane < H, -np.inf, 0.0).astype(np.float32)
        gc3 = jnp.concatenate([gc.reshape(b, t, LANES),
                               jnp.broadcast_to(pad_col, (b, extra, LANES))], axis=1)
        pad_row = np.where(np.arange(8) < H, -np.inf, 0.0).astype(np.float32)[:, None, None]
        gr3 = jnp.swapaxes(jnp.concatenate([gr.reshape(8, b, t),
                                            jnp.broadcast_to(pad_row, (8, b, extra))], axis=2), 0, 1)
    xo, c_out, n_out, m_out = _scan(p3, col_idx, kt3, gc3, gr3, xin, wout, hgain, state,
                                    chunk=chunk, n_heads=H)
    return xo[:, :t], c_out, n_out, m_out[:, :H, 0]


def kernel(x_prompt, x_sample, state_mlstm_C, state_mlstm_n, state_mlstm_m, cache_kv_w128, cache_kv_w512, cache_kv_w2048, norm_a, w_in_a, b_gates_a, hnorm_a, w_out_a, norm_kv, w_kv, k_norm, norm_b, w_in_b, q_norm, rel_bias, w_out_b):
    H = MLSTM_HEADS
    bp, tp, d = x_prompt.shape
    bs, ts, _ = x_sample.shape
    di = w_out_a.shape[1]
    dh = di // H
    gw = GROUP_HEADS * ATTN_HEAD_DIM
    qw = len(GROUPS) * gw
    caches = (cache_kv_w128, cache_kv_w512, cache_kv_w2048)
    assert norm_a.shape[0] == 1 and norm_b.shape[0] == 1, "one mLSTM layer, one attention layer"
    for cb, (win, _) in zip(caches, GROUPS):
        assert cb.shape[1] == win, "window buffers must hold a full window"

    w_at = w_in_a[0].T
    wt_a = w_at[:5 * di].astype(BF16)
    wgt = jnp.pad(w_at[5 * di:].astype(F32), ((0, LANES - 2 * H), (0, 0)))
    bg = jnp.pad(b_gates_a[0].astype(F32), (0, LANES - 2 * H))[None, :]
    na = norm_a[0].astype(F32)[None, :]
    hgain = hnorm_a[0].astype(F32)[None, :]
    wout_a = w_out_a[0].astype(BF16)
    weights_a = (na, wt_a, wgt, bg, hgain, wout_a)

    xp1, c_p, n_p, m_p = _layer_a(x_prompt, None, weights_a, chunk=SCAN_CHUNK, pad_to=tp)
    m0 = jnp.pad(jnp.broadcast_to(state_mlstm_m[0].astype(F32)[:, :, None], (bs, H, LANES)),
                 ((0, 0), (0, 8 - H), (0, 0)))
    state_s = (state_mlstm_C[0].astype(F32), state_mlstm_n[0].astype(F32), m0)
    assert ts <= SHORT_SCAN_CHUNK
    xs1, c_s, n_s, m_s = _layer_a(x_sample, state_s, weights_a, chunk=SHORT_SCAN_CHUNK,
                                  pad_to=SHORT_SCAN_CHUNK)

    nkv = norm_kv.astype(F32)[None, :]
    wkv = w_kv.astype(BF16)
    kgain = jnp.tile(k_norm.astype(F32), qw // ATTN_HEAD_DIM)[None, :]
    rows_p = [min(win, tp) for win, _ in GROUPS]
    dils = tuple(dil for _, dil in GROUPS)
    nb_ = norm_b[0].astype(F32)[None, :]
    wb = w_in_b[0].astype(BF16)
    qgain = jnp.tile(q_norm[0].astype(F32), qw // ATTN_HEAD_DIM)[None, :]
    qscale = ATTN_HEAD_DIM ** -0.5
    kp, vp, qp, zp, (kv128_p, kv512_p, kv2048_p) = _kvq_proj(
        xp1, nkv, wkv, kgain, nb_, wb, qgain, qscale=qscale, tm=KVQ_ROWS, tail_rows=rows_p,
        dils=dils)
    xs1_flat = xs1.reshape(bs * ts, d)
    by_seq = lambda a: a.reshape(bs, ts, a.shape[-1])
    ks, vs, kv128_s, kv512_s, kv2048_s = map(by_seq, _proj_headnorm(
        xs1_flat, nkv, wkv, kgain, n_norm=qw, scale=1.0, group_rows=True))

    wout_b = w_out_b[0].astype(BF16)
    qs, zs = map(by_seq, _proj_headnorm(xs1_flat, nb_, wb, qgain, n_norm=qw, scale=qscale))

    outs, lses = [], []
    for g, (win, dil) in enumerate(GROUPS):
        bias = _prompt_bias(rel_bias[:, g * GROUP_HEADS:(g + 1) * GROUP_HEADS], win, dil)
        o, lse = _attn_prompt(qp[g], kp[g], vp[g], bias, g)
        outs.append(o)
        lses.append(lse)
    y_p = _merge_out(outs, lses, zp, xp1, wout_b, tm=MERGE_ROWS)

    sbias = [_sample_bias(rel_bias[:, g * GROUP_HEADS:(g + 1) * GROUP_HEADS], win, dil,
                          caches[g].shape[1], ts) for g, (win, dil) in enumerate(GROUPS)]
    a_s = _attn_sample(qs, ks, vs, zs, caches, sbias)
    y_s = _matmul_residual(a_s.reshape(bs * ts, gw), xs1.reshape(bs * ts, d), wout_b,
                           tm=bs * ts).reshape(bs, ts, d)

    kv5 = lambda a: a.reshape(a.shape[0], a.shape[1], 2, GROUP_HEADS, ATTN_HEAD_DIM)
    return (y_p, y_s, c_p[None], n_p[None], m_p[None], c_s[None], n_s[None], m_s[None],
            kv5(kv128_p), kv5(kv512_p), kv5(kv2048_p), kv5(kv128_s), kv5(kv512_s), kv5(kv2048_s))
```

```python
import functools
import math

import numpy as np
import jax
import jax.numpy as jnp
from jax import lax
from jax.experimental import pallas as pl
from jax.experimental.pallas import tpu as pltpu

F32 = jnp.float32
BF16 = jnp.bfloat16

EPS = 1e-6
MLSTM_HEADS = 4
GROUPS = ((128, 1), (512, 4), (2048, 16))
GROUP_HEADS = 8
ATTN_HEAD_DIM = 64
ATTN_BLOCK = 128
N_BUCKETS = 32
MAX_DISTANCE = 2048

HEAD_SHIFT = ATTN_HEAD_DIM.bit_length() - 1

LANES = 128
MXU_DIM = 256
VMEM_LIMIT_BYTES = 56 * 1024 * 1024

INPROJ_ROWS = 512
SCAN_CHUNK = 256
SHORT_SCAN_CHUNK = 8
KVQ_ROWS = 512
MERGE_ROWS = 1024
ATTN_BLOCKS_PER_STEP = 16

NT_DIMS = (((1,), (1,)), ((), ()))


def _params(sem):
    return pltpu.CompilerParams(dimension_semantics=sem, vmem_limit_bytes=VMEM_LIMIT_BYTES)


def _rms_scale(xf):
    return lax.rsqrt(jnp.mean(xf * xf, axis=-1, keepdims=True) + EPS)


def _sigmoid(x):
    return 0.5 * jnp.tanh(0.5 * x) + 0.5


def _split_bf16(a):
    hi = a.astype(BF16)
    lo = (a - hi.astype(F32)).astype(BF16)
    return hi, lo


def _inproj_kernel(x_ref, g_ref, wt_ref, wgt_ref, bg_ref, p_ref, gc_ref, gr_ref, *kt_refs,
                   n_heads, tn, k_scale):
    xf = x_ref[...]
    xn = xf * _rms_scale(xf) * g_ref[...]
    xh, xl = _split_bf16(xn)
    wh, wl = _split_bf16(wgt_ref[...])
    gates = (lax.dot_general(xh, wh, NT_DIMS, preferred_element_type=F32)
             + lax.dot_general(xl, wh, NT_DIMS, preferred_element_type=F32)
             + lax.dot_general(xh, wl, NT_DIMS, preferred_element_type=F32)) + bg_ref[...]
    lane = lax.broadcasted_iota(jnp.int32, gates.shape, 1)
    logsig = jnp.minimum(gates, 0.0) - jnp.log(1.0 + jnp.exp(-jnp.abs(gates)))
    gcol = jnp.where(lane < n_heads, gates, jnp.where(lane < 2 * n_heads, logsig, 0.0))
    gc_ref[...] = gcol
    gr_ref[...] = gcol.T[:8, :]

    def tile(j):
        return lax.dot_general(xh, wt_ref[j * tn:(j + 1) * tn, :], NT_DIMS,
                               preferred_element_type=F32)

    q_tile, k_tile, v_tile, o_tile, z_tile = range(5)
    p_ref[:, 0:tn] = tile(q_tile).astype(BF16)
    if kt_refs:
        acc = lax.dot_general(wt_ref[k_tile * tn:(k_tile + 1) * tn, :], xh, NT_DIMS,
                              preferred_element_type=F32)
        kt_refs[0][...] = (acc * k_scale).astype(BF16)
        col = tn
    else:
        p_ref[:, tn:2 * tn] = (tile(k_tile) * k_scale).astype(BF16)
        col = 2 * tn
    p_ref[:, col:col + tn] = tile(v_tile).astype(BF16)
    z = tile(z_tile)
    p_ref[:, col + tn:col + 2 * tn] = (_sigmoid(tile(o_tile)) * (z * _sigmoid(z))).astype(BF16)


def _inproj(x3, g, wt, wgt, bg, *, tm, tn, n_heads, k_transposed, k_scale):
    b, t, d = x3.shape
    assert t % tm == 0 and wt.shape[0] == 5 * tn and 2 * n_heads <= 8
    n_out = (3 if k_transposed else 4) * tn
    out_specs = [
        pl.BlockSpec((None, tm, n_out), lambda bi, i: (bi, i, 0)),
        pl.BlockSpec((None, tm, LANES), lambda bi, i: (bi, i, 0)),
        pl.BlockSpec((None, 8, tm), lambda bi, i: (bi, 0, i)),
    ]
    out_shape = [
        jax.ShapeDtypeStruct((b, t, n_out), BF16),
        jax.ShapeDtypeStruct((b, t, LANES), F32),
        jax.ShapeDtypeStruct((b, 8, t), F32),
    ]
    if k_transposed:
        out_specs.append(pl.BlockSpec((None, tn, tm), lambda bi, i: (bi, 0, i)))
        out_shape.append(jax.ShapeDtypeStruct((b, tn, t), BF16))
    const = lambda shape, **kw: pl.BlockSpec(shape, lambda bi, i: (0, 0), **kw)
    return pl.pallas_call(
        functools.partial(_inproj_kernel, n_heads=n_heads, tn=tn, k_scale=k_scale),
        grid=(b, t // tm),
        in_specs=[
            pl.BlockSpec((None, tm, d), lambda bi, i: (bi, i, 0)),
            const((1, d)),
            const(wt.shape, pipeline_mode=pl.Buffered(1)),
            const((LANES, d)),
            const((1, LANES)),
        ],
        out_specs=out_specs,
        out_shape=out_shape,
        compiler_params=_params(("parallel", "parallel")),
        name="inproj",
    )(x3, g, wt, wgt, bg)


def _scan_kernel(*refs, chunk, n_heads, dh, n_chunks, has_state):
    L, H = chunk, n_heads
    dext = dh + LANES
    (q_ref, kt_ref, v_ref, gate_ref, gc_ref, gr_ref, x_ref, wout_ref, hg_ref) = refs[:9]
    pos = 9
    if has_state:
        c0_ref, n0_ref, m0_ref = refs[pos:pos + 3]
        pos += 3
    xo_ref, cout_ref, nout_ref, mout_ref = refs[pos:pos + 4]
    cext_ref, cb_ref, m_ref = refs[pos + 4:pos + 7]
    c = pl.program_id(1)

    @pl.when(c == 0)
    def _():
        if has_state:
            lane0 = lax.broadcasted_iota(jnp.int32, (dh, LANES), 1) == 0
            for h in range(H):
                ncol = jnp.broadcast_to(n0_ref[h:h + 1, :], (LANES, dh)).T
                cext_ref[h, :, :dh] = c0_ref[h]
                cext_ref[h, :, dh:] = jnp.where(lane0, ncol, 0.0)
            m_ref[...] = m0_ref[...]
        else:
            cext_ref[...] = jnp.zeros(cext_ref.shape, F32)
            m_ref[...] = jnp.zeros(m_ref.shape, F32)
        cb_ref[...] = cext_ref[...].astype(BF16)

    gc = gc_ref[...]
    gr = gr_ref[...]
    row = lax.broadcasted_iota(jnp.int32, (L, L), 0)
    col = lax.broadcasted_iota(jnp.int32, (L, L), 1)
    causal = row >= col
    lane = lax.broadcasted_iota(jnp.int32, gc.shape, 1)
    subl = lax.broadcasted_iota(jnp.int32, gr.shape, 0)
    lf_c_hi, lf_c_lo = _split_bf16(jnp.where((lane >= H) & (lane < 2 * H), gc, 0.0))
    lf_r_hi, lf_r_lo = _split_bf16(jnp.where((subl >= H) & (subl < 2 * H), gr, 0.0))
    lower = causal.astype(BF16)
    upper = (row <= col).astype(BF16)
    bc_all = (jnp.dot(lower, lf_c_hi, preferred_element_type=F32)
              + jnp.dot(lower, lf_c_lo, preferred_element_type=F32))
    br_all = (jnp.dot(lf_r_hi, upper, preferred_element_type=F32)
              + jnp.dot(lf_r_lo, upper, preferred_element_type=F32))
    ones_col = (lax.broadcasted_iota(jnp.int32, (L, LANES), 1) == 0).astype(BF16)
    y = jnp.zeros((L, wout_ref.shape[1]), F32)

    for h in range(H):
        sl = slice(h * dh, (h + 1) * dh)
        q = q_ref[:, sl]
        kt = kt_ref[sl, :]
        vext = jnp.concatenate([v_ref[:, sl], ones_col], axis=1)
        ig_r = gr[h:h + 1, :]
        b_c = bc_all[:, H + h:H + h + 1]
        b_r = br_all[H + h:H + h + 1, :]
        m_prev = m_ref[h:h + 1, 0:1]

        log_d = jnp.where(causal, b_c - b_r + ig_r, -jnp.inf)
        log_inter = b_c + m_prev
        m_t = jnp.maximum(log_inter, jnp.max(log_d, axis=1, keepdims=True))
        dmat = jnp.exp(log_d - m_t)
        inter = jnp.exp(log_inter - m_t)
        s = jnp.dot(q, kt, preferred_element_type=F32) * dmat
        numden = (jnp.dot(s.astype(BF16), vext, preferred_element_type=F32)
                  + inter * jnp.dot(q, cb_ref[h], preferred_element_type=F32))
        num = numden[:, :dh]
        den = numden[:, dh:dh + 1]
        hh = num / jnp.maximum(jnp.abs(den), jnp.exp(-m_t))

        b_last = b_r[:, L - 1:L]
        a_r = b_last - b_r + ig_r
        m_new = jnp.maximum(b_last + m_prev, jnp.max(a_r, axis=1, keepdims=True))
        w_r = jnp.exp(a_r - m_new)
        decay = jnp.exp(b_last + m_prev - m_new)
        ktw = (kt.astype(F32) * w_r).astype(BF16)
        c_new = decay * cext_ref[h] + jnp.dot(ktw, vext, preferred_element_type=F32)
        cext_ref[h] = c_new
        cb_ref[h] = c_new.astype(BF16)
        m_ref[h:h + 1, :] = jnp.broadcast_to(m_new, (1, LANES))

        hn = hh * lax.rsqrt(jnp.mean(hh * hh, axis=1, keepdims=True) + EPS)
        hg = (hn * hg_ref[:, sl] * gate_ref[:, sl].astype(F32)).astype(BF16)
        y = y + jnp.dot(hg, wout_ref[sl, :], preferred_element_type=F32)

    xo_ref[...] = x_ref[...] + y

    @pl.when(c == n_chunks - 1)
    def _():
        for h in range(H):
            cout_ref[h] = cext_ref[h, :, :dh]
            nout_ref[h:h + 1, :] = cext_ref[h, :, dh:].T[0:1, :]
        mout_ref[...] = m_ref[...]


def _scan(p3, col_idx, kt3, gc, gr, x3, wout, hgain, state, *, chunk, n_heads):
    b, t, _ = p3.shape
    dh = kt3.shape[1] // n_heads
    di = n_heads * dh
    d = x3.shape[2]
    nc = t // chunk
    assert t % chunk == 0
    has_state = state is not None
    qi, vi, gi = col_idx

    def pspec(ci):
        return pl.BlockSpec((None, chunk, di), lambda bi, c, ci=ci: (bi, c, ci))

    in_specs = [
        pspec(qi),
        pl.BlockSpec((None, di, chunk), lambda bi, c: (bi, 0, c)),
        pspec(vi), pspec(gi),
        pl.BlockSpec((None, chunk, LANES), lambda bi, c: (bi, c, 0)),
        pl.BlockSpec((None, 8, chunk), lambda bi, c: (bi, 0, c)),
        pl.BlockSpec((None, chunk, d), lambda bi, c: (bi, c, 0)),
        pl.BlockSpec((di, d), lambda bi, c: (0, 0)),
        pl.BlockSpec((1, di), lambda bi, c: (0, 0)),
    ]
    args = [p3, kt3, p3, p3, gc, gr, x3, wout, hgain]
    if has_state:
        c0, n0, m0 = state
        in_specs += [
            pl.BlockSpec((None, n_heads, dh, dh), lambda bi, c: (bi, 0, 0, 0)),
            pl.BlockSpec((None, n_heads, dh), lambda bi, c: (bi, 0, 0)),
            pl.BlockSpec((None, 8, LANES), lambda bi, c: (bi, 0, 0)),
        ]
        args += [c0, n0, m0]
    return pl.pallas_call(
        functools.partial(_scan_kernel, chunk=chunk, n_heads=n_heads, dh=dh,
                          n_chunks=nc, has_state=has_state),
        grid=(b, nc),
        in_specs=in_specs,
        out_specs=[
            pl.BlockSpec((None, chunk, d), lambda bi, c: (bi, c, 0)),
            pl.BlockSpec((None, n_heads, dh, dh), lambda bi, c: (bi, 0, 0, 0)),
            pl.BlockSpec((None, n_heads, dh), lambda bi, c: (bi, 0, 0)),
            pl.BlockSpec((None, 8, LANES), lambda bi, c: (bi, 0, 0)),
        ],
        out_shape=[
            jax.ShapeDtypeStruct((b, t, d), F32),
            jax.ShapeDtypeStruct((b, n_heads, dh, dh), F32),
            jax.ShapeDtypeStruct((b, n_heads, dh), F32),
            jax.ShapeDtypeStruct((b, 8, LANES), F32),
        ],
        scratch_shapes=[
            pltpu.VMEM((n_heads, dh, dh + LANES), F32),
            pltpu.VMEM((n_heads, dh, dh + LANES), BF16),
            pltpu.VMEM((8, LANES), F32),
        ],
        compiler_params=_params(("parallel", "arbitrary")),
        name="mlstm_scan",
    )(*args)


def _headnorm(a):
    n = a.shape[1]
    r = lax.broadcasted_iota(jnp.int32, (MXU_DIM, MXU_DIM), 0)
    c = lax.broadcasted_iota(jnp.int32, (MXU_DIM, MXU_DIM), 1)
    same_head = (lax.shift_right_logical(r, HEAD_SHIFT)
                 == lax.shift_right_logical(c, HEAD_SHIFT)).astype(BF16)
    parts = []
    for c0 in range(0, n, MXU_DIM):
        blk = a[:, c0:c0 + MXU_DIM]
        ss = jnp.dot((blk * blk).astype(BF16), same_head, preferred_element_type=F32)
        parts.append(blk * lax.rsqrt(ss * (1.0 / ATTN_HEAD_DIM) + EPS))
    return jnp.concatenate(parts, axis=1)


def _store_by_residue(ref, val, scr, dil):
    rows, width = val.shape
    if dil == 1:
        ref[0] = val.astype(BF16)
        return
    for c in range(width // LANES):
        scr[c] = val[:, c * LANES:(c + 1) * LANES]
    for r in range(dil):
        parts = [scr[c, pl.ds(r, rows // dil, stride=dil), :] for c in range(width // LANES)]
        ref[r] = jnp.concatenate(parts, axis=1).astype(BF16)


def _proj_headnorm_kernel(x_ref, g_ref, w_ref, hg_ref, a_ref, r_ref, *row_refs, n_norm, scale):
    gw = GROUP_HEADS * ATTN_HEAD_DIM
    xf = x_ref[...]
    xn = (xf * _rms_scale(xf) * g_ref[...]).astype(BF16)
    p = jnp.dot(xn, w_ref[...], preferred_element_type=F32)
    a = _headnorm(p[:, :n_norm]) * hg_ref[...]
    if scale != 1.0:
        a = a * scale
    r = p[:, n_norm:]
    a_ref[...] = a.astype(BF16)
    r_ref[...] = r.astype(BF16)
    for g, row_ref in enumerate(row_refs):
        cs = slice(g * gw, (g + 1) * gw)
        row_ref[:, :gw] = a[:, cs]
        row_ref[:, gw:] = r[:, cs]


def _proj_headnorm(x2, g, w, hgain, *, n_norm, scale, group_rows=False):
    m, d = x2.shape
    n = w.shape[1]
    gw = GROUP_HEADS * ATTN_HEAD_DIM
    assert n_norm % MXU_DIM == 0 and m % 16 == 0
    whole = lambda shape: pl.BlockSpec(shape, lambda i: (0, 0))
    out_specs = [whole((m, n_norm)), whole((m, n - n_norm))]
    out_shape = [jax.ShapeDtypeStruct((m, n_norm), BF16), jax.ShapeDtypeStruct((m, n - n_norm), BF16)]
    if group_rows:
        assert n - n_norm == n_norm
        for _ in range(n_norm // gw):
            out_specs.append(whole((m, 2 * gw)))
            out_shape.append(jax.ShapeDtypeStruct((m, 2 * gw), F32))
    return pl.pallas_call(
        functools.partial(_proj_headnorm_kernel, n_norm=n_norm, scale=scale),
        grid=(1,),
        in_specs=[whole((m, d)), whole((1, d)), whole((d, n)), whole((1, n_norm))],
        out_specs=out_specs,
        out_shape=out_shape,
        compiler_params=_params(("arbitrary",)),
        name="proj_headnorm",
    )(x2, g, w, hgain)


def _kvq_proj_kernel(x_ref, gkv_ref, wkv_ref, kg_ref, gb_ref, wb_ref, qg_ref, *refs,
                     qscale, tails, tm, dils):
    ng = len(dils)
    gw = GROUP_HEADS * ATTN_HEAD_DIM
    n_norm = ng * gw
    k_refs, v_refs, q_refs = refs[:ng], refs[ng:2 * ng], refs[2 * ng:3 * ng]
    z_ref = refs[3 * ng]
    tail_refs = refs[3 * ng + 1:3 * ng + 1 + len(tails)]
    scr = refs[-1]
    xf = x_ref[...]
    xhat = xf * _rms_scale(xf)
    xkv = (xhat * gkv_ref[...]).astype(BF16)
    xb = (xhat * gb_ref[...]).astype(BF16)
    for g, dil in enumerate(dils):
        cs = slice(g * gw, (g + 1) * gw)
        k = _headnorm(jnp.dot(xkv, wkv_ref[:, cs], preferred_element_type=F32)) * kg_ref[:, cs]
        v = jnp.dot(xkv, wkv_ref[:, n_norm + g * gw:n_norm + (g + 1) * gw],
                    preferred_element_type=F32)
        q = (_headnorm(jnp.dot(xb, wb_ref[:, cs], preferred_element_type=F32))
             * qg_ref[:, cs] * qscale)
        _store_by_residue(k_refs[g], k, scr, dil)
        _store_by_residue(v_refs[g], v, scr, dil)
        _store_by_residue(q_refs[g], q, scr, dil)
        rows, t_ref = tails[g][0], tail_refs[g]
        t_ref[:, :gw] = k[max(tm - rows, 0):, :]
        t_ref[:, gw:] = v[max(tm - rows, 0):, :]
    z_ref[...] = jnp.dot(xb, wb_ref[:, n_norm:], preferred_element_type=F32).astype(BF16)


def _kvq_proj(x3, gkv, wkv, kgain, gb, wb, qgain, *, qscale, tm, tail_rows, dils):
    b, t, d = x3.shape
    gw = GROUP_HEADS * ATTN_HEAD_DIM
    ng = len(dils)
    assert t % tm == 0 and wkv.shape[1] == 2 * ng * gw and wb.shape[1] > ng * gw
    n_tiles = t // tm
    out_specs, out_shape = [], []
    for _ in range(3):
        for dil in dils:
            assert tm % (dil * 16) == 0
            out_specs.append(pl.BlockSpec((None, dil, tm // dil, gw), lambda bi, i: (bi, 0, i, 0)))
            out_shape.append(jax.ShapeDtypeStruct((b, dil, t // dil, gw), BF16))
    zw = wb.shape[1] - ng * gw
    out_specs.append(pl.BlockSpec((None, tm, zw), lambda bi, i: (bi, i, 0)))
    out_shape.append(jax.ShapeDtypeStruct((b, t, zw), BF16))
    tails = []
    for rows in tail_rows:
        if rows >= tm:
            assert rows % tm == 0
            first, blk = n_tiles - rows // tm, tm
        else:
            first, blk = n_tiles - 1, rows
        tails.append((rows, first))
        out_specs.append(pl.BlockSpec(
            (None, blk, 2 * gw), lambda bi, i, first=first: (bi, jnp.maximum(i - first, 0), 0)))
        out_shape.append(jax.ShapeDtypeStruct((b, rows, 2 * gw), F32))
    const = lambda shape: pl.BlockSpec(shape, lambda bi, i: (0, 0))
    outs = pl.pallas_call(
        functools.partial(_kvq_proj_kernel, qscale=qscale, tails=tuple(tails), tm=tm, dils=dils),
        grid=(b, n_tiles),
        in_specs=[
            pl.BlockSpec((None, tm, d), lambda bi, i: (bi, i, 0)),
            const((1, d)), const(wkv.shape), const((1, ng * gw)),
            const((1, d)), const(wb.shape), const((1, ng * gw)),
        ],
        out_specs=out_specs,
        out_shape=out_shape,
        scratch_shapes=[pltpu.VMEM((gw // LANES, tm, LANES), F32)],
        compiler_params=_params(("parallel", "arbitrary")),
        name="kvq_proj",
    )(x3, gkv, wkv, kgain, gb, wb, qgain)
    return outs[:ng], outs[ng:2 * ng], outs[2 * ng:3 * ng], outs[3 * ng], outs[3 * ng + 1:]


def _attn_prompt_kernel(q_ref, kp_ref, kc_ref, vp_ref, vc_ref, bias_ref, o_ref, lse_ref, *, nq):
    blk = ATTN_BLOCK
    first = jnp.where(pl.program_id(2) == 0, 0, 1)
    lane = lax.broadcasted_iota(jnp.int32, (blk, LANES), 1)
    low_half = lane < ATTN_HEAD_DIM
    ones_tile = jnp.ones((2 * blk, LANES), BF16)
    for rr, jb in [(rr, jb) for rr in range(q_ref.shape[0]) for jb in range(nq)]:
        q = q_ref[rr, jb * blk:(jb + 1) * blk, :]
        if jb == 0:
            kcat = jnp.concatenate([kp_ref[rr], kc_ref[rr, 0:blk, :]], axis=0)
            vcat = jnp.concatenate([vp_ref[rr], vc_ref[rr, 0:blk, :]], axis=0)
        else:
            kcat = kc_ref[rr, (jb - 1) * blk:(jb + 1) * blk, :]
            vcat = vc_ref[rr, (jb - 1) * blk:(jb + 1) * blk, :]
        ms, outs = [], []
        sums = jnp.ones((blk, LANES), F32)
        for j in range(GROUP_HEADS // 2):
            ps = slice(j * LANES, (j + 1) * LANES)
            qf = q[:, ps].astype(F32)
            kpair = kcat[:, ps]
            vext = jnp.concatenate([vcat[:, ps], ones_tile], axis=1)
            pair = []
            for half in range(2):
                h = 2 * j + half
                qh = jnp.where(low_half if half == 0 else ~low_half, qf, 0.0).astype(BF16)
                bias = bias_ref[first, h] if jb == 0 else bias_ref[1, h]
                st = lax.dot_general(kpair, qh, NT_DIMS, preferred_element_type=F32) + bias
                m = jnp.max(st, axis=0, keepdims=True)
                p = jnp.exp(st - m).T.astype(BF16)
                pv = jnp.dot(p, vext, preferred_element_type=F32)
                l = pv[:, LANES:]
                pair.append(pv[:, :LANES] * (1.0 / l))
                sums = jnp.where(lane == h, l, sums)
                ms.append(m)
            outs.append(jnp.where(low_half, pair[0], pair[1]))
        o_ref[rr, jb * blk:(jb + 1) * blk, :] = jnp.concatenate(outs, axis=1).astype(o_ref.dtype)
        m_t = jnp.concatenate(ms + [jnp.zeros((LANES - GROUP_HEADS, blk), F32)], axis=0).T
        lse_ref[rr, jb * blk:(jb + 1) * blk, :] = m_t + jnp.log(sums)


def _attn_prompt(q, k, v, bias, g):
    b, dil, s, gw = q.shape
    nb = s // ATTN_BLOCK
    assert s % ATTN_BLOCK == 0
    nq = min(ATTN_BLOCKS_PER_STEP, nb)
    rps = min(ATTN_BLOCKS_PER_STEP // nq, dil)
    assert nb % nq == 0 and dil % rps == 0
    rows = nq * ATTN_BLOCK
    cur = pl.BlockSpec((None, rps, rows, gw), lambda bi, r, j: (bi, r, j, 0))
    prev = pl.BlockSpec((None, rps, ATTN_BLOCK, gw),
                        lambda bi, r, j: (bi, r, jnp.maximum(j * nq - 1, 0), 0))
    return pl.pallas_call(
        functools.partial(_attn_prompt_kernel, nq=nq),
        grid=(b, dil // rps, nb // nq),
        in_specs=[
            cur, prev, cur, prev, cur,
            pl.BlockSpec(bias.shape, lambda bi, r, j: (0, 0, 0, 0)),
        ],
        out_specs=[
            pl.BlockSpec((None, rps, rows, gw), lambda bi, r, j: (bi, r, j, 0)),
            pl.BlockSpec((None, rps, rows, LANES), lambda bi, r, j: (bi, r, j, 0)),
        ],
        out_shape=[
            jax.ShapeDtypeStruct((b, dil, s, gw), BF16),
            jax.ShapeDtypeStruct((b, dil, s, LANES), F32),
        ],
        compiler_params=_params(("parallel", "parallel", "arbitrary")),
        name="attn_prompt_g%d" % g,
    )(q, k, k, v, v, bias)


def _head_expand_matrix():
    r = lax.broadcasted_iota(jnp.int32, (LANES, GROUP_HEADS * ATTN_HEAD_DIM), 0)
    c = lax.broadcasted_iota(jnp.int32, (LANES, GROUP_HEADS * ATTN_HEAD_DIM), 1)
    return (r == lax.shift_right_logical(c, HEAD_SHIFT)).astype(BF16)


def _load_token_order(ref, scr):
    dil, per, width = ref.shape
    if dil == 1:
        return ref[0].astype(F32)
    n_tiles = width // LANES
    for r in range(dil):
        val = ref[r].astype(F32)
        for c in range(n_tiles):
            scr[c, pl.ds(r, per, stride=dil), :] = val[:, c * LANES:(c + 1) * LANES]
    return jnp.concatenate([scr[c] for c in range(n_tiles)], axis=1)


def _merge_out_kernel(o0_ref, o1_ref, o2_ref, l0_ref, l1_ref, l2_ref, z_ref, x_ref, w_ref, y_ref,
                      *scratch):
    o_refs, l_refs = (o0_ref, o1_ref, o2_ref), (l0_ref, l1_ref, l2_ref)
    os_, ls, k = [], [], 0
    for o_ref, l_ref in zip(o_refs, l_refs):
        if o_ref.shape[0] == 1:
            os_.append(o_ref[0].astype(F32))
            ls.append(l_ref[0])
        else:
            os_.append(_load_token_order(o_ref, scratch[k]))
            ls.append(_load_token_order(l_ref, scratch[k + 1]))
            k += 2
    lmax = jnp.maximum(jnp.maximum(ls[0], ls[1]), ls[2])
    es = [jnp.exp(l - lmax) for l in ls]
    tot = es[0] + es[1] + es[2]
    expand = _head_expand_matrix()
    o = jnp.zeros(os_[0].shape, F32)
    for e, og in zip(es, os_):
        wexp = jnp.dot((e / tot).astype(BF16), expand, preferred_element_type=F32)
        o = o + wexp * og
    zf = z_ref[...].astype(F32)
    a = (o * (zf * _sigmoid(zf))).astype(BF16)
    y_ref[...] = x_ref[...] + jnp.dot(a, w_ref[...], preferred_element_type=F32)


def _merge_out(outs, lses, z3, x3, w, *, tm):
    b, t, d = x3.shape
    gw = w.shape[0]
    assert t % tm == 0
    row = lambda width: pl.BlockSpec((None, tm, width), lambda bi, i: (bi, i, 0))
    by_residue = lambda a: pl.BlockSpec((None, a.shape[1], tm // a.shape[1], a.shape[3]),
                                        lambda bi, i: (bi, 0, i, 0))
    scratch = []
    for o in outs:
        if o.shape[1] > 1:
            scratch += [pltpu.VMEM((gw // LANES, tm, LANES), F32), pltpu.VMEM((1, tm, LANES), F32)]
    return pl.pallas_call(
        _merge_out_kernel,
        grid=(b, t // tm),
        in_specs=[by_residue(o) for o in outs] + [by_residue(l) for l in lses]
        + [row(gw), row(d), pl.BlockSpec((gw, d), lambda bi, i: (0, 0))],
        out_specs=row(d),
        out_shape=jax.ShapeDtypeStruct((b, t, d), F32),
        scratch_shapes=scratch,
        compiler_params=_params(("parallel", "parallel")),
        name="merge_out",
    )(*outs, *lses, z3, x3, w)


def _attn_sample_kernel(q_ref, kn_ref, vn_ref, z_ref, c0_ref, c1_ref, c2_ref,
                        b0_ref, b1_ref, b2_ref, a_ref, ks0, vs0, ks1, vs1, ks2, vs2, *, s_new):
    gw = GROUP_HEADS * ATTN_HEAD_DIM
    rows = GROUP_HEADS * s_new
    r = lax.broadcasted_iota(jnp.int32, (rows, gw), 0)
    c = lax.broadcasted_iota(jnp.int32, (rows, gw), 1)
    head_mask = (lax.shift_right_logical(r, int(math.log2(s_new)))
                 == lax.shift_right_logical(c, HEAD_SHIFT))
    caches = ((c0_ref, b0_ref, ks0, vs0), (c1_ref, b1_ref, ks1, vs1), (c2_ref, b2_ref, ks2, vs2))
    pad_rows = jnp.zeros((LANES - s_new, gw), F32)
    outs, lses = [], []
    for g, (c_ref, b_ref, ks, vs) in enumerate(caches):
        buf_len = c_ref.shape[2]
        cs = slice(g * gw, (g + 1) * gw)
        ks[:, :buf_len] = c_ref[0].astype(BF16)
        vs[:, :buf_len] = c_ref[1].astype(BF16)
        ks[:, buf_len:] = jnp.concatenate([kn_ref[:, cs].astype(F32), pad_rows], axis=0).T.astype(BF16)
        vs[:, buf_len:] = jnp.concatenate([vn_ref[:, cs].astype(F32), pad_rows], axis=0).T.astype(BF16)
        qg = q_ref[:, cs].astype(F32)
        qbd = jnp.where(head_mask, jnp.concatenate([qg] * GROUP_HEADS, axis=0), 0.0).astype(BF16)
        s = jnp.dot(qbd, ks[...], preferred_element_type=F32) + b_ref[...]
        m = jnp.max(s, axis=1, keepdims=True)
        p = jnp.exp(s - m)
        l = jnp.sum(p, axis=1, keepdims=True)
        outs.append(lax.dot_general(p.astype(BF16), vs[...], NT_DIMS, preferred_element_type=F32) / l)
        lses.append(m + jnp.log(l))
    lmax = jnp.maximum(jnp.maximum(lses[0], lses[1]), lses[2])
    es = [jnp.exp(l - lmax) for l in lses]
    tot = es[0] + es[1] + es[2]
    o = jnp.zeros((rows, gw), F32)
    for e, og in zip(es, outs):
        o = o + (e / tot) * og
    o = jnp.where(head_mask, o, 0.0)
    folded = o[0:s_new, :]
    for h in range(1, GROUP_HEADS):
        folded = folded + o[h * s_new:(h + 1) * s_new, :]
    zf = z_ref[...].astype(F32)
    a_ref[...] = (folded * (zf * _sigmoid(zf))).astype(BF16)


def _attn_sample(q, kn, vn, z, caches, biases):
    b, s_new, qw = q.shape
    gw = GROUP_HEADS * ATTN_HEAD_DIM
    assert s_new % 8 == 0
    cache2 = [jnp.transpose(cb, (0, 2, 3, 4, 1)).reshape(b, 2, gw, cb.shape[1]) for cb in caches]
    in_specs = [
        pl.BlockSpec((None, s_new, qw), lambda bi: (bi, 0, 0)),
        pl.BlockSpec((None, s_new, qw), lambda bi: (bi, 0, 0)),
        pl.BlockSpec((None, s_new, qw), lambda bi: (bi, 0, 0)),
        pl.BlockSpec((None, s_new, gw), lambda bi: (bi, 0, 0)),
    ]
    for cb in cache2:
        in_specs.append(pl.BlockSpec((None, 2, gw, cb.shape[3]), lambda bi: (bi, 0, 0, 0)))
    for bt in biases:
        in_specs.append(pl.BlockSpec(bt.shape, lambda bi: (0, 0)))
    scratch = []
    for cb in cache2:
        scratch += [pltpu.VMEM((gw, cb.shape[3] + LANES), BF16)] * 2
    return pl.pallas_call(
        functools.partial(_attn_sample_kernel, s_new=s_new),
        grid=(b,),
        in_specs=in_specs,
        out_specs=pl.BlockSpec((None, s_new, gw), lambda bi: (bi, 0, 0)),
        out_shape=jax.ShapeDtypeStruct((b, s_new, gw), BF16),
        scratch_shapes=scratch,
        compiler_params=_params(("arbitrary",)),
        name="attn_sample",
    )(q, kn, vn, z, *cache2, *biases)


def _matmul_residual_kernel(a_ref, x_ref, w_ref, y_ref):
    y_ref[...] = x_ref[...] + jnp.dot(a_ref[...], w_ref[...], preferred_element_type=F32)


def _matmul_residual(a2, x2, w, *, tm):
    m, d = x2.shape
    kdim = a2.shape[1]
    assert m % tm == 0
    return pl.pallas_call(
        _matmul_residual_kernel,
        grid=(m // tm,),
        in_specs=[
            pl.BlockSpec((tm, kdim), lambda i: (i, 0)),
            pl.BlockSpec((tm, d), lambda i: (i, 0)),
            pl.BlockSpec((kdim, d), lambda i: (0, 0)),
        ],
        out_specs=pl.BlockSpec((tm, d), lambda i: (i, 0)),
        out_shape=jax.ShapeDtypeStruct((m, d), F32),
        compiler_params=_params(("parallel",)),
        name="matmul_residual",
    )(a2, x2, w)


def _t5_bucket_np(dist):
    exact = N_BUCKETS // 2
    d = np.maximum(dist, 1).astype(np.float32)
    large = exact + (np.log(d / np.float32(exact)) / np.float32(math.log(MAX_DISTANCE / exact))
                     * np.float32(N_BUCKETS - exact)).astype(np.int32)
    return np.where(dist < exact, dist, np.minimum(large, N_BUCKETS - 1)).astype(np.int32)


def _bias_by_step(rel_bias_g, jmax, dil):
    return rel_bias_g.astype(F32)[_t5_bucket_np(np.arange(jmax + 1) * dil)]


def _prompt_bias(rel_bias_g, win, dil):
    jmax = win // dil
    assert jmax == ATTN_BLOCK
    qi = np.arange(ATTN_BLOCK)[:, None]
    kj = np.arange(2 * ATTN_BLOCK)[None, :]
    rel = qi + ATTN_BLOCK - kj
    band = (rel >= 0) & (rel <= jmax)
    bvec = _bias_by_step(rel_bias_g, jmax, dil)
    period = 2 * ATTN_BLOCK + 1
    base = jnp.concatenate([bvec[::-1], jnp.zeros((period - jmax - 1, GROUP_HEADS), F32)], axis=0).T
    bias = jnp.tile(base, (1, ATTN_BLOCK))[:, :2 * ATTN_BLOCK * ATTN_BLOCK]
    bias = bias.reshape(GROUP_HEADS, ATTN_BLOCK, 2 * ATTN_BLOCK)
    rest = jnp.where(band[None], bias, -jnp.inf)
    first = jnp.where((band & (kj >= ATTN_BLOCK))[None], bias, -jnp.inf)
    return jnp.swapaxes(jnp.stack([first, rest], axis=0), 2, 3)


def _sample_bias(rel_bias_g, win, dil, buf_len, s_new):
    jmax = win // dil
    assert buf_len == jmax * dil
    width = buf_len + LANES
    bvec = _bias_by_step(rel_bias_g, jmax, dil)
    gaps = jnp.full((jmax + 1, dil - 1, GROUP_HEADS), -jnp.inf, F32)
    by_dist = jnp.concatenate([bvec[:, None, :], gaps], axis=1).reshape((jmax + 1) * dil, GROUP_HEADS)
    padded = jnp.pad(by_dist[::-1], ((s_new, width), (0, 0)), constant_values=-jnp.inf)
    rows = [padded[dil - 1 - s + s_new:dil - 1 - s + s_new + width] for s in range(s_new)]
    table = jnp.transpose(jnp.stack(rows, axis=0), (2, 0, 1))
    return table.reshape(GROUP_HEADS * s_new, width)


def _layer_a(x3, state, weights, *, chunk, pad_to):
    norm_a, wt, wgt, bg, hgain, wout = weights
    b, t, d = x3.shape
    H = MLSTM_HEADS
    di = wout.shape[0]
    dh = di // H
    k_scale = dh ** -0.5
    k_idx = 1
    if t > SHORT_SCAN_CHUNK:
        assert pad_to == t
        p3, gc3, gr3, kt3 = _inproj(x3, norm_a, wt, wgt, bg, tm=min(INPROJ_ROWS, t), tn=di,
                                    n_heads=H, k_transposed=True, k_scale=k_scale)
        col_idx = (0, 1, 2)
        xin = x3
    else:
        m = b * t
        extra = pad_to - t
        p, gc, gr = _inproj(x3.reshape(1, m, d), norm_a, wt, wgt, bg, tm=m, tn=di, n_heads=H,
                            k_transposed=False, k_scale=k_scale)
        col_idx = (0, 2, 3)
        p3 = p.reshape(b, t, -1)
        k3 = p3[:, :, k_idx * di:(k_idx + 1) * di]
        kt3 = jnp.pad(jnp.swapaxes(k3, 1, 2), ((0, 0), (0, 0), (0, extra)))
        p3 = jnp.pad(p3, ((0, 0), (0, extra), (0, 0)))
        xin = jnp.pad(x3, ((0, 0), (0, extra), (0, 0)))
        lane = np.arange(LANES)
        pad_col = np.where(lane < H, -np.inf, 0.0).astype(np.float32)
        gc3 = jnp.concatenate([gc.reshape(b, t, LANES),
                               jnp.broadcast_to(pad_col, (b, extra, LANES))], axis=1)
        pad_row = np.where(np.arange(8) < H, -np.inf, 0.0).astype(np.float32)[:, None, None]
        gr3 = jnp.swapaxes(jnp.concatenate([gr.reshape(8, b, t),
                                            jnp.broadcast_to(pad_row, (8, b, extra))], axis=2), 0, 1)
    xo, c_out, n_out, m_out = _scan(p3, col_idx, kt3, gc3, gr3, xin, wout, hgain, state,
                                    chunk=chunk, n_heads=H)
    return xo[:, :t], c_out, n_out, m_out[:, :H, 0]


def kernel(x_prompt, x_sample, state_mlstm_C, state_mlstm_n, state_mlstm_m, cache_kv_w128, cache_kv_w512, cache_kv_w2048, norm_a, w_in_a, b_gates_a, hnorm_a, w_out_a, norm_kv, w_kv, k_norm, norm_b, w_in_b, q_norm, rel_bias, w_out_b):
    H = MLSTM_HEADS
    bp, tp, d = x_prompt.shape
    bs, ts, _ = x_sample.shape
    di = w_out_a.shape[1]
    dh = di // H
    gw = GROUP_HEADS * ATTN_HEAD_DIM
    qw = len(GROUPS) * gw
    caches = (cache_kv_w128, cache_kv_w512, cache_kv_w2048)
    assert norm_a.shape[0] == 1 and norm_b.shape[0] == 1, "one mLSTM layer, one attention layer"
    for cb, (win, _) in zip(caches, GROUPS):
        assert cb.shape[1] == win, "window buffers must hold a full window"

    w_at = w_in_a[0].T
    wt_a = w_at[:5 * di].astype(BF16)
    wgt = jnp.pad(w_at[5 * di:].astype(F32), ((0, LANES - 2 * H), (0, 0)))
    bg = jnp.pad(b_gates_a[0].astype(F32), (0, LANES - 2 * H))[None, :]
    na = norm_a[0].astype(F32)[None, :]
    hgain = hnorm_a[0].astype(F32)[None, :]
    wout_a = w_out_a[0].astype(BF16)
    weights_a = (na, wt_a, wgt, bg, hgain, wout_a)

    xp1, c_p, n_p, m_p = _layer_a(x_prompt, None, weights_a, chunk=SCAN_CHUNK, pad_to=tp)
    m0 = jnp.pad(jnp.broadcast_to(state_mlstm_m[0].astype(F32)[:, :, None], (bs, H, LANES)),
                 ((0, 0), (0, 8 - H), (0, 0)))
    state_s = (state_mlstm_C[0].astype(F32), state_mlstm_n[0].astype(F32), m0)
    assert ts <= SHORT_SCAN_CHUNK
    xs1, c_s, n_s, m_s = _layer_a(x_sample, state_s, weights_a, chunk=SHORT_SCAN_CHUNK,
                                  pad_to=SHORT_SCAN_CHUNK)

    nkv = norm_kv.astype(F32)[None, :]
    wkv = w_kv.astype(BF16)
    kgain = jnp.tile(k_norm.astype(F32), qw // ATTN_HEAD_DIM)[None, :]
    rows_p = [min(win, tp) for win, _ in GROUPS]
    dils = tuple(dil for _, dil in GROUPS)
    nb_ = norm_b[0].astype(F32)[None, :]
    wb = w_in_b[0].astype(BF16)
    qgain = jnp.tile(q_norm[0].astype(F32), qw // ATTN_HEAD_DIM)[None, :]
    qscale = ATTN_HEAD_DIM ** -0.5
    kp, vp, qp, zp, (kv128_p, kv512_p, kv2048_p) = _kvq_proj(
        xp1, nkv, wkv, kgain, nb_, wb, qgain, qscale=qscale, tm=KVQ_ROWS, tail_rows=rows_p,
        dils=dils)
    xs1_flat = xs1.reshape(bs * ts, d)
    by_seq = lambda a: a.reshape(bs, ts, a.shape[-1])
    ks, vs, kv128_s, kv512_s, kv2048_s = map(by_seq, _proj_headnorm(
        xs1_flat, nkv, wkv, kgain, n_norm=qw, scale=1.0, group_rows=True))

    wout_b = w_out_b[0].astype(BF16)
    qs, zs = map(by_seq, _proj_headnorm(xs1_flat, nb_, wb, qgain, n_norm=qw, scale=qscale))

    outs, lses = [], []
    for g, (win, dil) in enumerate(GROUPS):
        bias = _prompt_bias(rel_bias[:, g * GROUP_HEADS:(g + 1) * GROUP_HEADS], win, dil)
        o, lse = _attn_prompt(qp[g], kp[g], vp[g], bias, g)
        outs.append(o)
        lses.append(lse)
    y_p = _merge_out(outs, lses, zp, xp1, wout_b, tm=MERGE_ROWS)

    sbias = [_sample_bias(rel_bias[:, g * GROUP_HEADS:(g + 1) * GROUP_HEADS], win, dil,
                          caches[g].shape[1], ts) for g, (win, dil) in enumerate(GROUPS)]
    a_s = _attn_sample(qs, ks, vs, zs, caches, sbias)
    y_s = _matmul_residual(a_s.reshape(bs * ts, gw), xs1.reshape(bs * ts, d), wout_b,
                           tm=bs * ts).reshape(bs, ts, d)

    kv5 = lambda a: a.reshape(a.shape[0], a.shape[1], 2, GROUP_HEADS, ATTN_HEAD_DIM)
    return (y_p, y_s, c_p[None], n_p[None], m_p[None], c_s[None], n_s[None], m_s[None],
            kv5(kv128_p), kv5(kv512_p), kv5(kv2048_p), kv5(kv128_s), kv5(kv512_s), kv5(kv2048_s))
```

```python
import functools
import math

import numpy as np
import jax
import jax.numpy as jnp
from jax import lax
from jax.experimental import pallas as pl
from jax.experimental.pallas import tpu as pltpu

F32 = jnp.float32
BF16 = jnp.bfloat16

EPS = 1e-6
MLSTM_HEADS = 4
GROUPS = ((128, 1), (512, 4), (2048, 16))
GROUP_HEADS = 8
ATTN_HEAD_DIM = 64
ATTN_BLOCK = 128
N_BUCKETS = 32
MAX_DISTANCE = 2048

HEAD_SHIFT = ATTN_HEAD_DIM.bit_length() - 1

LANES = 128
MXU_DIM = 256
VMEM_LIMIT_BYTES = 56 * 1024 * 1024

INPROJ_ROWS = 512
SCAN_CHUNK = 256
SHORT_SCAN_CHUNK = 8
KVQ_ROWS = 512
MERGE_ROWS = 1024
ATTN_BLOCKS_PER_STEP = 16

NT_DIMS = (((1,), (1,)), ((), ()))


def _params(sem):
    return pltpu.CompilerParams(dimension_semantics=sem, vmem_limit_bytes=VMEM_LIMIT_BYTES)


def _rms_scale(xf):
    return lax.rsqrt(jnp.mean(xf * xf, axis=-1, keepdims=True) + EPS)


def _sigmoid(x):
    return 0.5 * jnp.tanh(0.5 * x) + 0.5


def _split_bf16(a):
    hi = a.astype(BF16)
    lo = (a - hi.astype(F32)).astype(BF16)
    return hi, lo


def _inproj_kernel(x_ref, g_ref, wt_ref, wgt_ref, bg_ref, p_ref, gc_ref, gr_ref, *kt_refs,
                   n_heads, tn, k_scale):
    xf = x_ref[...]
    xn = xf * _rms_scale(xf) * g_ref[...]
    xh, xl = _split_bf16(xn)
    wh, wl = _split_bf16(wgt_ref[...])
    gates = (lax.dot_general(xh, wh, NT_DIMS, preferred_element_type=F32)
             + lax.dot_general(xl, wh, NT_DIMS, preferred_element_type=F32)
             + lax.dot_general(xh, wl, NT_DIMS, preferred_element_type=F32)) + bg_ref[...]
    lane = lax.broadcasted_iota(jnp.int32, gates.shape, 1)
    logsig = jnp.minimum(gates, 0.0) - jnp.log(1.0 + jnp.exp(-jnp.abs(gates)))
    gcol = jnp.where(lane < n_heads, gates, jnp.where(lane < 2 * n_heads, logsig, 0.0))
    gc_ref[...] = gcol
    gr_ref[...] = gcol.T[:8, :]

    def tile(j):
        return lax.dot_general(xh, wt_ref[j * tn:(j + 1) * tn, :], NT_DIMS,
                               preferred_element_type=F32)

    q_tile, k_tile, v_tile, o_tile, z_tile = range(5)
    p_ref[:, 0:tn] = tile(q_tile).astype(BF16)
    if kt_refs:
        acc = lax.dot_general(wt_ref[k_tile * tn:(k_tile + 1) * tn, :], xh, NT_DIMS,
                              preferred_element_type=F32)
        kt_refs[0][...] = (acc * k_scale).astype(BF16)
        col = tn
    else:
        p_ref[:, tn:2 * tn] = (tile(k_tile) * k_scale).astype(BF16)
        col = 2 * tn
    p_ref[:, col:col + tn] = tile(v_tile).astype(BF16)
    z = tile(z_tile)
    p_ref[:, col + tn:col + 2 * tn] = (_sigmoid(tile(o_tile)) * (z * _sigmoid(z))).astype(BF16)


def _inproj(x3, g, wt, wgt, bg, *, tm, tn, n_heads, k_transposed, k_scale):
    b, t, d = x3.shape
    assert t % tm == 0 and wt.shape[0] == 5 * tn and 2 * n_heads <= 8
    n_out = (3 if k_transposed else 4) * tn
    out_specs = [
        pl.BlockSpec((None, tm, n_out), lambda bi, i: (bi, i, 0)),
        pl.BlockSpec((None, tm, LANES), lambda bi, i: (bi, i, 0)),
        pl.BlockSpec((None, 8, tm), lambda bi, i: (bi, 0, i)),
    ]
    out_shape = [
        jax.ShapeDtypeStruct((b, t, n_out), BF16),
        jax.ShapeDtypeStruct((b, t, LANES), F32),
        jax.ShapeDtypeStruct((b, 8, t), F32),
    ]
    if k_transposed:
        out_specs.append(pl.BlockSpec((None, tn, tm), lambda bi, i: (bi, 0, i)))
        out_shape.append(jax.ShapeDtypeStruct((b, tn, t), BF16))
    const = lambda shape, **kw: pl.BlockSpec(shape, lambda bi, i: (0, 0), **kw)
    return pl.pallas_call(
        functools.partial(_inproj_kernel, n_heads=n_heads, tn=tn, k_scale=k_scale),
        grid=(b, t // tm),
        in_specs=[
            pl.BlockSpec((None, tm, d), lambda bi, i: (bi, i, 0)),
            const((1, d)),
            const(wt.shape, pipeline_mode=pl.Buffered(1)),
            const((LANES, d)),
            const((1, LANES)),
        ],
        out_specs=out_specs,
        out_shape=out_shape,
        compiler_params=_params(("parallel", "parallel")),
        name="inproj",
    )(x3, g, wt, wgt, bg)


def _scan_kernel(*refs, chunk, n_heads, dh, n_chunks, has_state, fuse_out):
    L, H = chunk, n_heads
    dext = dh + LANES
    (q_ref, kt_ref, v_ref, gate_ref, gc_ref, gr_ref, x_ref, wout_ref, hg_ref) = refs[:9]
    pos = 9
    if has_state:
        c0_ref, n0_ref, m0_ref = refs[pos:pos + 3]
        pos += 3
    xo_ref, cout_ref, nout_ref, mout_ref = refs[pos:pos + 4]
    cext_ref, cb_ref, m_ref = refs[pos + 4:pos + 7]
    c = pl.program_id(1)

    @pl.when(c == 0)
    def _():
        if has_state:
            lane0 = lax.broadcasted_iota(jnp.int32, (dh, LANES), 1) == 0
            for h in range(H):
                ncol = jnp.broadcast_to(n0_ref[h:h + 1, :], (LANES, dh)).T
                cext_ref[h, :, :dh] = c0_ref[h]
                cext_ref[h, :, dh:] = jnp.where(lane0, ncol, 0.0)
            m_ref[...] = m0_ref[...]
        else:
            cext_ref[...] = jnp.zeros(cext_ref.shape, F32)
            m_ref[...] = jnp.zeros(m_ref.shape, F32)
        cb_ref[...] = cext_ref[...].astype(BF16)

    gc = gc_ref[...]
    gr = gr_ref[...]
    row = lax.broadcasted_iota(jnp.int32, (L, L), 0)
    col = lax.broadcasted_iota(jnp.int32, (L, L), 1)
    causal = row >= col
    lane = lax.broadcasted_iota(jnp.int32, gc.shape, 1)
    subl = lax.broadcasted_iota(jnp.int32, gr.shape, 0)
    lf_c_hi, lf_c_lo = _split_bf16(jnp.where((lane >= H) & (lane < 2 * H), gc, 0.0))
    lf_r_hi, lf_r_lo = _split_bf16(jnp.where((subl >= H) & (subl < 2 * H), gr, 0.0))
    lower = causal.astype(BF16)
    upper = (row <= col).astype(BF16)
    bc_all = (jnp.dot(lower, lf_c_hi, preferred_element_type=F32)
              + jnp.dot(lower, lf_c_lo, preferred_element_type=F32))
    br_all = (jnp.dot(lf_r_hi, upper, preferred_element_type=F32)
              + jnp.dot(lf_r_lo, upper, preferred_element_type=F32))
    ones_col = (lax.broadcasted_iota(jnp.int32, (L, LANES), 1) == 0).astype(BF16)
    y = jnp.zeros((L, wout_ref.shape[1]), F32)

    for h in range(H):
        sl = slice(h * dh, (h + 1) * dh)
        q = q_ref[:, sl]
        kt = kt_ref[sl, :]
        vext = jnp.concatenate([v_ref[:, sl], ones_col], axis=1)
        ig_r = gr[h:h + 1, :]
        b_c = bc_all[:, H + h:H + h + 1]
        b_r = br_all[H + h:H + h + 1, :]
        m_prev = m_ref[h:h + 1, 0:1]

        log_d = jnp.where(causal, b_c - b_r + ig_r, -jnp.inf)
        log_inter = b_c + m_prev
        m_t = jnp.maximum(log_inter, jnp.max(log_d, axis=1, keepdims=True))
        dmat = jnp.exp(log_d - m_t)
        inter = jnp.exp(log_inter - m_t)
        s = jnp.dot(q, kt, preferred_element_type=F32) * dmat
        numden = (jnp.dot(s.astype(BF16), vext, preferred_element_type=F32)
                  + inter * jnp.dot(q, cb_ref[h], preferred_element_type=F32))
        num = numden[:, :dh]
        den = numden[:, dh:dh + 1]
        hh = num / jnp.maximum(jnp.abs(den), jnp.exp(-m_t))

        b_last = b_r[:, L - 1:L]
        a_r = b_last - b_r + ig_r
        m_new = jnp.maximum(b_last + m_prev, jnp.max(a_r, axis=1, keepdims=True))
        w_r = jnp.exp(a_r - m_new)
        decay = jnp.exp(b_last + m_prev - m_new)
        ktw = (kt.astype(F32) * w_r).astype(BF16)
        c_new = decay * cext_ref[h] + jnp.dot(ktw, vext, preferred_element_type=F32)
        cext_ref[h] = c_new
        cb_ref[h] = c_new.astype(BF16)
        m_ref[h:h + 1, :] = jnp.broadcast_to(m_new, (1, LANES))

        hn = hh * lax.rsqrt(jnp.mean(hh * hh, axis=1, keepdims=True) + EPS)
        hg = (hn * hg_ref[:, sl] * gate_ref[:, sl].astype(F32)).astype(BF16)
        if fuse_out:
            y = y + jnp.dot(hg, wout_ref[sl, :], preferred_element_type=F32)
        else:
            xo_ref[:, sl] = hg

    if fuse_out:
        xo_ref[...] = x_ref[...] + y

    @pl.when(c == n_chunks - 1)
    def _():
        for h in range(H):
            cout_ref[h] = cext_ref[h, :, :dh]
            nout_ref[h:h + 1, :] = cext_ref[h, :, dh:].T[0:1, :]
        mout_ref[...] = m_ref[...]


def _scan(p3, col_idx, kt3, gc, gr, x3, wout, hgain, state, *, chunk, n_heads, fuse_out):
    b, t, _ = p3.shape
    dh = kt3.shape[1] // n_heads
    di = n_heads * dh
    d = x3.shape[2]
    nc = t // chunk
    assert t % chunk == 0
    has_state = state is not None
    qi, vi, gi = col_idx

    def pspec(ci):
        return pl.BlockSpec((None, chunk, di), lambda bi, c, ci=ci: (bi, c, ci))

    in_specs = [
        pspec(qi),
        pl.BlockSpec((None, di, chunk), lambda bi, c: (bi, 0, c)),
        pspec(vi), pspec(gi),
        pl.BlockSpec((None, chunk, LANES), lambda bi, c: (bi, c, 0)),
        pl.BlockSpec((None, 8, chunk), lambda bi, c: (bi, 0, c)),
        pl.BlockSpec((None, chunk, d), lambda bi, c: (bi, c, 0)),
        pl.BlockSpec((di, d), lambda bi, c: (0, 0)),
        pl.BlockSpec((1, di), lambda bi, c: (0, 0)),
    ]
    args = [p3, kt3, p3, p3, gc, gr, x3, wout, hgain]
    if has_state:
        c0, n0, m0 = state
        in_specs += [
            pl.BlockSpec((None, n_heads, dh, dh), lambda bi, c: (bi, 0, 0, 0)),
            pl.BlockSpec((None, n_heads, dh), lambda bi, c: (bi, 0, 0)),
            pl.BlockSpec((None, 8, LANES), lambda bi, c: (bi, 0, 0)),
        ]
        args += [c0, n0, m0]
    return pl.pallas_call(
        functools.partial(_scan_kernel, chunk=chunk, n_heads=n_heads, dh=dh,
                          n_chunks=nc, has_state=has_state, fuse_out=fuse_out),
        grid=(b, nc),
        in_specs=in_specs,
        out_specs=[
            pl.BlockSpec((None, chunk, d if fuse_out else di), lambda bi, c: (bi, c, 0)),
            pl.BlockSpec((None, n_heads, dh, dh), lambda bi, c: (bi, 0, 0, 0)),
            pl.BlockSpec((None, n_heads, dh), lambda bi, c: (bi, 0, 0)),
            pl.BlockSpec((None, 8, LANES), lambda bi, c: (bi, 0, 0)),
        ],
        out_shape=[
            jax.ShapeDtypeStruct((b, t, d), F32) if fuse_out
            else jax.ShapeDtypeStruct((b, t, di), BF16),
            jax.ShapeDtypeStruct((b, n_heads, dh, dh), F32),
            jax.ShapeDtypeStruct((b, n_heads, dh), F32),
            jax.ShapeDtypeStruct((b, 8, LANES), F32),
        ],
        scratch_shapes=[
            pltpu.VMEM((n_heads, dh, dh + LANES), F32),
            pltpu.VMEM((n_heads, dh, dh + LANES), BF16),
            pltpu.VMEM((8, LANES), F32),
        ],
        compiler_params=_params(("parallel", "arbitrary")),
        name="mlstm_scan",
    )(*args)


def _headnorm(a):
    n = a.shape[1]
    r = lax.broadcasted_iota(jnp.int32, (MXU_DIM, MXU_DIM), 0)
    c = lax.broadcasted_iota(jnp.int32, (MXU_DIM, MXU_DIM), 1)
    same_head = (lax.shift_right_logical(r, HEAD_SHIFT)
                 == lax.shift_right_logical(c, HEAD_SHIFT)).astype(BF16)
    parts = []
    for c0 in range(0, n, MXU_DIM):
        blk = a[:, c0:c0 + MXU_DIM]
        ss = jnp.dot((blk * blk).astype(BF16), same_head, preferred_element_type=F32)
        parts.append(blk * lax.rsqrt(ss * (1.0 / ATTN_HEAD_DIM) + EPS))
    return jnp.concatenate(parts, axis=1)


def _store_by_residue(ref, val, scr, dil):
    rows, width = val.shape
    if dil == 1:
        ref[0] = val.astype(BF16)
        return
    for c in range(width // LANES):
        scr[c] = val[:, c * LANES:(c + 1) * LANES]
    for r in range(dil):
        parts = [scr[c, pl.ds(r, rows // dil, stride=dil), :] for c in range(width // LANES)]
        ref[r] = jnp.concatenate(parts, axis=1).astype(BF16)


def _proj_headnorm_kernel(x_ref, g_ref, w_ref, hg_ref, a_ref, r_ref, *row_refs, n_norm, scale):
    gw = GROUP_HEADS * ATTN_HEAD_DIM
    xf = x_ref[...]
    xn = (xf * _rms_scale(xf) * g_ref[...]).astype(BF16)
    p = jnp.dot(xn, w_ref[...], preferred_element_type=F32)
    a = _headnorm(p[:, :n_norm]) * hg_ref[...]
    if scale != 1.0:
        a = a * scale
    r = p[:, n_norm:]
    a_ref[...] = a.astype(BF16)
    r_ref[...] = r.astype(BF16)
    for g, row_ref in enumerate(row_refs):
        cs = slice(g * gw, (g + 1) * gw)
        row_ref[:, :gw] = a[:, cs]
        row_ref[:, gw:] = r[:, cs]


def _proj_headnorm(x2, g, w, hgain, *, n_norm, scale, group_rows=False):
    m, d = x2.shape
    n = w.shape[1]
    gw = GROUP_HEADS * ATTN_HEAD_DIM
    assert n_norm % MXU_DIM == 0 and m % 16 == 0
    whole = lambda shape: pl.BlockSpec(shape, lambda i: (0, 0))
    out_specs = [whole((m, n_norm)), whole((m, n - n_norm))]
    out_shape = [jax.ShapeDtypeStruct((m, n_norm), BF16), jax.ShapeDtypeStruct((m, n - n_norm), BF16)]
    if group_rows:
        assert n - n_norm == n_norm
        for _ in range(n_norm // gw):
            out_specs.append(whole((m, 2 * gw)))
            out_shape.append(jax.ShapeDtypeStruct((m, 2 * gw), F32))
    return pl.pallas_call(
        functools.partial(_proj_headnorm_kernel, n_norm=n_norm, scale=scale),
        grid=(1,),
        in_specs=[whole((m, d)), whole((1, d)), whole((d, n)), whole((1, n_norm))],
        out_specs=out_specs,
        out_shape=out_shape,
        compiler_params=_params(("arbitrary",)),
        name="proj_headnorm",
    )(x2, g, w, hgain)


def _kvq_proj_kernel(x_ref, gkv_ref, wkv_ref, kg_ref, gb_ref, wb_ref, qg_ref, *refs,
                     qscale, tails, tm, dils):
    ng = len(dils)
    gw = GROUP_HEADS * ATTN_HEAD_DIM
    n_norm = ng * gw
    k_refs, v_refs, q_refs = refs[:ng], refs[ng:2 * ng], refs[2 * ng:3 * ng]
    z_ref = refs[3 * ng]
    tail_refs = refs[3 * ng + 1:3 * ng + 1 + len(tails)]
    scr = refs[-1]
    xf = x_ref[...]
    xhat = xf * _rms_scale(xf)
    xkv = (xhat * gkv_ref[...]).astype(BF16)
    xb = (xhat * gb_ref[...]).astype(BF16)
    for g, dil in enumerate(dils):
        cs = slice(g * gw, (g + 1) * gw)
        k = _headnorm(jnp.dot(xkv, wkv_ref[:, cs], preferred_element_type=F32)) * kg_ref[:, cs]
        v = jnp.dot(xkv, wkv_ref[:, n_norm + g * gw:n_norm + (g + 1) * gw],
                    preferred_element_type=F32)
        q = (_headnorm(jnp.dot(xb, wb_ref[:, cs], preferred_element_type=F32))
             * qg_ref[:, cs] * qscale)
        _store_by_residue(k_refs[g], k, scr, dil)
        _store_by_residue(v_refs[g], v, scr, dil)
        _store_by_residue(q_refs[g], q, scr, dil)
        rows, t_ref = tails[g][0], tail_refs[g]
        t_ref[:, :gw] = k[max(tm - rows, 0):, :]
        t_ref[:, gw:] = v[max(tm - rows, 0):, :]
    z_ref[...] = jnp.dot(xb, wb_ref[:, n_norm:], preferred_element_type=F32).astype(BF16)


def _kvq_proj(x3, gkv, wkv, kgain, gb, wb, qgain, *, qscale, tm, tail_rows, dils):
    b, t, d = x3.shape
    gw = GROUP_HEADS * ATTN_HEAD_DIM
    ng = len(dils)
    assert t % tm == 0 and wkv.shape[1] == 2 * ng * gw and wb.shape[1] > ng * gw
    n_tiles = t // tm
    out_specs, out_shape = [], []
    for _ in range(3):
        for dil in dils:
            assert tm % (dil * 16) == 0
            out_specs.append(pl.BlockSpec((None, dil, tm // dil, gw), lambda bi, i: (bi, 0, i, 0)))
            out_shape.append(jax.ShapeDtypeStruct((b, dil, t // dil, gw), BF16))
    zw = wb.shape[1] - ng * gw
    out_specs.append(pl.BlockSpec((None, tm, zw), lambda bi, i: (bi, i, 0)))
    out_shape.append(jax.ShapeDtypeStruct((b, t, zw), BF16))
    tails = []
    for rows in tail_rows:
        if rows >= tm:
            assert rows % tm == 0
            first, blk = n_tiles - rows // tm, tm
        else:
            first, blk = n_tiles - 1, rows
        tails.append((rows, first))
        out_specs.append(pl.BlockSpec(
            (None, blk, 2 * gw), lambda bi, i, first=first: (bi, jnp.maximum(i - first, 0), 0)))
        out_shape.append(jax.ShapeDtypeStruct((b, rows, 2 * gw), F32))
    const = lambda shape: pl.BlockSpec(shape, lambda bi, i: (0, 0))
    outs = pl.pallas_call(
        functools.partial(_kvq_proj_kernel, qscale=qscale, tails=tuple(tails), tm=tm, dils=dils),
        grid=(b, n_tiles),
        in_specs=[
            pl.BlockSpec((None, tm, d), lambda bi, i: (bi, i, 0)),
            const((1, d)), const(wkv.shape), const((1, ng * gw)),
            const((1, d)), const(wb.shape), const((1, ng * gw)),
        ],
        out_specs=out_specs,
        out_shape=out_shape,
        scratch_shapes=[pltpu.VMEM((gw // LANES, tm, LANES), F32)],
        compiler_params=_params(("parallel", "arbitrary")),
        name="kvq_proj",
    )(x3, gkv, wkv, kgain, gb, wb, qgain)
    return outs[:ng], outs[ng:2 * ng], outs[2 * ng:3 * ng], outs[3 * ng], outs[3 * ng + 1:]


def _attn_prompt_kernel(q_ref, kp_ref, kc_ref, vp_ref, vc_ref, bias_ref, o_ref, lse_ref, *, nq):
    blk = ATTN_BLOCK
    first = jnp.where(pl.program_id(2) == 0, 0, 1)
    lane = lax.broadcasted_iota(jnp.int32, (blk, LANES), 1)
    low_half = lane < ATTN_HEAD_DIM
    ones_tile = jnp.ones((2 * blk, LANES), BF16)
    for rr, jb in [(rr, jb) for rr in range(q_ref.shape[0]) for jb in range(nq)]:
        q = q_ref[rr, jb * blk:(jb + 1) * blk, :]
        if jb == 0:
            kcat = jnp.concatenate([kp_ref[rr], kc_ref[rr, 0:blk, :]], axis=0)
            vcat = jnp.concatenate([vp_ref[rr], vc_ref[rr, 0:blk, :]], axis=0)
        else:
            kcat = kc_ref[rr, (jb - 1) * blk:(jb + 1) * blk, :]
            vcat = vc_ref[rr, (jb - 1) * blk:(jb + 1) * blk, :]
        ms, outs = [], []
        sums = jnp.ones((blk, LANES), F32)
        for j in range(GROUP_HEADS // 2):
            ps = slice(j * LANES, (j + 1) * LANES)
            qf = q[:, ps].astype(F32)
            kpair = kcat[:, ps]
            vext = jnp.concatenate([vcat[:, ps], ones_tile], axis=1)
            pair = []
            for half in range(2):
                h = 2 * j + half
                qh = jnp.where(low_half if half == 0 else ~low_half, qf, 0.0).astype(BF16)
                bias = bias_ref[first, h] if jb == 0 else bias_ref[1, h]
                st = lax.dot_general(kpair, qh, NT_DIMS, preferred_element_type=F32) + bias
                m = jnp.max(st, axis=0, keepdims=True)
                p = jnp.exp(st - m).T.astype(BF16)
                pv = jnp.dot(p, vext, preferred_element_type=F32)
                l = pv[:, LANES:]
                pair.append(pv[:, :LANES] * (1.0 / l))
                sums = jnp.where(lane == h, l, sums)
                ms.append(m)
            outs.append(jnp.where(low_half, pair[0], pair[1]))
        o_ref[rr, jb * blk:(jb + 1) * blk, :] = jnp.concatenate(outs, axis=1).astype(o_ref.dtype)
        m_t = jnp.concatenate(ms + [jnp.zeros((LANES - GROUP_HEADS, blk), F32)], axis=0).T
        lse_ref[rr, jb * blk:(jb + 1) * blk, :] = m_t + jnp.log(sums)


def _attn_prompt(q, k, v, bias, g):
    b, dil, s, gw = q.shape
    nb = s // ATTN_BLOCK
    assert s % ATTN_BLOCK == 0
    nq = min(ATTN_BLOCKS_PER_STEP, nb)
    rps = min(ATTN_BLOCKS_PER_STEP // nq, dil)
    assert nb % nq == 0 and dil % rps == 0
    rows = nq * ATTN_BLOCK
    cur = pl.BlockSpec((None, rps, rows, gw), lambda bi, r, j: (bi, r, j, 0))
    prev = pl.BlockSpec((None, rps, ATTN_BLOCK, gw),
                        lambda bi, r, j: (bi, r, jnp.maximum(j * nq - 1, 0), 0))
    return pl.pallas_call(
        functools.partial(_attn_prompt_kernel, nq=nq),
        grid=(b, dil // rps, nb // nq),
        in_specs=[
            cur, prev, cur, prev, cur,
            pl.BlockSpec(bias.shape, lambda bi, r, j: (0, 0, 0, 0)),
        ],
        out_specs=[
            pl.BlockSpec((None, rps, rows, gw), lambda bi, r, j: (bi, r, j, 0)),
            pl.BlockSpec((None, rps, rows, LANES), lambda bi, r, j: (bi, r, j, 0)),
        ],
        out_shape=[
            jax.ShapeDtypeStruct((b, dil, s, gw), BF16),
            jax.ShapeDtypeStruct((b, dil, s, LANES), F32),
        ],
        compiler_params=_params(("parallel", "parallel", "arbitrary")),
        name="attn_prompt_g%d" % g,
    )(q, k, k, v, v, bias)


def _head_expand_matrix():
    r = lax.broadcasted_iota(jnp.int32, (LANES, GROUP_HEADS * ATTN_HEAD_DIM), 0)
    c = lax.broadcasted_iota(jnp.int32, (LANES, GROUP_HEADS * ATTN_HEAD_DIM), 1)
    return (r == lax.shift_right_logical(c, HEAD_SHIFT)).astype(BF16)


def _load_token_order(ref, scr):
    dil, per, width = ref.shape
    if dil == 1:
        return ref[0].astype(F32)
    n_tiles = width // LANES
    for r in range(dil):
        val = ref[r].astype(F32)
        for c in range(n_tiles):
            scr[c, pl.ds(r, per, stride=dil), :] = val[:, c * LANES:(c + 1) * LANES]
    return jnp.concatenate([scr[c] for c in range(n_tiles)], axis=1)


def _merge_out_kernel(o0_ref, o1_ref, o2_ref, l0_ref, l1_ref, l2_ref, z_ref, x_ref, w_ref, y_ref,
                      *scratch):
    o_refs, l_refs = (o0_ref, o1_ref, o2_ref), (l0_ref, l1_ref, l2_ref)
    os_, ls, k = [], [], 0
    for o_ref, l_ref in zip(o_refs, l_refs):
        if o_ref.shape[0] == 1:
            os_.append(o_ref[0].astype(F32))
            ls.append(l_ref[0])
        else:
            os_.append(_load_token_order(o_ref, scratch[k]))
            ls.append(_load_token_order(l_ref, scratch[k + 1]))
            k += 2
    lmax = jnp.maximum(jnp.maximum(ls[0], ls[1]), ls[2])
    es = [jnp.exp(l - lmax) for l in ls]
    tot = es[0] + es[1] + es[2]
    expand = _head_expand_matrix()
    o = jnp.zeros(os_[0].shape, F32)
    for e, og in zip(es, os_):
        wexp = jnp.dot((e / tot).astype(BF16), expand, preferred_element_type=F32)
        o = o + wexp * og
    zf = z_ref[...].astype(F32)
    a = (o * (zf * _sigmoid(zf))).astype(BF16)
    y_ref[...] = x_ref[...] + jnp.dot(a, w_ref[...], preferred_element_type=F32)


def _merge_out(outs, lses, z3, x3, w, *, tm):
    b, t, d = x3.shape
    gw = w.shape[0]
    assert t % tm == 0
    row = lambda width: pl.BlockSpec((None, tm, width), lambda bi, i: (bi, i, 0))
    by_residue = lambda a: pl.BlockSpec((None, a.shape[1], tm // a.shape[1], a.shape[3]),
                                        lambda bi, i: (bi, 0, i, 0))
    scratch = []
    for o in outs:
        if o.shape[1] > 1:
            scratch += [pltpu.VMEM((gw // LANES, tm, LANES), F32), pltpu.VMEM((1, tm, LANES), F32)]
    return pl.pallas_call(
        _merge_out_kernel,
        grid=(b, t // tm),
        in_specs=[by_residue(o) for o in outs] + [by_residue(l) for l in lses]
        + [row(gw), row(d), pl.BlockSpec((gw, d), lambda bi, i: (0, 0))],
        out_specs=row(d),
        out_shape=jax.ShapeDtypeStruct((b, t, d), F32),
        scratch_shapes=scratch,
        compiler_params=_params(("parallel", "parallel")),
        name="merge_out",
    )(*outs, *lses, z3, x3, w)


def _attn_sample_kernel(q_ref, kn_ref, vn_ref, z_ref, c0_ref, c1_ref, c2_ref,
                        b0_ref, b1_ref, b2_ref, a_ref, ks0, vs0, ks1, vs1, ks2, vs2, *, s_new):
    gw = GROUP_HEADS * ATTN_HEAD_DIM
    rows = GROUP_HEADS * s_new
    r = lax.broadcasted_iota(jnp.int32, (rows, gw), 0)
    c = lax.broadcasted_iota(jnp.int32, (rows, gw), 1)
    head_mask = (lax.shift_right_logical(r, int(math.log2(s_new)))
                 == lax.shift_right_logical(c, HEAD_SHIFT))
    caches = ((c0_ref, b0_ref, ks0, vs0), (c1_ref, b1_ref, ks1, vs1), (c2_ref, b2_ref, ks2, vs2))
    pad_rows = jnp.zeros((LANES - s_new, gw), F32)
    outs, lses = [], []
    for g, (c_ref, b_ref, ks, vs) in enumerate(caches):
        buf_len = c_ref.shape[2]
        cs = slice(g * gw, (g + 1) * gw)
        ks[:, :buf_len] = c_ref[0].astype(BF16)
        vs[:, :buf_len] = c_ref[1].astype(BF16)
        ks[:, buf_len:] = jnp.concatenate([kn_ref[:, cs].astype(F32), pad_rows], axis=0).T.astype(BF16)
        vs[:, buf_len:] = jnp.concatenate([vn_ref[:, cs].astype(F32), pad_rows], axis=0).T.astype(BF16)
        qg = q_ref[:, cs].astype(F32)
        qbd = jnp.where(head_mask, jnp.concatenate([qg] * GROUP_HEADS, axis=0), 0.0).astype(BF16)
        s = jnp.dot(qbd, ks[...], preferred_element_type=F32) + b_ref[...]
        m = jnp.max(s, axis=1, keepdims=True)
        p = jnp.exp(s - m)
        l = jnp.sum(p, axis=1, keepdims=True)
        outs.append(lax.dot_general(p.astype(BF16), vs[...], NT_DIMS, preferred_element_type=F32) / l)
        lses.append(m + jnp.log(l))
    lmax = jnp.maximum(jnp.maximum(lses[0], lses[1]), lses[2])
    es = [jnp.exp(l - lmax) for l in lses]
    tot = es[0] + es[1] + es[2]
    o = jnp.zeros((rows, gw), F32)
    for e, og in zip(es, outs):
        o = o + (e / tot) * og
    o = jnp.where(head_mask, o, 0.0)
    folded = o[0:s_new, :]
    for h in range(1, GROUP_HEADS):
        folded = folded + o[h * s_new:(h + 1) * s_new, :]
    zf = z_ref[...].astype(F32)
    a_ref[...] = (folded * (zf * _sigmoid(zf))).astype(BF16)


def _attn_sample(q, kn, vn, z, caches, biases):
    b, s_new, qw = q.shape
    gw = GROUP_HEADS * ATTN_HEAD_DIM
    assert s_new % 8 == 0
    cache2 = [jnp.transpose(cb, (0, 2, 3, 4, 1)).reshape(b, 2, gw, cb.shape[1]) for cb in caches]
    in_specs = [
        pl.BlockSpec((None, s_new, qw), lambda bi: (bi, 0, 0)),
        pl.BlockSpec((None, s_new, qw), lambda bi: (bi, 0, 0)),
        pl.BlockSpec((None, s_new, qw), lambda bi: (bi, 0, 0)),
        pl.BlockSpec((None, s_new, gw), lambda bi: (bi, 0, 0)),
    ]
    for cb in cache2:
        in_specs.append(pl.BlockSpec((None, 2, gw, cb.shape[3]), lambda bi: (bi, 0, 0, 0)))
    for bt in biases:
        in_specs.append(pl.BlockSpec(bt.shape, lambda bi: (0, 0)))
    scratch = []
    for cb in cache2:
        scratch += [pltpu.VMEM((gw, cb.shape[3] + LANES), BF16)] * 2
    return pl.pallas_call(
        functools.partial(_attn_sample_kernel, s_new=s_new),
        grid=(b,),
        in_specs=in_specs,
        out_specs=pl.BlockSpec((None, s_new, gw), lambda bi: (bi, 0, 0)),
        out_shape=jax.ShapeDtypeStruct((b, s_new, gw), BF16),
        scratch_shapes=scratch,
        compiler_params=_params(("arbitrary",)),
        name="attn_sample",
    )(q, kn, vn, z, *cache2, *biases)


def _matmul_residual_kernel(a_ref, x_ref, w_ref, y_ref):
    y_ref[...] = x_ref[...] + jnp.dot(a_ref[...], w_ref[...], preferred_element_type=F32)


def _matmul_residual(a2, x2, w, *, tm):
    m, d = x2.shape
    kdim = a2.shape[1]
    assert m % tm == 0
    return pl.pallas_call(
        _matmul_residual_kernel,
        grid=(m // tm,),
        in_specs=[
            pl.BlockSpec((tm, kdim), lambda i: (i, 0)),
            pl.BlockSpec((tm, d), lambda i: (i, 0)),
            pl.BlockSpec((kdim, d), lambda i: (0, 0)),
        ],
        out_specs=pl.BlockSpec((tm, d), lambda i: (i, 0)),
        out_shape=jax.ShapeDtypeStruct((m, d), F32),
        compiler_params=_params(("parallel",)),
        name="matmul_residual",
    )(a2, x2, w)


def _t5_bucket_np(dist):
    exact = N_BUCKETS // 2
    d = np.maximum(dist, 1).astype(np.float32)
    large = exact + (np.log(d / np.float32(exact)) / np.float32(math.log(MAX_DISTANCE / exact))
                     * np.float32(N_BUCKETS - exact)).astype(np.int32)
    return np.where(dist < exact, dist, np.minimum(large, N_BUCKETS - 1)).astype(np.int32)


def _bias_by_step(rel_bias_g, jmax, dil):
    return rel_bias_g.astype(F32)[_t5_bucket_np(np.arange(jmax + 1) * dil)]


def _prompt_bias(rel_bias_g, win, dil):
    jmax = win // dil
    assert jmax == ATTN_BLOCK
    qi = np.arange(ATTN_BLOCK)[:, None]
    kj = np.arange(2 * ATTN_BLOCK)[None, :]
    rel = qi + ATTN_BLOCK - kj
    band = (rel >= 0) & (rel <= jmax)
    bvec = _bias_by_step(rel_bias_g, jmax, dil)
    period = 2 * ATTN_BLOCK + 1
    base = jnp.concatenate([bvec[::-1], jnp.zeros((period - jmax - 1, GROUP_HEADS), F32)], axis=0).T
    bias = jnp.tile(base, (1, ATTN_BLOCK))[:, :2 * ATTN_BLOCK * ATTN_BLOCK]
    bias = bias.reshape(GROUP_HEADS, ATTN_BLOCK, 2 * ATTN_BLOCK)
    rest = jnp.where(band[None], bias, -jnp.inf)
    first = jnp.where((band & (kj >= ATTN_BLOCK))[None], bias, -jnp.inf)
    return jnp.swapaxes(jnp.stack([first, rest], axis=0), 2, 3)


def _sample_bias(rel_bias_g, win, dil, buf_len, s_new):
    jmax = win // dil
    assert buf_len == jmax * dil
    width = buf_len + LANES
    bvec = _bias_by_step(rel_bias_g, jmax, dil)
    gaps = jnp.full((jmax + 1, dil - 1, GROUP_HEADS), -jnp.inf, F32)
    by_dist = jnp.concatenate([bvec[:, None, :], gaps], axis=1).reshape((jmax + 1) * dil, GROUP_HEADS)
    padded = jnp.pad(by_dist[::-1], ((s_new, width), (0, 0)), constant_values=-jnp.inf)
    rows = [padded[dil - 1 - s + s_new:dil - 1 - s + s_new + width] for s in range(s_new)]
    table = jnp.transpose(jnp.stack(rows, axis=0), (2, 0, 1))
    return table.reshape(GROUP_HEADS * s_new, width)


def _layer_a(x3, state, weights, *, chunk, pad_to):
    norm_a, wt, wgt, bg, hgain, wout = weights
    b, t, d = x3.shape
    H = MLSTM_HEADS
    di = wout.shape[0]
    dh = di // H
    k_scale = dh ** -0.5
    k_idx = 1
    if t > SHORT_SCAN_CHUNK:
        assert pad_to == t
        p3, gc3, gr3, kt3 = _inproj(x3, norm_a, wt, wgt, bg, tm=min(INPROJ_ROWS, t), tn=di,
                                    n_heads=H, k_transposed=True, k_scale=k_scale)
        col_idx = (0, 1, 2)
        xin = x3
    else:
        m = b * t
        extra = pad_to - t
        p, gc, gr = _inproj(x3.reshape(1, m, d), norm_a, wt, wgt, bg, tm=m, tn=di, n_heads=H,
                            k_transposed=False, k_scale=k_scale)
        col_idx = (0, 2, 3)
        p3 = p.reshape(b, t, -1)
        k3 = p3[:, :, k_idx * di:(k_idx + 1) * di]
        kt3 = jnp.pad(jnp.swapaxes(k3, 1, 2), ((0, 0), (0, 0), (0, extra)))
        p3 = jnp.pad(p3, ((0, 0), (0, extra), (0, 0)))
        xin = jnp.pad(x3, ((0, 0), (0, extra), (0, 0)))
        lane = np.arange(LANES)
        pad_col = np.where(lane < H, -np.inf, 0.0).astype(np.float32)
        gc3 = jnp.concatenate([gc.reshape(b, t, LANES),
                               jnp.broadcast_to(pad_col, (b, extra, LANES))], axis=1)
        pad_row = np.where(np.arange(8) < H, -np.inf, 0.0).astype(np.float32)[:, None, None]
        gr3 = jnp.swapaxes(jnp.concatenate([gr.reshape(8, b, t),
                                            jnp.broadcast_to(pad_row, (8, b, extra))], axis=2), 0, 1)
    short = t <= SHORT_SCAN_CHUNK
    xo, c_out, n_out, m_out = _scan(p3, col_idx, kt3, gc3, gr3, xin, wout, hgain, state,
                                    chunk=chunk, n_heads=H, fuse_out=not short)
    xo = xo[:, :t]
    if short:
        xo = _matmul_residual(xo.reshape(b * t, di), x3.reshape(b * t, d), wout,
                              tm=b * t).reshape(b, t, d)
    return xo, c_out, n_out, m_out[:, :H, 0]


def kernel(x_prompt, x_sample, state_mlstm_C, state_mlstm_n, state_mlstm_m, cache_kv_w128, cache_kv_w512, cache_kv_w2048, norm_a, w_in_a, b_gates_a, hnorm_a, w_out_a, norm_kv, w_kv, k_norm, norm_b, w_in_b, q_norm, rel_bias, w_out_b):
    H = MLSTM_HEADS
    bp, tp, d = x_prompt.shape
    bs, ts, _ = x_sample.shape
    di = w_out_a.shape[1]
    dh = di // H
    gw = GROUP_HEADS * ATTN_HEAD_DIM
    qw = len(GROUPS) * gw
    caches = (cache_kv_w128, cache_kv_w512, cache_kv_w2048)
    assert norm_a.shape[0] == 1 and norm_b.shape[0] == 1, "one mLSTM layer, one attention layer"
    for cb, (win, _) in zip(caches, GROUPS):
        assert cb.shape[1] == win, "window buffers must hold a full window"

    w_at = w_in_a[0].T
    wt_a = w_at[:5 * di].astype(BF16)
    wgt = jnp.pad(w_at[5 * di:].astype(F32), ((0, LANES - 2 * H), (0, 0)))
    bg = jnp.pad(b_gates_a[0].astype(F32), (0, LANES - 2 * H))[None, :]
    na = norm_a[0].astype(F32)[None, :]
    hgain = hnorm_a[0].astype(F32)[None, :]
    wout_a = w_out_a[0].astype(BF16)
    weights_a = (na, wt_a, wgt, bg, hgain, wout_a)

    xp1, c_p, n_p, m_p = _layer_a(x_prompt, None, weights_a, chunk=SCAN_CHUNK, pad_to=tp)
    m0 = jnp.pad(jnp.broadcast_to(state_mlstm_m[0].astype(F32)[:, :, None], (bs, H, LANES)),
                 ((0, 0), (0, 8 - H), (0, 0)))
    state_s = (state_mlstm_C[0].astype(F32), state_mlstm_n[0].astype(F32), m0)
    assert ts <= SHORT_SCAN_CHUNK
    xs1, c_s, n_s, m_s = _layer_a(x_sample, state_s, weights_a, chunk=SHORT_SCAN_CHUNK,
                                  pad_to=SHORT_SCAN_CHUNK)

    nkv = norm_kv.astype(F32)[None, :]
    wkv = w_kv.astype(BF16)
    kgain = jnp.tile(k_norm.astype(F32), qw // ATTN_HEAD_DIM)[None, :]
    rows_p = [min(win, tp) for win, _ in GROUPS]
    dils = tuple(dil for _, dil in GROUPS)
    nb_ = norm_b[0].astype(F32)[None, :]
    wb = w_in_b[0].astype(BF16)
    qgain = jnp.tile(q_norm[0].astype(F32), qw // ATTN_HEAD_DIM)[None, :]
    qscale = ATTN_HEAD_DIM ** -0.5
    kp, vp, qp, zp, (kv128_p, kv512_p, kv2048_p) = _kvq_proj(
        xp1, nkv, wkv, kgain, nb_, wb, qgain, qscale=qscale, tm=KVQ_ROWS, tail_rows=rows_p,
        dils=dils)
    xs1_flat = xs1.reshape(bs * ts, d)
    by_seq = lambda a: a.reshape(bs, ts, a.shape[-1])
    ks, vs, kv128_s, kv512_s, kv2048_s = map(by_seq, _proj_headnorm(
        xs1_flat, nkv, wkv, kgain, n_norm=qw, scale=1.0, group_rows=True))

    wout_b = w_out_b[0].astype(BF16)
    qs, zs = map(by_seq, _proj_headnorm(xs1_flat, nb_, wb, qgain, n_norm=qw, scale=qscale))

    outs, lses = [], []
    for g, (win, dil) in enumerate(GROUPS):
        bias = _prompt_bias(rel_bias[:, g * GROUP_HEADS:(g + 1) * GROUP_HEADS], win, dil)
        o, lse = _attn_prompt(qp[g], kp[g], vp[g], bias, g)
        outs.append(o)
        lses.append(lse)
    y_p = _merge_out(outs, lses, zp, xp1, wout_b, tm=MERGE_ROWS)

    sbias = [_sample_bias(rel_bias[:, g * GROUP_HEADS:(g + 1) * GROUP_HEADS], win, dil,
                          caches[g].shape[1], ts) for g, (win, dil) in enumerate(GROUPS)]
    a_s = _attn_sample(qs, ks, vs, zs, caches, sbias)
    y_s = _matmul_residual(a_s.reshape(bs * ts, gw), xs1.reshape(bs * ts, d), wout_b,
                           tm=bs * ts).reshape(bs, ts, d)

    kv5 = lambda a: a.reshape(a.shape[0], a.shape[1], 2, GROUP_HEADS, ATTN_HEAD_DIM)
    return (y_p, y_s, c_p[None], n_p[None], m_p[None], c_s[None], n_s[None], m_s[None],
            kv5(kv128_p), kv5(kv512_p), kv5(kv2048_p), kv5(kv128_s), kv5(kv512_s), kv5(kv2048_s))
```
